```python
import math
import jax
import jax.numpy as jnp
from jax import lax
import numpy as np

D_MODEL = 1024
BATCH = 1
SEQ = 16384
DEPTH = 1
DEC_BATCH = 32
DEC_SEQ = 4
PAST_LEN = 16384
PAGE_SIZE = 128

GDN_HEADS = 8
GDN_DK = 128
GDN_DV = 128
GDN_CONV = 4
GDN_CHUNK = 64
DSA_HEADS = 8
DSA_HD = 128
IDX_HEADS = 8
IDX_DIM = 64
TOPK_MAX = 256
Q_BLOCK = 128
REL_BUCKETS = 32
REL_MAX_DIST = 128
D_FF = 2816
FFN_CONV = 3
EPS = 1e-6

GDN_QK = GDN_HEADS * GDN_DK
GDN_V = GDN_HEADS * GDN_DV
GDN_CONV_CH = 2 * GDN_QK + GDN_V
DSA_W = DSA_HEADS * DSA_HD
IDX_W_SCALE = (IDX_HEADS * IDX_DIM) ** -0.5
IN_SPLITS = (GDN_CONV_CH, GDN_V, GDN_HEADS, GDN_HEADS, DSA_W, DSA_W, DSA_W,
             IDX_HEADS * IDX_DIM, IDX_DIM, IDX_HEADS, D_MODEL, D_MODEL)
IN_PROJ_DIM = sum(IN_SPLITS)

kernel_name = "hybrid_gdn_dsa_convffn_step"


def rmsnorm(x, w):
    xf = x.astype(jnp.float32)
    return xf * lax.rsqrt(jnp.mean(xf * xf, -1, keepdims=True) + EPS) * w.astype(jnp.float32)


def layernorm(x, w, b):
    xc = x - jnp.mean(x, -1, keepdims=True)
    return xc * lax.rsqrt(jnp.mean(xc * xc, -1, keepdims=True) + EPS) * w + b


def l2norm(x):
    return x * lax.rsqrt(jnp.sum(x * x, -1, keepdims=True) + EPS)


def causal_dwconv(x, buf, w):
    width = w.shape[0]
    t = x.shape[1]
    xp = jnp.concatenate([buf.astype(x.dtype), x], axis=1)
    y = xp[:, 0:t] * w[0]
    for i in range(1, width):
        y = y + xp[:, i:i + t] * w[i]
    return y, xp[:, t:]


def ada_mod(c, w_ada, b_ada):
    m = jax.nn.silu(c.astype(jnp.float32)) @ w_ada + b_ada
    return jnp.split(m[:, None, :], 6, axis=-1)


def rel_bucket(dist):
    n = jnp.maximum(dist, 0)
    max_exact = REL_BUCKETS // 2
    nf = jnp.maximum(n, max_exact).astype(jnp.float32)
    large = max_exact + (jnp.log(nf / max_exact) / math.log(REL_MAX_DIST / max_exact)
                         * (REL_BUCKETS - max_exact)).astype(jnp.int32)
    large = jnp.minimum(large, REL_BUCKETS - 1)
    return jnp.where(n < max_exact, n, large)


def gated_delta_chunked(q, k, v, g, beta, S0):
    B, T, H, dk = k.shape
    dv = v.shape[-1]
    C = GDN_CHUNK
    N = T // C

    def chunks(a):
        return jnp.moveaxis(a.reshape((B, N, C) + a.shape[2:]), 2, 3)

    qc, kc, vc = chunks(q), chunks(k), chunks(v)
    gc, bc = chunks(g), chunks(beta)
    G = jnp.cumsum(gc, axis=-1)
    tril = jnp.tril(jnp.ones((C, C), dtype=bool))
    strict = jnp.tril(jnp.ones((C, C), dtype=bool), -1)
    decay = jnp.exp(jnp.where(tril, G[..., :, None] - G[..., None, :], -jnp.inf))
    A = jnp.where(strict, bc[..., :, None] * jnp.einsum("bnhid,bnhjd->bnhij", kc, kc) * decay, 0.0)
    rhs = jnp.concatenate([vc * bc[..., None], kc * (bc * jnp.exp(G))[..., None]], axis=-1)
    sol = lax.linalg.triangular_solve(A + jnp.eye(C, dtype=A.dtype), rhs,
                                      left_side=True, lower=True, unit_diagonal=True)
    U, W = sol[..., :dv], sol[..., dv:]
    qk = jnp.einsum("bnhid,bnhjd->bnhij", qc, kc) * decay
    q_dec = qc * jnp.exp(G)[..., None]
    k_dec = kc * jnp.exp(G[..., -1:] - G)[..., None]
    g_end = jnp.exp(G[..., -1])

    def step(S, xs):
        U_n, W_n, qk_n, qd_n, kd_n, ge_n = xs
        u = U_n - jnp.einsum("bhcd,bhde->bhce", W_n, S)
        o = jnp.einsum("bhcd,bhde->bhce", qd_n, S) + jnp.einsum("bhij,bhje->bhie", qk_n, u)
        S = S * ge_n[..., None, None] + jnp.einsum("bhcd,bhce->bhde", kd_n, u)
        return S, o

    xs = tuple(jnp.moveaxis(a, 1, 0) for a in (U, W, qk, q_dec, k_dec, g_end))
    S, o = lax.scan(step, S0.astype(jnp.float32), xs)
    o = jnp.moveaxis(o, (0, 3), (1, 2)).reshape(B, T, H, dv)
    return o, S


def gated_delta_recurrent(q, k, v, g, beta, S0):
    def step(S, xs):
        q_t, k_t, v_t, g_t, b_t = xs
        S = S * jnp.exp(g_t)[..., None, None]
        u = b_t[..., None] * (v_t - jnp.einsum("bhd,bhde->bhe", k_t, S))
        S = S + k_t[..., :, None] * u[..., None, :]
        return S, jnp.einsum("bhd,bhde->bhe", q_t, S)

    xs = tuple(jnp.moveaxis(a, 1, 0) for a in (q, k, v, g, beta))
    S, o = lax.scan(step, S0.astype(jnp.float32), xs)
    return jnp.moveaxis(o, 0, 1), S


def indexer_scores(q_idx, w_idx, k_idx):
    s = jax.nn.relu(jnp.einsum("bqhd,bsd->bqhs", q_idx, k_idx.astype(jnp.float32)))
    return jnp.einsum("bqhs,bqh->bqs", s, w_idx)


def sparse_attend(q, k_sel, v_sel, dist, rel_bias):
    logits = jnp.einsum("bqhd,bqkhd->bhqk", q, k_sel.astype(jnp.float32)) * (DSA_HD ** -0.5)
    bias = rel_bias.astype(jnp.float32)[rel_bucket(dist)]
    logits = logits + jnp.moveaxis(bias, -1, 1)
    logits = jnp.where((dist >= 0)[:, None], logits, -1e30)
    p = jax.nn.softmax(logits, axis=-1)
    return jnp.einsum("bhqk,bqkhd->bqhd", p, v_sel.astype(jnp.float32))


def dsa_prompt(q, k, v, q_idx, k_idx, w_idx, rel_bias):
    B, T = q.shape[:2]
    topk = min(TOPK_MAX, T // 4)
    nblk = T // Q_BLOCK
    pos = jnp.arange(T, dtype=jnp.int32)
    take = jax.vmap(lambda rows, idx: rows[idx])

    def blocks(a):
        return jnp.moveaxis(a.reshape((B, nblk, Q_BLOCK) + a.shape[2:]), 1, 0)

    def one_block(xs):
        qb, qib, wb, pb = xs
        sc = indexer_scores(qib, wb, k_idx)
        sc = jnp.where((pos[None, :] <= pb[:, None])[None], sc, -jnp.inf)
        _, sel = lax.top_k(sc, topk)
        dist = pb[None, :, None] - sel
        return sparse_attend(qb, take(k, sel), take(v, sel), dist, rel_bias)

    out = lax.map(one_block, (blocks(q), blocks(q_idx), blocks(w_idx), pos.reshape(nblk, Q_BLOCK)))
    return jnp.moveaxis(out, 0, 1).reshape(B, T, DSA_HEADS, DSA_HD)


def dsa_sample(q, k_new, v_new, q_idx, k_idx_new, w_idx, cache_k, cache_v, cache_idx_k, page_table, rel_bias):
    Bd, Tn = q.shape[:2]
    n_pages = page_table.shape[1]
    past = n_pages * PAGE_SIZE
    L = past + Tn
    topk = min(TOPK_MAX, L // 4)
    k_idx_past = cache_idx_k[page_table].reshape(Bd, past, IDX_DIM).astype(jnp.float32)
    k_idx_all = jnp.concatenate([k_idx_past, k_idx_new], axis=1)
    qpos = past + jnp.arange(Tn, dtype=jnp.int32)
    sc = indexer_scores(q_idx, w_idx, k_idx_all)
    sc = jnp.where((jnp.arange(L, dtype=jnp.int32)[None, :] <= qpos[:, None])[None], sc, -jnp.inf)
    _, sel = lax.top_k(sc, topk)
    in_past = (sel < past)[..., None, None]
    sp = jnp.minimum(sel, past - 1)
    phys = jax.vmap(lambda pt, i: pt[i])(page_table, sp // PAGE_SIZE)
    off = sp % PAGE_SIZE
    sn = jnp.clip(sel - past, 0, Tn - 1)
    take = jax.vmap(lambda rows, idx: rows[idx])
    k_sel = jnp.where(in_past, cache_k[phys, off].astype(jnp.float32), take(k_new, sn))
    v_sel = jnp.where(in_past, cache_v[phys, off].astype(jnp.float32), take(v_new, sn))
    dist = qpos[None, :, None] - sel
    return sparse_attend(q, k_sel, v_sel, dist, rel_bias)


def trunk_layer(x, c, p, rel_bias, gdn_buf, gdn_S, ffn_buf, paged):
    B, T, _ = x.shape
    sh1, sc1, g1, sh2, sc2, g2 = ada_mod(c, p["w_ada"], p["b_ada"])
    h = rmsnorm(x, p["norm1_w"]) * (1.0 + sc1) + sh1
    z = h @ p["w_in"]
    cuts = [int(i) for i in np.cumsum(IN_SPLITS)[:-1]]
    (conv_in, gz, b_raw, a_raw, qb, kb, vb, qi, ki, wi, gate_a, gate_b) = jnp.split(z, cuts, axis=-1)

    conv_out, new_gdn_buf = causal_dwconv(conv_in, gdn_buf, p["gdn_conv_w"])
    conv_out = jax.nn.silu(conv_out)
    qa, ka, va = jnp.split(conv_out, [GDN_QK, 2 * GDN_QK], axis=-1)
    qa = l2norm(qa.reshape(B, T, GDN_HEADS, GDN_DK)) * (GDN_DK ** -0.5)
    ka = l2norm(ka.reshape(B, T, GDN_HEADS, GDN_DK))
    va = va.reshape(B, T, GDN_HEADS, GDN_DV)
    beta = jax.nn.sigmoid(b_raw)
    g = -jnp.exp(p["gdn_A_log"]) * jax.nn.softplus(a_raw + p["gdn_dt_bias"])
    if paged is None:
        oa, new_S = gated_delta_chunked(qa, ka, va, g, beta, gdn_S)
    else:
        oa, new_S = gated_delta_recurrent(qa, ka, va, g, beta, gdn_S)
    oa = rmsnorm(oa, p["gdn_norm_w"]) * jax.nn.silu(gz.reshape(B, T, GDN_HEADS, GDN_DV))
    ya = oa.reshape(B, T, GDN_V) @ p["w_branch_a"]

    qb = qb.reshape(B, T, DSA_HEADS, DSA_HD)
    kb = kb.reshape(B, T, DSA_HEADS, DSA_HD)
    vb = vb.reshape(B, T, DSA_HEADS, DSA_HD)
    qi = qi.reshape(B, T, IDX_HEADS, IDX_DIM)
    ki = layernorm(ki, p["idx_knorm_w"], p["idx_knorm_b"])
    wi = wi * IDX_W_SCALE
    if paged is None:
        ob = dsa_prompt(qb, kb, vb, qi, ki, wi, rel_bias)
    else:
        ob = dsa_sample(qb, kb, vb, qi, ki, wi, paged[0], paged[1], paged[2], paged[3], rel_bias)
    yb = ob.reshape(B, T, DSA_W) @ p["w_branch_b"]

    mixed = jax.nn.sigmoid(gate_a) * ya + jax.nn.sigmoid(gate_b) * yb
    x = x + g1 * (mixed @ p["w_out"])

    h = rmsnorm(x, p["norm2_w"]) * (1.0 + sc2) + sh2
    u = h @ p["w_up"]
    u_c, new_ffn_buf = causal_dwconv(u, ffn_buf, p["ffn_conv_w"])
    ug, uv = jnp.split(u_c + p["ffn_conv_b"], 2, axis=-1)
    x = x + g2 * ((jax.nn.silu(ug) * uv) @ p["w_down"])
    return x, (kb, vb, ki, new_S, new_gdn_buf, new_ffn_buf)


def setup_inputs(seed: int = 0) -> dict:
    key = jax.random.key(seed)
    ks = jax.random.split(key, 40)
    f32 = jnp.float32
    n_pages = PAST_LEN // PAGE_SIZE
    n_phys = (DEC_BATCH * n_pages * 5) // 4

    def nrm(k, shape, s):
        return jax.random.normal(k, shape, f32) * s

    page_table = jax.random.permutation(ks[0], n_phys)[: DEC_BATCH * n_pages]
    page_table = page_table.reshape(DEC_BATCH, n_pages).astype(jnp.int32)
    dt = jnp.exp(jax.random.uniform(ks[1], (DEPTH, GDN_HEADS), f32, math.log(1e-3), math.log(1e-1)))
    dt_bias = dt + jnp.log(-jnp.expm1(-dt))
    return {
        "x_prompt": nrm(ks[2], (BATCH, SEQ, D_MODEL), 1.0),
        "x_sample": nrm(ks[3], (DEC_BATCH, DEC_SEQ, D_MODEL), 1.0),
        "cache_k": nrm(ks[4], (DEPTH, n_phys, PAGE_SIZE, DSA_HEADS, DSA_HD), 1.0),
        "cache_v": nrm(ks[5], (DEPTH, n_phys, PAGE_SIZE, DSA_HEADS, DSA_HD), 1.0),
        "cache_idx_k": nrm(ks[6], (DEPTH, n_phys, PAGE_SIZE, IDX_DIM), 1.0),
        "state_gdn": nrm(ks[7], (DEPTH, DEC_BATCH, GDN_HEADS, GDN_DK, GDN_DV), GDN_DK ** -0.5),
        "state_gdn_conv": nrm(ks[8], (DEPTH, DEC_BATCH, GDN_CONV - 1, GDN_CONV_CH), 1.0),
        "state_ffn_conv": nrm(ks[9], (DEPTH, DEC_BATCH, FFN_CONV - 1, 2 * D_FF), 1.0),
        "page_table": page_table,
        "c_prompt": nrm(ks[10], (BATCH, D_MODEL), 1.0),
        "c_sample": nrm(ks[11], (DEC_BATCH, D_MODEL), 1.0),
        "w_ada": nrm(ks[12], (DEPTH, D_MODEL, 6 * D_MODEL), 0.5 * D_MODEL ** -0.5),
        "b_ada": nrm(ks[13], (DEPTH, 6 * D_MODEL), 0.02),
        "norm1_w": 1.0 + nrm(ks[14], (DEPTH, D_MODEL), 0.05),
        "w_in": nrm(ks[15], (DEPTH, D_MODEL, IN_PROJ_DIM), D_MODEL ** -0.5),
        "gdn_conv_w": nrm(ks[16], (DEPTH, GDN_CONV, GDN_CONV_CH), GDN_CONV ** -0.5),
        "gdn_A_log": jnp.log(jax.random.uniform(ks[17], (DEPTH, GDN_HEADS), f32, 1.0, 16.0)),
        "gdn_dt_bias": dt_bias,
        "gdn_norm_w": 1.0 + nrm(ks[18], (DEPTH, GDN_DV), 0.05),
        "idx_knorm_w": 1.0 + nrm(ks[19], (DEPTH, IDX_DIM), 0.05),
        "idx_knorm_b": nrm(ks[20], (DEPTH, IDX_DIM), 0.02),
        "w_branch_a": nrm(ks[21], (DEPTH, GDN_V, D_MODEL), GDN_V ** -0.5),
        "w_branch_b": nrm(ks[22], (DEPTH, DSA_W, D_MODEL), DSA_W ** -0.5),
        "w_out": nrm(ks[23], (DEPTH, D_MODEL, D_MODEL), D_MODEL ** -0.5),
        "norm2_w": 1.0 + nrm(ks[24], (DEPTH, D_MODEL), 0.05),
        "w_up": nrm(ks[25], (DEPTH, D_MODEL, 2 * D_FF), D_MODEL ** -0.5),
        "ffn_conv_w": nrm(ks[26], (DEPTH, FFN_CONV, 2 * D_FF), FFN_CONV ** -0.5),
        "ffn_conv_b": nrm(ks[27], (DEPTH, 2 * D_FF), 0.02),
        "w_down": nrm(ks[28], (DEPTH, D_FF, D_MODEL), D_FF ** -0.5),
        "rel_bias": nrm(ks[29], (REL_BUCKETS, DSA_HEADS), 0.5),
        "final_norm_w": 1.0 + nrm(ks[30], (D_MODEL,), 0.05),
    }


def reference(x_prompt, x_sample, cache_k, cache_v, cache_idx_k, state_gdn, state_gdn_conv, state_ffn_conv,
              page_table, c_prompt, c_sample, w_ada, b_ada, norm1_w, w_in, gdn_conv_w, gdn_A_log, gdn_dt_bias,
              gdn_norm_w, idx_knorm_w, idx_knorm_b, w_branch_a, w_branch_b, w_out, norm2_w, w_up, ffn_conv_w,
              ffn_conv_b, w_down, rel_bias, final_norm_w):
    f32 = jnp.float32
    xp = x_prompt.astype(f32)
    xs = x_sample.astype(f32)
    bp = xp.shape[0]
    st_p, st_s = [], []
    for l in range(DEPTH):
        p = {"w_ada": w_ada[l], "b_ada": b_ada[l], "norm1_w": norm1_w[l], "w_in": w_in[l],
             "gdn_conv_w": gdn_conv_w[l], "gdn_A_log": gdn_A_log[l], "gdn_dt_bias": gdn_dt_bias[l],
             "gdn_norm_w": gdn_norm_w[l], "idx_knorm_w": idx_knorm_w[l], "idx_knorm_b": idx_knorm_b[l],
             "w_branch_a": w_branch_a[l], "w_branch_b": w_branch_b[l], "w_out": w_out[l],
             "norm2_w": norm2_w[l], "w_up": w_up[l], "ffn_conv_w": ffn_conv_w[l],
             "ffn_conv_b": ffn_conv_b[l], "w_down": w_down[l]}
        xp, sp = trunk_layer(xp, c_prompt, p, rel_bias,
                             jnp.zeros((bp, GDN_CONV - 1, GDN_CONV_CH), f32),
                             jnp.zeros((bp, GDN_HEADS, GDN_DK, GDN_DV), f32),
                             jnp.zeros((bp, FFN_CONV - 1, 2 * D_FF), f32), None)
        xs, ss = trunk_layer(xs, c_sample, p, rel_bias, state_gdn_conv[l], state_gdn[l], state_ffn_conv[l],
                             (cache_k[l], cache_v[l], cache_idx_k[l], page_table))
        st_p.append(sp)
        st_s.append(ss)

    def stk(sts, i):
        return jnp.stack([s[i] for s in sts])

    y_prompt = rmsnorm(xp, final_norm_w)
    y_sample = rmsnorm(xs, final_norm_w)
    return (y_prompt, y_sample,
            stk(st_p, 0), stk(st_p, 1), stk(st_p, 2),
            stk(st_s, 0), stk(st_s, 1), stk(st_s, 2),
            stk(st_p, 3), stk(st_s, 3),
            stk(st_p, 4), stk(st_s, 4),
            stk(st_p, 5), stk(st_s, 5))
```

```python
import functools
import math

import jax
import jax.numpy as jnp
import numpy as np
from jax import lax
from jax.experimental import pallas as pl
from jax.experimental.pallas import tpu as pltpu

F32 = jnp.float32
BF16 = jnp.bfloat16
I32 = jnp.int32
HI = lax.Precision.HIGHEST

D_MODEL = 1024
N_HEADS = 8
HEAD_DIM = 128
IDX_HEADS = 8
IDX_DIM = 64
TOPK_MAX = 256
D_FF = 2816
GDN_CONV = 4
FFN_CONV = 3
GDN_CHUNK = 64
PAGE = 128
REL_BUCKETS = 32
REL_MAX_DIST = 128
EPS = 1e-6
IDX_W_SCALE = (IDX_HEADS * IDX_DIM) ** -0.5
ATT_SCALE = HEAD_DIM ** -0.5

C_CONV, C_GZ, C_Q, C_K, C_V, C_GA, C_GB, C_QI, C_SM, C_SM2, C_END = (
    0, 3072, 4096, 5120, 6144, 7168, 8192, 9216, 9728, 9856, 9984)
L_KI, L_B, L_A, L_WI = 0, 64, 72, 80

GROUP = 8
FIRST_TOK = 4
NEG_INF = float("-inf")
KEY_NEG_INF = -2139095041
INT_MIN = -2147483648
VMEM_LIMIT = 56 * 1024 * 1024


def _cp(sem, vmem=VMEM_LIMIT):
    return pltpu.CompilerParams(dimension_semantics=sem, vmem_limit_bytes=vmem)


def _const_spec(shape):
    nd = len(shape)
    return pl.BlockSpec(shape, lambda *a: (0,) * nd, pipeline_mode=pl.Buffered(1))


def _sigmoid(x):
    return 1.0 / (1.0 + jnp.exp(-x))


def _silu(x):
    return x * _sigmoid(x)


def _softplus(x):
    return jnp.maximum(x, 0.0) + jnp.log(1.0 + jnp.exp(-jnp.abs(x)))


def _nt(a, b, precision=None):
    return lax.dot_general(a, b, (((1,), (1,)), ((), ())), precision=precision, preferred_element_type=F32)


def _tn(a, b, precision=None):
    return lax.dot_general(a, b, (((0,), (0,)), ((), ())), precision=precision, preferred_element_type=F32)


def _mm(a, b, precision=None):
    return jnp.dot(a, b, precision=precision, preferred_element_type=F32)


def _ada_kernel(c_ref, w_ref, b_ref, o_ref):
    c = c_ref[...]
    o_ref[...] = _mm(_silu(c), w_ref[...], HI) + b_ref[...]


def ada_mod(c_all, w_ada, b_ada):
    r, d = c_all.shape
    n = w_ada.shape[1]
    tn = 1024
    return pl.pallas_call(
        _ada_kernel,
        grid=(n // tn,),
        in_specs=[pl.BlockSpec((r, d), lambda j: (0, 0)),
                  pl.BlockSpec((d, tn), lambda j: (0, j)),
                  pl.BlockSpec((1, tn), lambda j: (0, j))],
        out_specs=pl.BlockSpec((r, tn), lambda j: (0, j)),
        out_shape=jax.ShapeDtypeStruct((r, n), F32),
        compiler_params=_cp(("parallel",)),
        name="ada_mod",
    )(c_all, w_ada, b_ada.reshape(1, n))


def _rel_bucket(d):
    n = jnp.maximum(d, 0)
    max_exact = REL_BUCKETS // 2
    nf = jnp.maximum(n, max_exact).astype(F32)
    large = max_exact + (jnp.log(nf / max_exact) / math.log(REL_MAX_DIST / max_exact)
                         * (REL_BUCKETS - max_exact)).astype(I32)
    large = jnp.minimum(large, REL_BUCKETS - 1)
    return jnp.where(n < max_exact, n, large)


def _bias_kernel(rb_ref, e_ref, bl_ref, bt_ref):
    def table(d, h):
        bk = _rel_bucket(d)
        acc = jnp.zeros(d.shape, F32)
        for b in range(REL_BUCKETS):
            acc = acc + jnp.where(bk == b, rb_ref[b, h], 0.0)
        return acc - rb_ref[REL_BUCKETS - 1, h]

    a = lax.broadcasted_iota(I32, (128, 128), 0)
    b = lax.broadcasted_iota(I32, (128, 128), 1)
    r = lax.broadcasted_iota(I32, (GROUP, 128), 0)
    c = lax.broadcasted_iota(I32, (GROUP, 128), 1)
    for h in range(N_HEADS):
        e_ref[0, h] = table(a - b, h)
        e_ref[1, h] = table(a - b + 128, h)
        bl_ref[h * GROUP:(h + 1) * GROUP, :] = table(128 + r - FIRST_TOK - c, h)
        bt_ref[h * GROUP:(h + 1) * GROUP, :] = table(r - c, h)


def bias_tables(rel_bias):
    return pl.pallas_call(
        _bias_kernel,
        in_specs=[pl.BlockSpec(memory_space=pltpu.SMEM)],
        out_shape=(jax.ShapeDtypeStruct((2, N_HEADS, 128, 128), F32),
                   jax.ShapeDtypeStruct((N_HEADS * GROUP, 128), F32),
                   jax.ShapeDtypeStruct((N_HEADS * GROUP, 128), F32)),
        name="bias_tables",
    )(rel_bias)


def _in_kernel(x_ref, sh_ref, sc_ref, nw_ref, w_ref, lnw_ref, lnb_ref,
               conv_ref, gz_ref, q_ref, kf_ref, vf_ref, k16_ref, v16_ref, ga_ref, gb_ref, qi_ref,
               small_ref, ki_ref, kb0_ref, kb1_ref):
    x = x_ref[...]
    h = x * lax.rsqrt(jnp.mean(x * x, -1, keepdims=True) + EPS) * nw_ref[...]
    h = h * (1.0 + sc_ref[...]) + sh_ref[...]
    hb = h.astype(BF16)

    def proj(a, b):
        return _mm(hb, w_ref[:, a:b])

    for c in range(0, 3072, 1024):
        conv_ref[:, c:c + 1024] = proj(C_CONV + c, C_CONV + c + 1024)
    gz_ref[...] = proj(C_GZ, C_Q)
    q_ref[...] = (proj(C_Q, C_K) * ATT_SCALE).astype(BF16)
    k = proj(C_K, C_V)
    kf_ref[...] = k
    k16_ref[...] = k.astype(BF16)
    v = proj(C_V, C_GA)
    vf_ref[...] = v
    v16_ref[...] = v.astype(BF16)
    ga_ref[...] = _sigmoid(proj(C_GA, C_GB))
    gb_ref[...] = _sigmoid(proj(C_GB, C_QI))
    qi_ref[...] = proj(C_QI, C_SM).astype(BF16)
    z = proj(C_SM, C_SM2)
    small_ref[...] = z
    lane = lax.broadcasted_iota(I32, z.shape, 1)
    for zz, lo, out_ref in ((z, True, kb0_ref), (proj(C_SM2, C_END), False, kb1_ref)):
        m = (lane < IDX_DIM) if lo else (lane >= IDX_DIM)
        zk = jnp.where(m, zz, 0.0)
        mean = jnp.sum(zk, -1, keepdims=True) * (1.0 / IDX_DIM)
        xc = jnp.where(m, zz - mean, 0.0)
        var = jnp.sum(xc * xc, -1, keepdims=True) * (1.0 / IDX_DIM)
        row = 0 if lo else 1
        kn = xc * lax.rsqrt(var + EPS) * lnw_ref[row:row + 1, :] + lnb_ref[row:row + 1, :]
        out_ref[...] = kn.astype(BF16)
        if lo:
            ki_ref[...] = kn[:, :IDX_DIM]


def in_proj(x, sh, sc, norm_w, w_cat, lnw2, lnb2, tm=256):
    t = x.shape[0]
    tm = min(tm, t)
    per_row = sh.shape[0] != 1
    mod_spec = (pl.BlockSpec((tm, D_MODEL), lambda i: (i, 0)) if per_row
                else pl.BlockSpec((1, D_MODEL), lambda i: (0, 0)))

    def rows(n, dt):
        return pl.BlockSpec((tm, n), lambda i: (i, 0)), jax.ShapeDtypeStruct((t, n), dt)

    outs = [rows(3072, F32), rows(1024, F32), rows(1024, BF16), rows(1024, F32), rows(1024, F32),
            rows(1024, BF16), rows(1024, BF16), rows(1024, F32), rows(1024, F32), rows(512, BF16),
            rows(128, F32), rows(IDX_DIM, F32), rows(128, BF16), rows(128, BF16)]
    return pl.pallas_call(
        _in_kernel,
        grid=(t // tm,),
        in_specs=[pl.BlockSpec((tm, D_MODEL), lambda i: (i, 0)), mod_spec, mod_spec,
                  _const_spec((1, D_MODEL)), _const_spec((D_MODEL, C_END)),
                  _const_spec((2, 128)), _const_spec((2, 128))],
        out_specs=[o[0] for o in outs],
        out_shape=[o[1] for o in outs],
        compiler_params=_cp(("parallel",)),
        name="in_proj",
    )(x, sh, sc, norm_w, w_cat, lnw2, lnb2)


def _gdn_kernel(*refs, R, C, N, has_hist):
    if has_hist:
        conv_ref, hist_ref, small_ref, gz_ref, s0_ref, cw_ref, misc_ref, nw_ref, o_ref, sout_ref, xbuf, s_scr = refs
    else:
        conv_ref, small_ref, gz_ref, s0_ref, cw_ref, misc_ref, nw_ref, o_ref, sout_ref, xbuf, s_scr = refs
    n = pl.program_id(1)

    @pl.when(n == 0)
    def _():
        xbuf[...] = jnp.zeros_like(xbuf)
        s_scr[...] = s0_ref[0]

    x = conv_ref[...]
    rows_c = lax.broadcasted_iota(I32, (C, 1), 0)
    if has_hist:
        rows_r = lax.broadcasted_iota(I32, (R, 1), 0)
        x = jnp.where((rows_r % GROUP) < FIRST_TOK, hist_ref[...], x)
        valid = (rows_c < R) & ((rows_c % GROUP) >= FIRST_TOK)
    else:
        valid = rows_c >= 0
    validf = jnp.where(valid, 1.0, 0.0)
    xbuf[8:8 + R, :] = x
    w = cw_ref[...]
    y = (w[3:4] * xbuf[8:8 + C, :] + w[2:3] * xbuf[7:7 + C, :]
         + w[1:2] * xbuf[6:6 + C, :] + w[0:1] * xbuf[5:5 + C, :])
    if N > 1:
        xbuf[0:8, :] = xbuf[C:C + 8, :]
    y = _silu(y)

    small = small_ref[...]
    if R < C:
        small = jnp.concatenate([small, jnp.zeros((C - R, 128), F32)], axis=0)
    misc = misc_ref[...]
    gfull = -jnp.exp(misc[0:1]) * _softplus(small + misc[1:2]) * validf
    betaf = _sigmoid(small) * validf
    ri = lax.broadcasted_iota(I32, (C, C), 0)
    ci = lax.broadcasted_iota(I32, (C, C), 1)
    tril = ri >= ci
    strict = ri > ci
    eye = ri == ci
    eye_f = jnp.where(eye, 1.0, 0.0)
    ones_cc = jnp.ones((C, C), F32)
    G = _mm(jnp.where(tril, 1.0, 0.0), gfull, HI)

    for h in range(N_HEADS):
        q = y[:, h * 128:(h + 1) * 128]
        k = y[:, 1024 + h * 128:1024 + (h + 1) * 128]
        v = y[:, 2048 + h * 128:2048 + (h + 1) * 128]
        q = q * lax.rsqrt(jnp.sum(q * q, -1, keepdims=True) + EPS) * (HEAD_DIM ** -0.5)
        k = k * lax.rsqrt(jnp.sum(k * k, -1, keepdims=True) + EPS) * validf
        v = v * validf
        bc = betaf[:, L_B + h:L_B + h + 1]
        gc = G[:, L_A + h:L_A + h + 1]
        gb = jnp.broadcast_to(gc, (C, C))
        grow = _mm(ones_cc, jnp.where(eye, gb, 0.0), HI)
        dec = jnp.where(tril, jnp.exp(jnp.minimum(gb - grow, 0.0)), 0.0)
        kk = _nt(k, k, HI)
        a = jnp.where(strict, bc * kk * dec, 0.0)
        nm = -a
        xinv = eye_f + nm
        pw = _mm(nm, nm, HI)
        span = 2
        while span < C:
            xinv = xinv + _mm(xinv, pw, HI)
            span *= 2
            if span < C:
                pw = _mm(pw, pw, HI)
        eg = jnp.exp(gc)
        u_mat = _mm(xinv, v * bc, HI)
        w_mat = _mm(xinv, k * (bc * eg), HI)
        qk = _nt(q, k, HI) * dec
        glast = gc[C - 1:C, :]
        qd = q * eg
        kd = k * jnp.exp(glast - gc)
        ge = jnp.exp(glast)
        s = s_scr[h]
        u = u_mat - _mm(w_mat, s, HI)
        o = _mm(qd, s, HI) + _mm(qk, u, HI)
        s_scr[h] = s * ge + _tn(kd, u, HI)
        o = o[:R]
        on = o * lax.rsqrt(jnp.mean(o * o, -1, keepdims=True) + EPS) * nw_ref[...]
        gz = gz_ref[:, h * 128:(h + 1) * 128]
        o_ref[:, h * 128:(h + 1) * 128] = (on * _silu(gz)).astype(o_ref.dtype)

    @pl.when(n == N - 1)
    def _():
        sout_ref[0] = s_scr[...]


def gdn(conv_in, hist, small, gz, s0, conv_w, misc, norm_w, *, nseq, nchunk, rows, out_dtype):
    C = GDN_CHUNK
    t = conv_in.shape[0]
    has_hist = hist is not None

    def rspec(n):
        return pl.BlockSpec((rows, n), lambda b, c: (b * nchunk + c, 0))

    in_specs = [rspec(3072)] + ([rspec(3072)] if has_hist else []) + [
        rspec(128), rspec(1024),
        pl.BlockSpec((1, N_HEADS, 128, 128), lambda b, c: (b, 0, 0, 0)),
        pl.BlockSpec((GDN_CONV, 3072), lambda b, c: (0, 0)),
        pl.BlockSpec((8, 128), lambda b, c: (0, 0)),
        pl.BlockSpec((1, 128), lambda b, c: (0, 0))]
    args = [conv_in] + ([hist] if has_hist else []) + [small, gz, s0, conv_w, misc, norm_w]
    return pl.pallas_call(
        functools.partial(_gdn_kernel, R=rows, C=C, N=nchunk, has_hist=has_hist),
        grid=(nseq, nchunk),
        in_specs=in_specs,
        out_specs=[rspec(1024), pl.BlockSpec((1, N_HEADS, 128, 128), lambda b, c: (b, 0, 0, 0))],
        out_shape=[jax.ShapeDtypeStruct((t, 1024), out_dtype),
                   jax.ShapeDtypeStruct((nseq, N_HEADS, 128, 128), F32)],
        scratch_shapes=[pltpu.VMEM((8 + C, 3072), F32), pltpu.VMEM((N_HEADS, 128, 128), F32)],
        compiler_params=_cp(("arbitrary", "arbitrary")),
        name="gdn",
    )(*args)


def _sort_key(s):
    bits = pltpu.bitcast(s + 0.0, I32)
    return bits ^ ((bits >> 31) & 0x7FFFFFFF)


def _count_ge(keys_ref, nkt, cand, tq, tk):
    def body(j, acc):
        kt = keys_ref[:, pl.ds(pl.multiple_of(j * tk, tk), tk)]
        c = jnp.where(kt >= cand, 1, 0)
        for g in range(tk // 128):
            acc = acc + c[:, g * 128:(g + 1) * 128]
        return acc

    acc = lax.fori_loop(0, nkt, body, jnp.zeros((tq, 128), I32))
    return jnp.sum(acc, axis=-1, keepdims=True)


def _kth_largest(keys_ref, nkt, tq, tk, k):
    c0 = _count_ge(keys_ref, nkt, jnp.zeros((tq, 1), I32), tq, tk)
    pos = c0 >= k
    thr = jnp.where(pos, 0, INT_MIN)
    cnt = jnp.where(pos, c0, nkt * tk)

    def cond(st):
        bit, _, _, unresolved = st
        return (bit >= 0) & (unresolved > 0)

    def body(st):
        bit, thr, cnt, _ = st
        cand = thr + lax.shift_left(jnp.int32(1), bit)
        c = _count_ge(keys_ref, nkt, cand, tq, tk)
        ok = c >= k
        thr = jnp.where(ok, cand, thr)
        cnt = jnp.where(ok, c, cnt)
        return bit - 1, thr, cnt, jnp.max(jnp.where(cnt != k, 1, 0))

    _, thr, cnt, _ = lax.while_loop(cond, body, (jnp.int32(30), thr, cnt, jnp.max(jnp.where(cnt != k, 1, 0))))
    return thr, cnt


def _tie_budget(keys_ref, nkt, thr, cnt, tq, tk, k):
    tied = (cnt > k) & (thr > KEY_NEG_INF)
    any_tied = jnp.max(jnp.where(tied, 1, 0))
    return tied, any_tied


def _emit_selection(keys_ref, nkt, thr, cnt, tq, tk, k, emit):
    tied = (cnt > k) & (thr > KEY_NEG_INF)
    any_tied = jnp.max(jnp.where(tied, 1, 0))

    @pl.when(any_tied == 0)
    def _():
        def body(j, carry):
            kt = keys_ref[:, pl.ds(pl.multiple_of(j * tk, tk), tk)]
            emit(j, kt >= thr)
            return carry
        lax.fori_loop(0, nkt, body, 0)

    @pl.when(any_tied != 0)
    def _():
        c_gt = _count_ge(keys_ref, nkt, thr + 1, tq, tk)
        budget = jnp.where(tied, k - c_gt, nkt * tk).astype(F32)
        ri = lax.broadcasted_iota(I32, (tk, tk), 0)
        ci = lax.broadcasted_iota(I32, (tk, tk), 1)
        upper = jnp.where(ri <= ci, 1.0, 0.0).astype(BF16)

        def body(j, seen):
            kt = keys_ref[:, pl.ds(pl.multiple_of(j * tk, tk), tk)]
            eq = jnp.where(kt == thr, 1.0, 0.0)
            rank = _mm(eq.astype(BF16), upper) + seen
            sel = jnp.where(kt > thr, 1.0, jnp.where(rank <= budget, eq, 0.0))
            emit(j, sel > 0.5)
            return seen + jnp.sum(eq, axis=-1, keepdims=True)
        lax.fori_loop(0, nkt, body, jnp.zeros((tq, 1), F32))


def _idx_kernel(qi_ref, small_ref, kb0_ref, kb1_ref, mask_ref, keys_ref, *, TQ, TK, T, k):
    i = pl.program_id(0)
    nkt = (i * TQ + TQ + TK - 1) // TK
    w = small_ref[:, L_WI:L_WI + IDX_HEADS] * IDX_W_SCALE
    pos_q = i * TQ + lax.broadcasted_iota(I32, (TQ, 1), 0)

    def score_tile(j, carry):
        off = pl.multiple_of(j * TK, TK)
        ka = kb0_ref[pl.ds(off, TK), :]
        kb = kb1_ref[pl.ds(off, TK), :]
        acc = jnp.zeros((TQ, TK), F32)
        for p in range(IDX_HEADS // 2):
            q2 = qi_ref[:, p * 128:(p + 1) * 128]
            acc = acc + w[:, 2 * p:2 * p + 1] * jnp.maximum(_nt(q2, ka), 0.0)
            acc = acc + w[:, 2 * p + 1:2 * p + 2] * jnp.maximum(_nt(q2, kb), 0.0)
        pos_k = off + lax.broadcasted_iota(I32, (1, TK), 1)
        acc = jnp.where(pos_k <= pos_q, acc, NEG_INF)
        keys_ref[:, pl.ds(off, TK)] = _sort_key(acc)
        return carry

    lax.fori_loop(0, nkt, score_tile, 0)
    thr, cnt = _kth_largest(keys_ref, nkt, TQ, TK, k)

    def emit(j, sel):
        off = pl.multiple_of(j * TK, TK)
        pos_k = off + lax.broadcasted_iota(I32, (1, TK), 1)
        m = jnp.where(pos_k <= pos_q, jnp.where(sel, 1, 0), 0)
        mask_ref[:, pl.ds(off, TK)] = m.astype(jnp.int8)

    _emit_selection(keys_ref, nkt, thr, cnt, TQ, TK, k, emit)

    def zero_tile(j, carry):
        mask_ref[:, pl.ds(pl.multiple_of(j * TK, TK), TK)] = jnp.zeros((TQ, TK), jnp.int8)
        return carry

    lax.fori_loop(nkt, T // TK, zero_tile, 0)


def prompt_topk_mask(qi16, small, kb0, kb1, k, TQ=128, TK=512):
    t = qi16.shape[0]
    TK = min(TK, t)
    return pl.pallas_call(
        functools.partial(_idx_kernel, TQ=TQ, TK=TK, T=t, k=k),
        grid=(t // TQ,),
        in_specs=[pl.BlockSpec((TQ, 512), lambda i: (i, 0)),
                  pl.BlockSpec((TQ, 128), lambda i: (i, 0)),
                  _const_spec((t, 128)), _const_spec((t, 128))],
        out_specs=pl.BlockSpec((TQ, t), lambda i: (i, 0)),
        out_shape=jax.ShapeDtypeStruct((t, t), jnp.int8),
        scratch_shapes=[pltpu.VMEM((TQ, t), I32)],
        compiler_params=_cp(("parallel",)),
        name="prompt_topk_mask",
    )(qi16, small, kb0, kb1)


def _att_kernel(qa_ref, ka_ref, q_ref, k_ref, v_ref, mask_ref, e_ref, o_ref, m_scr, l_scr, acc_scr, s_scr, *, TA):
    p = pl.program_id(0)
    i = qa_ref[p]
    j = ka_ref[p]

    @pl.when(j == 0)
    def _():
        m_scr[...] = jnp.full(m_scr.shape, -1e30, F32)
        l_scr[...] = jnp.zeros(l_scr.shape, F32)
        acc_scr[...] = jnp.zeros(acc_scr.shape, F32)

    def update(near):
        msk = mask_ref[...].astype(I32) != 0
        nb = TA // 128
        if near:
            fd = jnp.where(j == i, 1.0, 0.0)
        for h in range(N_HEADS):
            hs = slice(h * 128, (h + 1) * 128)
            s_scr[...] = _nt(q_ref[:, hs], k_ref[:, hs])
            if near:
                for b in range(nb):
                    bs = slice(b * 128, (b + 1) * 128)
                    s_scr[bs, bs] += fd * e_ref[0, h]
                    if b + 1 < nb:
                        s_scr[(b + 1) * 128:(b + 2) * 128, bs] += fd * e_ref[1, h]
                s_scr[0:128, TA - 128:TA] += (1.0 - fd) * e_ref[1, h]
            sm = jnp.where(msk, s_scr[...], NEG_INF)
            m_old = m_scr[h]
            m_new = jnp.maximum(m_old, jnp.max(sm, axis=-1, keepdims=True))
            alpha = jnp.exp(m_old - m_new)
            pr = jnp.exp(sm - m_new)
            l_scr[h] = alpha * l_scr[h] + jnp.sum(pr, axis=-1, keepdims=True)
            acc_scr[:, hs] = alpha * acc_scr[:, hs] + _mm(pr.astype(BF16), v_ref[:, hs])
            m_scr[h] = m_new

    @pl.when(j >= i - 1)
    def _():
        update(True)

    @pl.when(j < i - 1)
    def _():
        update(False)

    @pl.when(j == i)
    def _():
        for h in range(N_HEADS):
            hs = slice(h * 128, (h + 1) * 128)
            o_ref[:, hs] = (acc_scr[:, hs] / l_scr[h]).astype(o_ref.dtype)


def prompt_attention(q16, k16, v16, mask, e_tab, TA=512):
    t = q16.shape[0]
    TA = min(TA, t)
    nb = t // TA
    qa = np.concatenate([np.full(i + 1, i, np.int32) for i in range(nb)])
    ka = np.concatenate([np.arange(i + 1, dtype=np.int32) for i in range(nb)])
    grid_spec = pltpu.PrefetchScalarGridSpec(
        num_scalar_prefetch=2,
        grid=(len(qa),),
        in_specs=[pl.BlockSpec((TA, 1024), lambda p, qa, ka: (qa[p], 0)),
                  pl.BlockSpec((TA, 1024), lambda p, qa, ka: (ka[p], 0)),
                  pl.BlockSpec((TA, 1024), lambda p, qa, ka: (ka[p], 0)),
                  pl.BlockSpec((TA, TA), lambda p, qa, ka: (qa[p], ka[p])),
                  pl.BlockSpec((2, N_HEADS, 128, 128), lambda p, qa, ka: (0, 0, 0, 0))],
        out_specs=pl.BlockSpec((TA, 1024), lambda p, qa, ka: (qa[p], 0)),
        scratch_shapes=[pltpu.VMEM((N_HEADS, TA, 1), F32), pltpu.VMEM((N_HEADS, TA, 1), F32),
                        pltpu.VMEM((TA, 1024), F32), pltpu.VMEM((TA, TA), F32)])
    return pl.pallas_call(
        functools.partial(_att_kernel, TA=TA),
        grid_spec=grid_spec,
        out_shape=jax.ShapeDtypeStruct((t, 1024), BF16),
        compiler_params=_cp(("arbitrary",)),
        name="prompt_attention",
    )(jnp.asarray(qa), jnp.asarray(ka), q16, k16, v16, mask, e_tab)


def _page_copies(pt_ref, cache_ref, buf_ref, sem_ref, b, first_page, npages, slot):
    out = []
    for g in range(npages):
        pid = pt_ref[b, first_page + g]
        out.append(pltpu.make_async_copy(cache_ref.at[pid], buf_ref.at[slot, pl.ds(g * PAGE, PAGE)], sem_ref.at[slot]))
    return out


def _sidx_kernel(pt_ref, qm_ref, w_ref, knew_ref, cache_ref, sc_ref, kbuf, sem, *, NP, NB):
    b = pl.program_id(0)
    slot = b % 2

    @pl.when(b == 0)
    def _():
        for c in _page_copies(pt_ref, cache_ref, kbuf, sem, 0, 0, NP, 0):
            c.start()

    @pl.when(b + 1 < NB)
    def _():
        for c in _page_copies(pt_ref, cache_ref, kbuf, sem, b + 1, 0, NP, 1 - slot):
            c.start()

    for c in _page_copies(pt_ref, cache_ref, kbuf, sem, b, 0, NP, slot):
        c.wait()

    qm = qm_ref[0]
    w = w_ref[0]

    def fold(s):
        s = jnp.maximum(s, 0.0) * w
        out = s[0:GROUP]
        for h in range(1, IDX_HEADS):
            out = out + s[h * GROUP:(h + 1) * GROUP]
        return out

    past = NP * PAGE
    CH = 2048
    for c0 in range(0, past, CH):
        kc = kbuf[slot, c0:c0 + CH, :].astype(BF16)
        sc_ref[0, :, c0:c0 + CH] = fold(_nt(qm, kc))
    knew = jnp.concatenate([knew_ref[...], jnp.zeros((128 - GROUP, IDX_DIM), F32)], axis=0).astype(BF16)
    tail = fold(_nt(qm, knew))
    r = lax.broadcasted_iota(I32, (GROUP, 128), 0)
    c = lax.broadcasted_iota(I32, (GROUP, 128), 1)
    sc_ref[0, :, past:past + 128] = jnp.where((c >= FIRST_TOK) & (c <= r), tail, NEG_INF)


def sample_scores(page_table, qm, wcol, ki_s, cache_idx):
    nb, npg = page_table.shape
    past = npg * PAGE
    grid_spec = pltpu.PrefetchScalarGridSpec(
        num_scalar_prefetch=1,
        grid=(nb,),
        in_specs=[pl.BlockSpec((1, IDX_HEADS * GROUP, IDX_DIM), lambda b, pt: (b, 0, 0)),
                  pl.BlockSpec((1, IDX_HEADS * GROUP, 1), lambda b, pt: (b, 0, 0)),
                  pl.BlockSpec((GROUP, IDX_DIM), lambda b, pt: (b, 0)),
                  pl.BlockSpec(memory_space=pl.ANY)],
        out_specs=pl.BlockSpec((1, GROUP, past + 128), lambda b, pt: (b, 0, 0)),
        scratch_shapes=[pltpu.VMEM((2, past, IDX_DIM), F32), pltpu.SemaphoreType.DMA((2,))])
    return pl.pallas_call(
        functools.partial(_sidx_kernel, NP=npg, NB=nb),
        grid_spec=grid_spec,
        out_shape=jax.ShapeDtypeStruct((nb, GROUP, past + 128), F32),
        compiler_params=_cp(("arbitrary",)),
        name="sample_scores",
    )(page_table, qm, wcol, ki_s, cache_idx)


def _ssel_kernel(sc_ref, am_ref, keys_ref, *, TQ, TK, W, k):
    nkt = W // TK

    def to_keys(j, carry):
        off = pl.multiple_of(j * TK, TK)
        keys_ref[:, pl.ds(off, TK)] = _sort_key(sc_ref[:, pl.ds(off, TK)])
        return carry

    lax.fori_loop(0, nkt, to_keys, 0)
    thr, cnt = _kth_largest(keys_ref, nkt, TQ, TK, k)

    def emit(j, sel):
        off = pl.multiple_of(j * TK, TK)
        valid = sc_ref[:, pl.ds(off, TK)] > NEG_INF
        am_ref[:, pl.ds(off, TK)] = jnp.where(valid, jnp.where(sel, 0.0, NEG_INF), NEG_INF)

    _emit_selection(keys_ref, nkt, thr, cnt, TQ, TK, k, emit)


def sample_topk_mask(scores, k, TQ=32, TK=128):
    r, w = scores.shape
    TQ = min(TQ, r)
    return pl.pallas_call(
        functools.partial(_ssel_kernel, TQ=TQ, TK=TK, W=w, k=k),
        grid=(r // TQ,),
        in_specs=[pl.BlockSpec((TQ, w), lambda i: (i, 0))],
        out_specs=pl.BlockSpec((TQ, w), lambda i: (i, 0)),
        out_shape=jax.ShapeDtypeStruct((r, w), F32),
        scratch_shapes=[pltpu.VMEM((TQ, w), I32)],
        compiler_params=_cp(("parallel",)),
        name="sample_topk_mask",
    )(scores)


def _satt_kernel(pt_ref, qx_ref, am_ref, knew_ref, vnew_ref, bl_ref, bt_ref, ck_ref, cv_ref, o_ref,
                 kbuf, vbuf, ksem, vsem, m_scr, l_scr, acc_scr, *, PC, NCH, NB):
    b = pl.program_id(0)
    c = pl.program_id(1)
    g = b * NCH + c
    slot = g % 2
    CW = PC * PAGE

    def copies(bq, cq, sl):
        return (_page_copies(pt_ref, ck_ref, kbuf, ksem, bq, cq * PC, PC, sl)
                + _page_copies(pt_ref, cv_ref, vbuf, vsem, bq, cq * PC, PC, sl))

    @pl.when(g == 0)
    def _():
        for cp in copies(0, 0, 0):
            cp.start()

    @pl.when(g + 1 < NB * NCH)
    def _():
        nxt = g + 1
        for cp in copies(nxt // NCH, nxt % NCH, 1 - slot):
            cp.start()

    for cp in copies(b, c, slot):
        cp.wait()

    @pl.when(c == 0)
    def _():
        m_scr[...] = jnp.full(m_scr.shape, -1e30, F32)
        l_scr[...] = jnp.zeros(l_scr.shape, F32)
        acc_scr[...] = jnp.zeros(acc_scr.shape, F32)

    qx = qx_ref[0]

    def online(s, v16):
        m_old = m_scr[...]
        m_new = jnp.maximum(m_old, jnp.max(s, axis=-1, keepdims=True))
        alpha = jnp.exp(m_old - m_new)
        pr = jnp.exp(s - m_new)
        l_scr[...] = alpha * l_scr[...] + jnp.sum(pr, axis=-1, keepdims=True)
        acc_scr[...] = alpha * acc_scr[...] + _mm(pr.astype(BF16), v16)
        m_scr[...] = m_new

    def rows8(a):
        return jnp.concatenate([a] * N_HEADS, axis=0)

    last = jnp.where(c == NCH - 1, 1.0, 0.0)
    s = _nt(qx, kbuf[slot].astype(BF16))
    s = s + rows8(am_ref[0, :, pl.ds(pl.multiple_of(c * CW, CW), CW)])
    s = s + last * jnp.concatenate([jnp.zeros((N_HEADS * GROUP, CW - 128), F32), bl_ref[...]], axis=1)
    online(s, vbuf[slot].astype(BF16))

    @pl.when(c == NCH - 1)
    def _():
        zpad = jnp.zeros((128 - GROUP, 1024), F32)
        kn = jnp.concatenate([knew_ref[...], zpad], axis=0).astype(BF16)
        vn = jnp.concatenate([vnew_ref[...], zpad], axis=0).astype(BF16)
        s2 = _nt(qx, kn) + rows8(am_ref[0, :, NCH * CW:NCH * CW + 128]) + bt_ref[...]
        online(s2, vn)
        out = acc_scr[...] / l_scr[...]
        for h in range(N_HEADS):
            o_ref[:, h * 128:(h + 1) * 128] = out[h * GROUP:(h + 1) * GROUP, h * 128:(h + 1) * 128]


def sample_attention(page_table, qx, addmask, kf_s, vf_s, b_last, b_tail, cache_k, cache_v, PC=8):
    nb, npg = page_table.shape
    PC = min(PC, npg)
    nch = npg // PC
    w = addmask.shape[-1]
    grid_spec = pltpu.PrefetchScalarGridSpec(
        num_scalar_prefetch=1,
        grid=(nb, nch),
        in_specs=[pl.BlockSpec((1, N_HEADS * GROUP, 1024), lambda b, c, pt: (b, 0, 0)),
                  pl.BlockSpec((1, GROUP, w), lambda b, c, pt: (b, 0, 0)),
                  pl.BlockSpec((GROUP, 1024), lambda b, c, pt: (b, 0)),
                  pl.BlockSpec((GROUP, 1024), lambda b, c, pt: (b, 0)),
                  pl.BlockSpec((N_HEADS * GROUP, 128), lambda b, c, pt: (0, 0)),
                  pl.BlockSpec((N_HEADS * GROUP, 128), lambda b, c, pt: (0, 0)),
                  pl.BlockSpec(memory_space=pl.ANY),
                  pl.BlockSpec(memory_space=pl.ANY)],
        out_specs=pl.BlockSpec((GROUP, 1024), lambda b, c, pt: (b, 0)),
        scratch_shapes=[pltpu.VMEM((2, PC * PAGE, 1024), F32), pltpu.VMEM((2, PC * PAGE, 1024), F32),
                        pltpu.SemaphoreType.DMA((2,)), pltpu.SemaphoreType.DMA((2,)),
                        pltpu.VMEM((N_HEADS * GROUP, 1), F32), pltpu.VMEM((N_HEADS * GROUP, 1), F32),
                        pltpu.VMEM((N_HEADS * GROUP, 1024), F32)])
    return pl.pallas_call(
        functools.partial(_satt_kernel, PC=PC, NCH=nch, NB=nb),
        grid_spec=grid_spec,
        out_shape=jax.ShapeDtypeStruct((nb * GROUP, 1024), F32),
        compiler_params=_cp(("arbitrary", "arbitrary")),
        name="sample_attention",
    )(page_table, qx, addmask, kf_s, vf_s, b_last, b_tail, cache_k, cache_v)


def _mix_kernel(oa_ref, ob_ref, ga_ref, gb_ref, x_ref, g1_ref, wa_ref, wb_ref, wo_ref, o_ref):
    ya = _mm(oa_ref[...].astype(BF16), wa_ref[...])
    yb = _mm(ob_ref[...].astype(BF16), wb_ref[...])
    mixed = ga_ref[...] * ya + gb_ref[...] * yb
    o_ref[...] = x_ref[...] + g1_ref[...] * _mm(mixed.astype(BF16), wo_ref[...])


def mix(oa, ob, ga, gb, x, g1, wa, wb, wo, tm=512):
    t = x.shape[0]
    tm = min(tm, t)
    per_row = g1.shape[0] != 1
    mod_spec = (pl.BlockSpec((tm, D_MODEL), lambda i: (i, 0)) if per_row
                else pl.BlockSpec((1, D_MODEL), lambda i: (0, 0)))
    row = pl.BlockSpec((tm, D_MODEL), lambda i: (i, 0))
    wspec = _const_spec((D_MODEL, D_MODEL))
    return pl.pallas_call(
        _mix_kernel,
        grid=(t // tm,),
        in_specs=[row, row, row, row, row, mod_spec, wspec, wspec, wspec],
        out_specs=row,
        out_shape=jax.ShapeDtypeStruct((t, D_MODEL), F32),
        compiler_params=_cp(("parallel",)),
        name="mix",
    )(oa, ob, ga, gb, x, g1, wa, wb, wo)


def _ffn_kernel(*refs, TM, has_hist):
    if has_hist:
        (x_ref, sh_ref, sc_ref, g2_ref, hist_ref, n2_ref, wup_ref, cw_ref, cb_ref, wdn_ref, fw_ref,
         y_ref, u_ref, ubuf) = refs
    else:
        (x_ref, sh_ref, sc_ref, g2_ref, n2_ref, wup_ref, cw_ref, cb_ref, wdn_ref, fw_ref,
         y_ref, u_ref, ubuf) = refs
    i = pl.program_id(0)

    @pl.when(i == 0)
    def _():
        ubuf[0:8, :] = jnp.zeros((8, 2 * D_FF), F32)

    x = x_ref[...]
    h = x * lax.rsqrt(jnp.mean(x * x, -1, keepdims=True) + EPS) * n2_ref[...]
    h = h * (1.0 + sc_ref[...]) + sh_ref[...]
    hb = h.astype(BF16)
    if has_hist:
        is_hist = (lax.broadcasted_iota(I32, (TM, 1), 0) % GROUP) < FIRST_TOK
    CB = 256
    acc = jnp.zeros((TM, D_MODEL), F32)
    for c in range(0, D_FF, CB):
        halves = []
        for base in (c, D_FF + c):
            cs = slice(base, base + CB)
            u = _mm(hb, wup_ref[:, cs])
            if has_hist:
                u = jnp.where(is_hist, hist_ref[:, cs], u)
            ubuf[8:8 + TM, cs] = u
            halves.append(cw_ref[2:3, cs] * u + cw_ref[1:2, cs] * ubuf[7:7 + TM, cs]
                          + cw_ref[0:1, cs] * ubuf[6:6 + TM, cs] + cb_ref[:, cs])
        act = _silu(halves[0]) * halves[1]
        acc = acc + _mm(act.astype(BF16), wdn_ref[c:c + CB, :])
    if has_hist:
        u_ref[...] = ubuf[8:8 + TM, :]
    else:
        u_ref[...] = ubuf[TM:TM + 8, :]
    ubuf[0:8, :] = ubuf[TM:TM + 8, :]
    x2 = x + g2_ref[...] * acc
    y_ref[...] = x2 * lax.rsqrt(jnp.mean(x2 * x2, -1, keepdims=True) + EPS) * fw_ref[...]


def ffn(x1, sh, sc, g2, hist, norm2_w, w_up, conv_w, conv_b, w_down, final_w, tm=256):
    t = x1.shape[0]
    tm = min(tm, t)
    has_hist = hist is not None
    per_row = sh.shape[0] != 1
    mod_spec = (pl.BlockSpec((tm, D_MODEL), lambda i: (i, 0)) if per_row
                else pl.BlockSpec((1, D_MODEL), lambda i: (0, 0)))
    row = pl.BlockSpec((tm, D_MODEL), lambda i: (i, 0))
    in_specs = [row, mod_spec, mod_spec, mod_spec]
    args = [x1, sh, sc, g2]
    if has_hist:
        in_specs.append(pl.BlockSpec((tm, 2 * D_FF), lambda i: (i, 0)))
        args.append(hist)
    in_specs += [_const_spec((1, D_MODEL)), _const_spec((D_MODEL, 2 * D_FF)), _const_spec((FFN_CONV, 2 * D_FF)),
                 _const_spec((1, 2 * D_FF)), _const_spec((D_FF, D_MODEL)), _const_spec((1, D_MODEL))]
    args += [norm2_w, w_up, conv_w, conv_b, w_down, final_w]
    if has_hist:
        u_spec = pl.BlockSpec((tm, 2 * D_FF), lambda i: (i, 0))
        u_shape = jax.ShapeDtypeStruct((t, 2 * D_FF), F32)
    else:
        u_spec = pl.BlockSpec((8, 2 * D_FF), lambda i: (0, 0))
        u_shape = jax.ShapeDtypeStruct((8, 2 * D_FF), F32)
    return pl.pallas_call(
        functools.partial(_ffn_kernel, TM=tm, has_hist=has_hist),
        grid=(t // tm,),
        in_specs=in_specs,
        out_specs=[row, u_spec],
        out_shape=[jax.ShapeDtypeStruct((t, D_MODEL), F32), u_shape],
        scratch_shapes=[pltpu.VMEM((8 + tm, 2 * D_FF), F32)],
        compiler_params=_cp(("arbitrary",)),
        name="ffn",
    )(*args)


def _group_rows(a, first):
    b, n, c = a.shape
    return jnp.pad(a, ((0, 0), (first, GROUP - first - n), (0, 0))).reshape(b * GROUP, c)


def kernel(x_prompt, x_sample, cache_k, cache_v, cache_idx_k, state_gdn, state_gdn_conv, state_ffn_conv,
           page_table, c_prompt, c_sample, w_ada, b_ada, norm1_w, w_in, gdn_conv_w, gdn_A_log, gdn_dt_bias,
           gdn_norm_w, idx_knorm_w, idx_knorm_b, w_branch_a, w_branch_b, w_out, norm2_w, w_up, ffn_conv_w,
           ffn_conv_b, w_down, rel_bias, final_norm_w):
    bp, tp, _ = x_prompt.shape
    bs, ts, _ = x_sample.shape
    assert bp == 1 and ts == GROUP - FIRST_TOK and w_ada.shape[0] == 1
    npg = page_table.shape[1]
    past = npg * PAGE

    w = w_in[0]
    w_cat = jnp.concatenate(
        [w[:, 0:4096], w[:, 4112:7184], w[:, 7768:9816], w[:, 7184:7696],
         w[:, 7696:7760], w[:, 4096:4112], w[:, 7760:7768], jnp.zeros((D_MODEL, 40), F32),
         jnp.zeros((D_MODEL, 64), F32), w[:, 7696:7760]], axis=1).astype(BF16)
    z64 = jnp.zeros((IDX_DIM,), F32)
    lnw2 = jnp.stack([jnp.concatenate([idx_knorm_w[0], z64]), jnp.concatenate([z64, idx_knorm_w[0]])])
    lnb2 = jnp.stack([jnp.concatenate([idx_knorm_b[0], z64]), jnp.concatenate([z64, idx_knorm_b[0]])])
    misc = jnp.zeros((8, 128), F32)
    misc = misc.at[0, L_A:L_A + N_HEADS].set(gdn_A_log[0]).at[1, L_A:L_A + N_HEADS].set(gdn_dt_bias[0])
    wa16, wb16, wo16 = w_branch_a[0].astype(BF16), w_branch_b[0].astype(BF16), w_out[0].astype(BF16)
    wup16, wdn16 = w_up[0].astype(BF16), w_down[0].astype(BF16)
    n1 = norm1_w[0].reshape(1, D_MODEL)
    n2 = norm2_w[0].reshape(1, D_MODEL)
    fw = final_norm_w.reshape(1, D_MODEL)
    gnw = gdn_norm_w[0].reshape(1, 128)
    ffn_b = ffn_conv_b[0].reshape(1, 2 * D_FF)

    c_all = jnp.concatenate([c_prompt, c_sample], axis=0)
    pad_r = (-c_all.shape[0]) % 8
    mod = ada_mod(jnp.pad(c_all, ((0, pad_r), (0, 0))), w_ada[0], b_ada[0])
    mod_p = mod[0:1]
    mod_s = jnp.repeat(mod[1:1 + bs], GROUP, axis=0)

    def mods(m):
        return [m[:, i * D_MODEL:(i + 1) * D_MODEL] for i in range(6)]

    e_tab, b_last, b_tail = bias_tables(rel_bias)

    xp = x_prompt[0]
    sh1, sc1, g1, sh2, sc2, g2 = mods(mod_p)
    (conv_p, gz_p, q_p, kf_p, vf_p, k16_p, v16_p, ga_p, gb_p, qi_p, small_p, ki_p, kb0_p, kb1_p) = in_proj(
        xp, sh1, sc1, n1, w_cat, lnw2, lnb2)
    oa_p, s_p = gdn(conv_p, None, small_p, gz_p, jnp.zeros((1, N_HEADS, 128, 128), F32), gdn_conv_w[0], misc, gnw,
                    nseq=1, nchunk=tp // GDN_CHUNK, rows=GDN_CHUNK, out_dtype=BF16)
    mask_p = prompt_topk_mask(qi_p, small_p, kb0_p, kb1_p, min(TOPK_MAX, tp // 4))
    ob_p = prompt_attention(q_p, k16_p, v16_p, mask_p, e_tab)
    x1_p = mix(oa_p, ob_p, ga_p, gb_p, xp, g1, wa16, wb16, wo16)
    y_p, utail_p = ffn(x1_p, sh2, sc2, g2, None, n2, wup16, ffn_conv_w[0], ffn_b, wdn16, fw)

    xs = _group_rows(x_sample, FIRST_TOK)
    sh1, sc1, g1, sh2, sc2, g2 = mods(mod_s)
    (conv_s, gz_s, q_s, kf_s, vf_s, _, _, ga_s, gb_s, qi_s, small_s, ki_s, _, _) = in_proj(
        xs, sh1, sc1, n1, w_cat, lnw2, lnb2)
    hist_gdn = _group_rows(state_gdn_conv[0], FIRST_TOK - (GDN_CONV - 1))
    oa_s, s_s = gdn(conv_s, hist_gdn, small_s, gz_s, state_gdn[0], gdn_conv_w[0], misc, gnw,
                    nseq=bs, nchunk=1, rows=GROUP, out_dtype=F32)
    qm = qi_s.reshape(bs, GROUP, IDX_HEADS, IDX_DIM).transpose(0, 2, 1, 3).reshape(bs, IDX_HEADS * GROUP, IDX_DIM)
    wcol = (small_s[:, L_WI:L_WI + IDX_HEADS] * IDX_W_SCALE).reshape(bs, GROUP, IDX_HEADS)
    wcol = wcol.transpose(0, 2, 1).reshape(bs, IDX_HEADS * GROUP, 1)
    scores = sample_scores(page_table, qm, wcol, ki_s, cache_idx_k[0])
    sc_tok = scores[:, FIRST_TOK:, :].reshape(bs * ts, past + 128)
    am = sample_topk_mask(sc_tok, min(TOPK_MAX, (past + ts) // 4))
    am = jnp.pad(am.reshape(bs, ts, past + 128), ((0, 0), (FIRST_TOK, 0), (0, 0)))
    q4 = q_s.reshape(bs, GROUP, N_HEADS, HEAD_DIM).transpose(0, 2, 1, 3)
    qx = (q4[:, :, :, None, :] * jnp.eye(N_HEADS, dtype=BF16)[None, :, None, :, None]).reshape(
        bs, N_HEADS * GROUP, N_HEADS * HEAD_DIM)
    n_phys = cache_k.shape[1]
    ob_s = sample_attention(page_table, qx, am, kf_s, vf_s, b_last, b_tail,
                            cache_k[0].reshape(n_phys, PAGE, N_HEADS * HEAD_DIM),
                            cache_v[0].reshape(n_phys, PAGE, N_HEADS * HEAD_DIM))
    x1_s = mix(oa_s, ob_s, ga_s, gb_s, xs, g1, wa16, wb16, wo16)
    hist_ffn = _group_rows(state_ffn_conv[0], FIRST_TOK - (FFN_CONV - 1))
    y_s, u_s = ffn(x1_s, sh2, sc2, g2, hist_ffn, n2, wup16, ffn_conv_w[0], ffn_b, wdn16, fw)

    def tok(a):
        return a.reshape(bs, GROUP, -1)[:, FIRST_TOK:, :]

    hd = (N_HEADS, HEAD_DIM)
    return (y_p[None], tok(y_s),
            kf_p.reshape((1, 1, tp) + hd), vf_p.reshape((1, 1, tp) + hd), ki_p.reshape(1, 1, tp, IDX_DIM),
            tok(kf_s).reshape((1, bs, ts) + hd), tok(vf_s).reshape((1, bs, ts) + hd), tok(ki_s)[None],
            s_p[None], s_s[None],
            conv_p[tp - (GDN_CONV - 1):][None, None],
            conv_s.reshape(bs, GROUP, -1)[:, GROUP - (GDN_CONV - 1):][None],
            utail_p[8 - (FFN_CONV - 1):][None, None],
            u_s.reshape(bs, GROUP, -1)[:, GROUP - (FFN_CONV - 1):][None])
```

```python
import functools
import math

import jax
import jax.numpy as jnp
import numpy as np
from jax import lax
from jax.experimental import pallas as pl
from jax.experimental.pallas import tpu as pltpu

F32 = jnp.float32
BF16 = jnp.bfloat16
I32 = jnp.int32
HI = lax.Precision.HIGHEST

D_MODEL = 1024
N_HEADS = 8
HEAD_DIM = 128
IDX_HEADS = 8
IDX_DIM = 64
TOPK_MAX = 256
D_FF = 2816
GDN_CONV = 4
FFN_CONV = 3
GDN_CHUNK = 64
PAGE = 128
REL_BUCKETS = 32
REL_MAX_DIST = 128
EPS = 1e-6
IDX_W_SCALE = (IDX_HEADS * IDX_DIM) ** -0.5
ATT_SCALE = HEAD_DIM ** -0.5

C_CONV, C_GZ, C_Q, C_K, C_V, C_GA, C_GB, C_QI, C_SM, C_SM2, C_END = (
    0, 3072, 4096, 5120, 6144, 7168, 8192, 9216, 9728, 9856, 9984)
L_KI, L_B, L_A, L_WI = 0, 64, 72, 80

GROUP = 8
FIRST_TOK = 4
NEG_INF = float("-inf")
KEY_NEG_INF = -2139095041
INT_MIN = -2147483648
VMEM_LIMIT = 56 * 1024 * 1024


def _cp(sem, vmem=VMEM_LIMIT):
    return pltpu.CompilerParams(dimension_semantics=sem, vmem_limit_bytes=vmem)


def _const_spec(shape):
    nd = len(shape)
    return pl.BlockSpec(shape, lambda *a: (0,) * nd, pipeline_mode=pl.Buffered(1))


def _sigmoid(x):
    return 1.0 / (1.0 + jnp.exp(-x))


def _silu(x):
    return x * _sigmoid(x)


def _softplus(x):
    return jnp.maximum(x, 0.0) + jnp.log(1.0 + jnp.exp(-jnp.abs(x)))


def _nt(a, b, precision=None):
    return lax.dot_general(a, b, (((1,), (1,)), ((), ())), precision=precision, preferred_element_type=F32)


def _tn(a, b, precision=None):
    return lax.dot_general(a, b, (((0,), (0,)), ((), ())), precision=precision, preferred_element_type=F32)


def _mm(a, b, precision=None):
    return jnp.dot(a, b, precision=precision, preferred_element_type=F32)


def _ada_kernel(c_ref, w_ref, b_ref, o_ref):
    c = c_ref[...]
    o_ref[...] = _mm(_silu(c), w_ref[...], HI) + b_ref[...]


def ada_mod(c_all, w_ada, b_ada):
    r, d = c_all.shape
    n = w_ada.shape[1]
    tn = 1024
    return pl.pallas_call(
        _ada_kernel,
        grid=(n // tn,),
        in_specs=[pl.BlockSpec((r, d), lambda j: (0, 0)),
                  pl.BlockSpec((d, tn), lambda j: (0, j)),
                  pl.BlockSpec((1, tn), lambda j: (0, j))],
        out_specs=pl.BlockSpec((r, tn), lambda j: (0, j)),
        out_shape=jax.ShapeDtypeStruct((r, n), F32),
        compiler_params=_cp(("parallel",)),
        name="ada_mod",
    )(c_all, w_ada, b_ada.reshape(1, n))


def _rel_bucket(d):
    n = jnp.maximum(d, 0)
    max_exact = REL_BUCKETS // 2
    nf = jnp.maximum(n, max_exact).astype(F32)
    large = max_exact + (jnp.log(nf / max_exact) / math.log(REL_MAX_DIST / max_exact)
                         * (REL_BUCKETS - max_exact)).astype(I32)
    large = jnp.minimum(large, REL_BUCKETS - 1)
    return jnp.where(n < max_exact, n, large)


def _bias_kernel(rb_ref, e_ref, bl_ref, bt_ref):
    def table(d, h):
        bk = _rel_bucket(d)
        acc = jnp.zeros(d.shape, F32)
        for b in range(REL_BUCKETS):
            acc = acc + jnp.where(bk == b, rb_ref[b, h], 0.0)
        return acc - rb_ref[REL_BUCKETS - 1, h]

    a = lax.broadcasted_iota(I32, (128, 128), 0)
    b = lax.broadcasted_iota(I32, (128, 128), 1)
    r = lax.broadcasted_iota(I32, (GROUP, 128), 0)
    c = lax.broadcasted_iota(I32, (GROUP, 128), 1)
    for h in range(N_HEADS):
        e_ref[0, h] = table(a - b, h)
        e_ref[1, h] = table(a - b + 128, h)
        bl_ref[h * GROUP:(h + 1) * GROUP, :] = table(128 + r - FIRST_TOK - c, h)
        bt_ref[h * GROUP:(h + 1) * GROUP, :] = table(r - c, h)


def bias_tables(rel_bias):
    return pl.pallas_call(
        _bias_kernel,
        in_specs=[pl.BlockSpec(memory_space=pltpu.SMEM)],
        out_shape=(jax.ShapeDtypeStruct((2, N_HEADS, 128, 128), F32),
                   jax.ShapeDtypeStruct((N_HEADS * GROUP, 128), F32),
                   jax.ShapeDtypeStruct((N_HEADS * GROUP, 128), F32)),
        name="bias_tables",
    )(rel_bias)


def _in_kernel(x_ref, sh_ref, sc_ref, nw_ref, w_ref, wt_ref, lnw_ref, lnb_ref,
               conv_ref, gz_ref, q_ref, kf_ref, vf_ref, k16_ref, v16_ref, ga_ref, gb_ref, qi_ref,
               small_ref, ki_ref, kb0_ref, kb1_ref, smt_ref):
    x = x_ref[...]
    h = x * lax.rsqrt(jnp.mean(x * x, -1, keepdims=True) + EPS) * nw_ref[...]
    h = h * (1.0 + sc_ref[...]) + sh_ref[...]
    hb = h.astype(BF16)

    def proj(a, b):
        return _mm(hb, w_ref[:, a:b])

    for c in range(0, 3072, 1024):
        conv_ref[:, c:c + 1024] = proj(C_CONV + c, C_CONV + c + 1024)
    gz_ref[...] = proj(C_GZ, C_Q)
    q_ref[...] = (proj(C_Q, C_K) * ATT_SCALE).astype(BF16)
    k = proj(C_K, C_V)
    kf_ref[...] = k
    k16_ref[...] = k.astype(BF16)
    v = proj(C_V, C_GA)
    vf_ref[...] = v
    v16_ref[...] = v.astype(BF16)
    ga_ref[...] = _sigmoid(proj(C_GA, C_GB))
    gb_ref[...] = _sigmoid(proj(C_GB, C_QI))
    qi_ref[...] = proj(C_QI, C_SM).astype(BF16)
    z = proj(C_SM, C_SM2)
    small_ref[...] = z
    smt_ref[...] = _nt(wt_ref[...], hb)
    lane = lax.broadcasted_iota(I32, z.shape, 1)
    for zz, lo, out_ref in ((z, True, kb0_ref), (proj(C_SM2, C_END), False, kb1_ref)):
        m = (lane < IDX_DIM) if lo else (lane >= IDX_DIM)
        zk = jnp.where(m, zz, 0.0)
        mean = jnp.sum(zk, -1, keepdims=True) * (1.0 / IDX_DIM)
        xc = jnp.where(m, zz - mean, 0.0)
        var = jnp.sum(xc * xc, -1, keepdims=True) * (1.0 / IDX_DIM)
        row = 0 if lo else 1
        kn = xc * lax.rsqrt(var + EPS) * lnw_ref[row:row + 1, :] + lnb_ref[row:row + 1, :]
        out_ref[...] = kn.astype(BF16)
        if lo:
            ki_ref[...] = kn[:, :IDX_DIM]


def in_proj(x, sh, sc, norm_w, w_cat, w_smt, lnw2, lnb2, tm=256):
    t = x.shape[0]
    tm = min(tm, t)
    per_row = sh.shape[0] != 1
    mod_spec = (pl.BlockSpec((tm, D_MODEL), lambda i: (i, 0)) if per_row
                else pl.BlockSpec((1, D_MODEL), lambda i: (0, 0)))

    def rows(n, dt):
        return pl.BlockSpec((tm, n), lambda i: (i, 0)), jax.ShapeDtypeStruct((t, n), dt)

    outs = [rows(3072, F32), rows(1024, F32), rows(1024, BF16), rows(1024, F32), rows(1024, F32),
            rows(1024, BF16), rows(1024, BF16), rows(1024, F32), rows(1024, F32), rows(512, BF16),
            rows(128, F32), rows(IDX_DIM, F32), rows(128, BF16), rows(128, BF16),
            (pl.BlockSpec((128, tm), lambda i: (0, i)), jax.ShapeDtypeStruct((128, t), F32))]
    return pl.pallas_call(
        _in_kernel,
        grid=(t // tm,),
        in_specs=[pl.BlockSpec((tm, D_MODEL), lambda i: (i, 0)), mod_spec, mod_spec,
                  _const_spec((1, D_MODEL)), _const_spec((D_MODEL, C_END)), _const_spec((128, D_MODEL)),
                  _const_spec((2, 128)), _const_spec((2, 128))],
        out_specs=[o[0] for o in outs],
        out_shape=[o[1] for o in outs],
        compiler_params=_cp(("parallel",)),
        name="in_proj",
    )(x, sh, sc, norm_w, w_cat, w_smt, lnw2, lnb2)


def _b16(a):
    return a.astype(BF16)


def _mmb(a, b):
    return jnp.dot(_b16(a), _b16(b), preferred_element_type=F32)


def _split(a):
    hi = _b16(a)
    return hi, _b16(a - hi.astype(F32))


def _mm3(a, b):
    ah, al = a
    bh, bl = b
    return (jnp.dot(ah, bh, preferred_element_type=F32) + jnp.dot(ah, bl, preferred_element_type=F32)
            + jnp.dot(al, bh, preferred_element_type=F32))


def _gdn_kernel(*refs, R, C, N, has_hist):
    s_scr = refs[-N_HEADS:]
    refs = refs[:-N_HEADS]
    if has_hist:
        conv_ref, hist_ref, small_ref, gz_ref, s0_ref, cw_ref, misc_ref, nw_ref, o_ref, sout_ref, xbuf = refs
    else:
        (conv_ref, small_ref, smt_ref, at_ref, dtt_ref, gz_ref, s0_ref, cw_ref, misc_ref, nw_ref,
         o_ref, sout_ref, xbuf) = refs
    nch = max(R // C, 1)
    RC = nch * C
    shift = C.bit_length() - 1
    n = pl.program_id(1)

    @pl.when(n == 0)
    def _():
        xbuf[...] = jnp.zeros_like(xbuf)
        for h in range(N_HEADS):
            s_scr[h][...] = s0_ref[0, h]

    x = conv_ref[...]
    validf = None
    if has_hist:
        rows_r = lax.broadcasted_iota(I32, (R, 1), 0)
        x = jnp.where((rows_r % GROUP) < FIRST_TOK, hist_ref[...], x)
        rows_c = lax.broadcasted_iota(I32, (RC, 1), 0)
        validf = jnp.where((rows_c < R) & ((rows_c % GROUP) >= FIRST_TOK), 1.0, 0.0)
    xbuf[8:8 + R, :] = x

    small = small_ref[...]
    if R < RC:
        small = jnp.concatenate([small, jnp.zeros((RC - R, 128), F32)], axis=0)
    misc = misc_ref[...]
    gfull = -jnp.exp(misc[0:1]) * _softplus(small + misc[1:2])
    betaf = _sigmoid(small)
    if validf is not None:
        gfull = gfull * validf
        betaf = betaf * validf
    ri = lax.broadcasted_iota(I32, (RC, RC), 0)
    ci = lax.broadcasted_iota(I32, (RC, RC), 1)
    same = (ri >> shift) == (ci >> shift)
    gcol = _mm(jnp.where(same & (ri >= ci), 1.0, 0.0), gfull, HI)
    if not has_hist:
        gt = -jnp.exp(at_ref[...]) * _softplus(smt_ref[...] + dtt_ref[...])
        grow_all = _mm(gt[L_A:L_A + N_HEADS, :], jnp.where(same & (ri <= ci), 1.0, 0.0), HI)
    rc = lax.broadcasted_iota(I32, (C, C), 0)
    cc = lax.broadcasted_iota(I32, (C, C), 1)
    tril = rc >= cc
    strict = rc > cc
    eye = rc == cc
    ones_cc = jnp.ones((C, C), F32)
    w = cw_ref[...]

    pre = []
    zrows = jnp.zeros((128 - C, 128), F32)
    for c in range(nch):
        r0 = c * C
        y = (w[3:4] * xbuf[8 + r0:8 + r0 + C, :] + w[2:3] * xbuf[7 + r0:7 + r0 + C, :]
             + w[1:2] * xbuf[6 + r0:6 + r0 + C, :] + w[0:1] * xbuf[5 + r0:5 + r0 + C, :])
        y = _silu(y)
        vf = None if validf is None else validf[r0:r0 + C]
        for h in range(N_HEADS):
            q = y[:, h * 128:(h + 1) * 128]
            k = y[:, 1024 + h * 128:1024 + (h + 1) * 128]
            v = y[:, 2048 + h * 128:2048 + (h + 1) * 128]
            q = q * lax.rsqrt(jnp.sum(q * q, -1, keepdims=True) + EPS) * (HEAD_DIM ** -0.5)
            k = k * lax.rsqrt(jnp.sum(k * k, -1, keepdims=True) + EPS)
            if vf is not None:
                k = k * vf
                v = v * vf
            bc = jnp.broadcast_to(betaf[r0:r0 + C, L_B + h:L_B + h + 1], (C, 128))
            gc = jnp.broadcast_to(gcol[r0:r0 + C, L_A + h:L_A + h + 1], (C, 128))
            gb = gc[:, :C]
            if has_hist:
                grow = _mm(ones_cc, jnp.where(eye, gb, 0.0), HI)
            else:
                grow = jnp.broadcast_to(grow_all[h:h + 1, r0:r0 + C], (C, C))
            dec = jnp.where(tril, jnp.exp(jnp.minimum(gb - grow, 0.0)), 0.0)
            eg = jnp.exp(gc)
            glast = gc[C - 1:C, :]
            kd = k * jnp.exp(glast - gc)
            kdt = jnp.concatenate([kd, zrows], axis=0).T[:, :C]
            pre.append((q, k, v, bc, dec, eg, jnp.exp(glast), _b16(kdt)))

    nprob = nch * N_HEADS
    a_l = []
    for q, k, v, bc, dec, eg, ge, kdt in pre:
        kh, kl = _split(k)
        a_l.append(jnp.where(strict, bc[:, :C] * (_nt(kh, kh) + _nt(kh, kl) + _nt(kl, kh)) * dec, 0.0))
    xo_l = [-a for a in a_l]
    pw_l = []
    for a in a_l:
        a_s = _split(a)
        pw_l.append(_mm3(a_s, a_s))
    span = 2
    while span < C:
        span *= 2
        pws_l = [_split(pw) for pw in pw_l]
        xo_l = [xo + pw + _mm3(_split(xo), pws) for xo, pw, pws in zip(xo_l, pw_l, pws_l)]
        if span < C:
            pw_l = [_mm3(pws, pws) for pws in pws_l]
    um_l, wm_l, qk_l = [], [], []
    for (q, k, v, bc, dec, eg, ge, kdt), xo in zip(pre, xo_l):
        rv = v * bc
        rk = k * (bc * eg)
        xo_s = _split(xo)
        um_l.append(rv + _mm3(xo_s, _split(rv)))
        wm_l.append(rk + _mm3(xo_s, _split(rk)))
        qk_l.append(_nt(_b16(q), _b16(k)) * dec)

    outs = [None] * nprob
    for c in range(nch):
        probs = range(c * N_HEADS, (c + 1) * N_HEADS)
        s_l = [s_scr[h][...] for h in range(N_HEADS)]
        sb_l = [_b16(s) for s in s_l]
        u_l = [um_l[p] - _mm(_b16(wm_l[p]), sb) for p, sb in zip(probs, sb_l)]
        for h, p in enumerate(probs):
            q, k, v, bc, dec, eg, ge, kdt = pre[p]
            outs[p] = _mm(_b16(q * eg), sb_l[h]) + _mmb(qk_l[p], u_l[h])
            s_scr[h][...] = s_l[h] * ge + _mm(kdt, _b16(u_l[h]))

    ro = min(R, C)
    for c in range(nch):
        r0 = c * C
        for h in range(N_HEADS):
            o = outs[c * N_HEADS + h][:ro]
            on = o * lax.rsqrt(jnp.mean(o * o, -1, keepdims=True) + EPS) * nw_ref[...]
            gz = gz_ref[r0:r0 + ro, h * 128:(h + 1) * 128]
            o_ref[r0:r0 + ro, h * 128:(h + 1) * 128] = (on * _silu(gz)).astype(o_ref.dtype)

    if N > 1:
        xbuf[0:8, :] = xbuf[RC:RC + 8, :]

    @pl.when(n == N - 1)
    def _():
        for h in range(N_HEADS):
            sout_ref[0, h] = s_scr[h][...]


def gdn(conv_in, hist, small, smt, gz, s0, conv_w, misc, norm_w, *, nseq, rows, out_dtype):
    C = GDN_CHUNK
    t = conv_in.shape[0]
    nstep = t // (nseq * rows)
    has_hist = hist is not None
    rc = max(rows // C, 1) * C

    def rspec(n):
        return pl.BlockSpec((rows, n), lambda b, c: (b * nstep + c, 0))

    def cspec(shape):
        return pl.BlockSpec(shape, lambda b, c: (0,) * len(shape))

    if has_hist:
        in_specs = [rspec(3072), rspec(3072), rspec(128), rspec(1024)]
        args = [conv_in, hist, small, gz]
    else:
        lane_rep = jnp.ones((1, rows), F32)
        at = misc[0].reshape(128, 1) * lane_rep
        dtt = misc[1].reshape(128, 1) * lane_rep
        in_specs = [rspec(3072), rspec(128), pl.BlockSpec((128, rows), lambda b, c: (0, b * nstep + c)),
                    cspec((128, rows)), cspec((128, rows)), rspec(1024)]
        args = [conv_in, small, smt, at, dtt, gz]
    in_specs += [pl.BlockSpec((1, N_HEADS, 128, 128), lambda b, c: (b, 0, 0, 0)),
                 cspec((GDN_CONV, 3072)), cspec((8, 128)), cspec((1, 128))]
    args += [s0, conv_w, misc, norm_w]
    return pl.pallas_call(
        functools.partial(_gdn_kernel, R=rows, C=C, N=nstep, has_hist=has_hist),
        grid=(nseq, nstep),
        in_specs=in_specs,
        out_specs=[rspec(1024), pl.BlockSpec((1, N_HEADS, 128, 128), lambda b, c: (b, 0, 0, 0))],
        out_shape=[jax.ShapeDtypeStruct((t, 1024), out_dtype),
                   jax.ShapeDtypeStruct((nseq, N_HEADS, 128, 128), F32)],
        scratch_shapes=[pltpu.VMEM((8 + rc, 3072), F32)] + [pltpu.VMEM((128, 128), F32)] * N_HEADS,
        compiler_params=_cp(("arbitrary", "arbitrary")),
        name="gdn",
    )(*args)


def _sort_key(s):
    bits = pltpu.bitcast(s + 0.0, I32)
    return bits ^ ((bits >> 31) & 0x7FFFFFFF)


def _count_ge(keys_ref, nkt, cand, tq, tk):
    def body(j, acc):
        kt = keys_ref[:, pl.ds(pl.multiple_of(j * tk, tk), tk)]
        c = jnp.where(kt >= cand, 1, 0)
        for g in range(tk // 128):
            acc = acc + c[:, g * 128:(g + 1) * 128]
        return acc

    acc = lax.fori_loop(0, nkt, body, jnp.zeros((tq, 128), I32))
    return jnp.sum(acc, axis=-1, keepdims=True)


def _kth_largest(keys_ref, nkt, rmax, tq, tk, k):
    hi = _sort_key(rmax) + 1
    lo0 = _sort_key(jnp.where(rmax >= 0.0, rmax * 0.125, rmax * 8.0))
    c0 = _count_ge(keys_ref, nkt, lo0, tq, tk)
    ok0 = c0 >= k
    lo = jnp.where(ok0, lo0, INT_MIN)
    cnt = jnp.where(ok0, c0, nkt * tk)
    hi = jnp.where(ok0, hi, lo0)

    def mid_of(lo, hi):
        return (lo >> 1) + (hi >> 1) + (lo & hi & 1)

    def n_active(lo, hi, cnt):
        return jnp.max(jnp.where((cnt != k) & (mid_of(lo, hi) != lo), 1, 0))

    def cond(st):
        return st[3] > 0

    def body(st):
        lo, hi, cnt, _ = st
        mid = mid_of(lo, hi)
        active = (cnt != k) & (mid != lo)
        c = _count_ge(keys_ref, nkt, mid, tq, tk)
        up = active & (c >= k)
        down = active & (c < k)
        lo = jnp.where(up, mid, lo)
        cnt = jnp.where(up, c, cnt)
        hi = jnp.where(down, mid, hi)
        return lo, hi, cnt, n_active(lo, hi, cnt)

    lo, _, cnt, _ = lax.while_loop(cond, body, (lo, hi, cnt, n_active(lo, hi, cnt)))
    return lo, cnt


def _emit_selection(keys_ref, nkt, thr, cnt, tq, tk, k, emit):
    tied = (cnt > k) & (thr > KEY_NEG_INF)
    any_tied = jnp.max(jnp.where(tied, 1, 0))

    @pl.when(any_tied == 0)
    def _():
        def body(j, carry):
            kt = keys_ref[:, pl.ds(pl.multiple_of(j * tk, tk), tk)]
            emit(j, kt >= thr)
            return carry
        lax.fori_loop(0, nkt, body, 0)

    @pl.when(any_tied != 0)
    def _():
        c_gt = _count_ge(keys_ref, nkt, thr + 1, tq, tk)
        budget = jnp.where(tied, k - c_gt, nkt * tk).astype(F32)
        ri = lax.broadcasted_iota(I32, (tk, tk), 0)
        ci = lax.broadcasted_iota(I32, (tk, tk), 1)
        upper = jnp.where(ri <= ci, 1.0, 0.0).astype(BF16)

        def body(j, seen):
            kt = keys_ref[:, pl.ds(pl.multiple_of(j * tk, tk), tk)]
            eq = jnp.where(kt == thr, 1.0, 0.0)
            rank = _mm(eq.astype(BF16), upper) + seen
            sel = jnp.where(kt > thr, 1.0, jnp.where(rank <= budget, eq, 0.0))
            emit(j, sel > 0.5)
            return seen + jnp.sum(eq, axis=-1, keepdims=True)
        lax.fori_loop(0, nkt, body, jnp.zeros((tq, 1), F32))


def _idx_kernel(qi_ref, small_ref, kb0_ref, kb1_ref, mask_ref, keys_ref, *, TQ, TK, T, k):
    i = pl.program_id(0)
    nkt = (i * TQ + TQ + TK - 1) // TK
    w = small_ref[:, L_WI:L_WI + IDX_HEADS] * IDX_W_SCALE
    pos_q = i * TQ + lax.broadcasted_iota(I32, (TQ, 1), 0)

    def score_tile(j, rmax):
        off = pl.multiple_of(j * TK, TK)
        ka = kb0_ref[pl.ds(off, TK), :]
        kb = kb1_ref[pl.ds(off, TK), :]
        acc = jnp.zeros((TQ, TK), F32)
        for p in range(IDX_HEADS // 2):
            q2 = qi_ref[:, p * 128:(p + 1) * 128]
            acc = acc + w[:, 2 * p:2 * p + 1] * jnp.maximum(_nt(q2, ka), 0.0)
            acc = acc + w[:, 2 * p + 1:2 * p + 2] * jnp.maximum(_nt(q2, kb), 0.0)
        pos_k = off + lax.broadcasted_iota(I32, (1, TK), 1)
        acc = jnp.where(pos_k <= pos_q, acc, NEG_INF)
        keys_ref[:, pl.ds(off, TK)] = _sort_key(acc)
        for g in range(TK // 128):
            rmax = jnp.maximum(rmax, acc[:, g * 128:(g + 1) * 128])
        return rmax

    rmax = lax.fori_loop(0, nkt, score_tile, jnp.full((TQ, 128), NEG_INF, F32))
    thr, cnt = _kth_largest(keys_ref, nkt, jnp.max(rmax, axis=-1, keepdims=True), TQ, TK, k)

    def emit(j, sel):
        off = pl.multiple_of(j * TK, TK)
        pos_k = off + lax.broadcasted_iota(I32, (1, TK), 1)
        m = jnp.where(pos_k <= pos_q, jnp.where(sel, 1, 0), 0)
        mask_ref[:, pl.ds(off, TK)] = m.astype(jnp.int8)

    _emit_selection(keys_ref, nkt, thr, cnt, TQ, TK, k, emit)

    def zero_tile(j, carry):
        mask_ref[:, pl.ds(pl.multiple_of(j * TK, TK), TK)] = jnp.zeros((TQ, TK), jnp.int8)
        return carry

    lax.fori_loop(nkt, T // TK, zero_tile, 0)


def prompt_topk_mask(qi16, small, kb0, kb1, k, TQ=128, TK=512):
    t = qi16.shape[0]
    TK = min(TK, t)
    return pl.pallas_call(
        functools.partial(_idx_kernel, TQ=TQ, TK=TK, T=t, k=k),
        grid=(t // TQ,),
        in_specs=[pl.BlockSpec((TQ, 512), lambda i: (i, 0)),
                  pl.BlockSpec((TQ, 128), lambda i: (i, 0)),
                  _const_spec((t, 128)), _const_spec((t, 128))],
        out_specs=pl.BlockSpec((TQ, t), lambda i: (i, 0)),
        out_shape=jax.ShapeDtypeStruct((t, t), jnp.int8),
        scratch_shapes=[pltpu.VMEM((TQ, t), I32)],
        compiler_params=_cp(("parallel",)),
        name="prompt_topk_mask",
    )(qi16, small, kb0, kb1)


def _att_kernel(qa_ref, ka_ref, q_ref, k_ref, v_ref, mask_ref, e_ref, o_ref, *scr, TA, RS):
    m_scr, l_scr, acc_scr = scr[0:8], scr[8:16], scr[16:24]
    s_scr, p_scr, b_scr = scr[24:26], scr[26:28], scr[28]
    p = pl.program_id(0)
    i = qa_ref[p]
    j = ka_ref[p]

    @pl.when(j == 0)
    def _():
        for h in range(N_HEADS):
            m_scr[h][...] = jnp.full((TA, 1), -1e30, F32)
            l_scr[h][...] = jnp.zeros((TA, HEAD_DIM), F32)
            acc_scr[h][...] = jnp.zeros((TA, HEAD_DIM), F32)

    for r0 in range(0, TA, 32):
        b_scr[r0:r0 + 32, :] = jnp.where(mask_ref[r0:r0 + 32, :].astype(I32) != 0, 0.0, NEG_INF)

    def update(near):
        nb = TA // 128
        if near:
            fd = jnp.where(j == i, 1.0, 0.0)
        for h in range(N_HEADS):
            hs = slice(h * 128, (h + 1) * 128)
            sb = s_scr[h % 2]
            pb = p_scr[h % 2]
            sb[...] = _nt(q_ref[:, hs], k_ref[:, hs])
            if near:
                for b in range(nb):
                    bs = slice(b * 128, (b + 1) * 128)
                    sb[bs, bs] += fd * e_ref[0, h]
                    if b + 1 < nb:
                        sb[(b + 1) * 128:(b + 2) * 128, bs] += fd * e_ref[1, h]
                sb[0:128, TA - 128:TA] += (1.0 - fd) * e_ref[1, h]
            m_all = m_scr[h][...]
            m_parts = []
            for r0 in range(0, TA, RS):
                rs = slice(r0, r0 + RS)
                sm = sb[rs, :] + b_scr[rs, :]
                m_parts.append(jnp.maximum(m_all[rs], jnp.max(sm, axis=-1, keepdims=True)))
            for n, r0 in enumerate(range(0, TA, RS)):
                rs = slice(r0, r0 + RS)
                sm = sb[rs, :] + b_scr[rs, :]
                pb[rs, :] = jnp.exp(sm - m_parts[n]).astype(BF16)
            m_new = jnp.concatenate(m_parts, axis=0)
            alpha = jnp.exp(m_all - m_new)
            m_scr[h][...] = m_new
            pr = pb[...]
            l_scr[h][...] = alpha * l_scr[h][...] + _mm(pr, jnp.ones((TA, HEAD_DIM), BF16))
            acc_scr[h][...] = alpha * acc_scr[h][...] + _mm(pr, v_ref[:, hs])

    @pl.when(j >= i - 1)
    def _():
        update(True)

    @pl.when(j < i - 1)
    def _():
        update(False)

    @pl.when(j == i)
    def _():
        for h in range(N_HEADS):
            o_ref[:, h * 128:(h + 1) * 128] = (acc_scr[h][...] / l_scr[h][...]).astype(o_ref.dtype)


def prompt_attention(q16, k16, v16, mask, e_tab, TA=512, RS=32):
    t = q16.shape[0]
    TA = min(TA, t)
    nb = t // TA
    qa = np.concatenate([np.full(i + 1, i, np.int32) for i in range(nb)])
    ka = np.concatenate([np.arange(i + 1, dtype=np.int32) for i in range(nb)])
    grid_spec = pltpu.PrefetchScalarGridSpec(
        num_scalar_prefetch=2,
        grid=(len(qa),),
        in_specs=[pl.BlockSpec((TA, 1024), lambda p, qa, ka: (qa[p], 0)),
                  pl.BlockSpec((TA, 1024), lambda p, qa, ka: (ka[p], 0)),
                  pl.BlockSpec((TA, 1024), lambda p, qa, ka: (ka[p], 0)),
                  pl.BlockSpec((TA, TA), lambda p, qa, ka: (qa[p], ka[p])),
                  pl.BlockSpec((2, N_HEADS, 128, 128), lambda p, qa, ka: (0, 0, 0, 0))],
        out_specs=pl.BlockSpec((TA, 1024), lambda p, qa, ka: (qa[p], 0)),
        scratch_shapes=([pltpu.VMEM((TA, 1), F32)] * N_HEADS + [pltpu.VMEM((TA, HEAD_DIM), F32)] * (2 * N_HEADS)
                        + [pltpu.VMEM((TA, TA), F32)] * 2 + [pltpu.VMEM((TA, TA), BF16)] * 2
                        + [pltpu.VMEM((TA, TA), F32)]))
    return pl.pallas_call(
        functools.partial(_att_kernel, TA=TA, RS=RS),
        grid_spec=grid_spec,
        out_shape=jax.ShapeDtypeStruct((t, 1024), BF16),
        compiler_params=_cp(("arbitrary",)),
        name="prompt_attention",
    )(jnp.asarray(qa), jnp.asarray(ka), q16, k16, v16, mask, e_tab)


def _page_copies(pt_ref, cache_ref, buf_ref, sem_ref, b, first_page, npages, slot):
    out = []
    for g in range(npages):
        pid = pt_ref[b, first_page + g]
        out.append(pltpu.make_async_copy(cache_ref.at[0, pid], buf_ref.at[slot, pl.ds(g * PAGE, PAGE)],
                                         sem_ref.at[slot]))
    return out


def _sidx_kernel(pt_ref, qm_ref, w_ref, knew_ref, cache_ref, sc_ref, kbuf, sem, *, NP, NB):
    b = pl.program_id(0)
    slot = b % 2

    @pl.when(b == 0)
    def _():
        for c in _page_copies(pt_ref, cache_ref, kbuf, sem, 0, 0, NP, 0):
            c.start()

    @pl.when(b + 1 < NB)
    def _():
        for c in _page_copies(pt_ref, cache_ref, kbuf, sem, b + 1, 0, NP, 1 - slot):
            c.start()

    for c in _page_copies(pt_ref, cache_ref, kbuf, sem, b, 0, NP, slot):
        c.wait()

    qm = qm_ref[0]
    w = w_ref[0]

    def fold(s):
        s = jnp.maximum(s, 0.0) * w
        out = s[0:GROUP]
        for h in range(1, IDX_HEADS):
            out = out + s[h * GROUP:(h + 1) * GROUP]
        return out

    past = NP * PAGE
    CH = 2048
    for c0 in range(0, past, CH):
        kc = kbuf[slot, c0:c0 + CH, :].astype(BF16)
        sc_ref[0, :, c0:c0 + CH] = fold(_nt(qm, kc))
    knew = jnp.concatenate([knew_ref[...], jnp.zeros((128 - GROUP, IDX_DIM), F32)], axis=0).astype(BF16)
    tail = fold(_nt(qm, knew))
    r = lax.broadcasted_iota(I32, (GROUP, 128), 0)
    c = lax.broadcasted_iota(I32, (GROUP, 128), 1)
    sc_ref[0, :, past:past + 128] = jnp.where((c >= FIRST_TOK) & (c <= r), tail, NEG_INF)


def sample_scores(page_table, qm, wcol, ki_s, cache_idx):
    nb, npg = page_table.shape
    past = npg * PAGE
    grid_spec = pltpu.PrefetchScalarGridSpec(
        num_scalar_prefetch=1,
        grid=(nb,),
        in_specs=[pl.BlockSpec((1, IDX_HEADS * GROUP, IDX_DIM), lambda b, pt: (b, 0, 0)),
                  pl.BlockSpec((1, IDX_HEADS * GROUP, 1), lambda b, pt: (b, 0, 0)),
                  pl.BlockSpec((GROUP, IDX_DIM), lambda b, pt: (b, 0)),
                  pl.BlockSpec(memory_space=pl.ANY)],
        out_specs=pl.BlockSpec((1, GROUP, past + 128), lambda b, pt: (b, 0, 0)),
        scratch_shapes=[pltpu.VMEM((2, past, IDX_DIM), F32), pltpu.SemaphoreType.DMA((2,))])
    return pl.pallas_call(
        functools.partial(_sidx_kernel, NP=npg, NB=nb),
        grid_spec=grid_spec,
        out_shape=jax.ShapeDtypeStruct((nb, GROUP, past + 128), F32),
        compiler_params=_cp(("arbitrary",)),
        name="sample_scores",
    )(page_table, qm, wcol, ki_s, cache_idx)


def _ssel_kernel(sc_ref, am_ref, keys_ref, *, TQ, TK, W, k):
    nkt = W // TK

    def to_keys(j, rmax):
        off = pl.multiple_of(j * TK, TK)
        s = sc_ref[:, pl.ds(off, TK)]
        keys_ref[:, pl.ds(off, TK)] = _sort_key(s)
        for g in range(TK // 128):
            rmax = jnp.maximum(rmax, s[:, g * 128:(g + 1) * 128])
        return rmax

    rmax = lax.fori_loop(0, nkt, to_keys, jnp.full((TQ, 128), NEG_INF, F32))
    thr, cnt = _kth_largest(keys_ref, nkt, jnp.max(rmax, axis=-1, keepdims=True), TQ, TK, k)

    def emit(j, sel):
        off = pl.multiple_of(j * TK, TK)
        valid = sc_ref[:, pl.ds(off, TK)] > NEG_INF
        am_ref[:, pl.ds(off, TK)] = jnp.where(valid, jnp.where(sel, 0.0, NEG_INF), NEG_INF)

    _emit_selection(keys_ref, nkt, thr, cnt, TQ, TK, k, emit)


def sample_topk_mask(scores, k, TQ=32, TK=128):
    r, w = scores.shape
    TQ = min(TQ, r)
    return pl.pallas_call(
        functools.partial(_ssel_kernel, TQ=TQ, TK=TK, W=w, k=k),
        grid=(r // TQ,),
        in_specs=[pl.BlockSpec((TQ, w), lambda i: (i, 0))],
        out_specs=pl.BlockSpec((TQ, w), lambda i: (i, 0)),
        out_shape=jax.ShapeDtypeStruct((r, w), F32),
        scratch_shapes=[pltpu.VMEM((TQ, w), I32)],
        compiler_params=_cp(("parallel",)),
        name="sample_topk_mask",
    )(scores)


def _satt_kernel(pt_ref, q_ref, am_ref, knew_ref, vnew_ref, bl_ref, bt_ref, ck_ref, cv_ref, o_ref,
                 kbuf, vbuf, ksem, vsem, m_scr, l_scr, acc_scr, *, PC, NCH, NB):
    b = pl.program_id(0)
    c = pl.program_id(1)
    g = b * NCH + c
    slot = g % 2
    CW = PC * PAGE

    def copies(bq, cq, sl):
        return (_page_copies(pt_ref, ck_ref, kbuf, ksem, bq, cq * PC, PC, sl)
                + _page_copies(pt_ref, cv_ref, vbuf, vsem, bq, cq * PC, PC, sl))

    @pl.when(g == 0)
    def _():
        for cp in copies(0, 0, 0):
            cp.start()

    @pl.when(g + 1 < NB * NCH)
    def _():
        nxt = g + 1
        for cp in copies(nxt // NCH, nxt % NCH, 1 - slot):
            cp.start()

    for cp in copies(b, c, slot):
        cp.wait()

    @pl.when(c == 0)
    def _():
        m_scr[...] = jnp.full(m_scr.shape, -1e30, F32)
        l_scr[...] = jnp.zeros(l_scr.shape, F32)
        acc_scr[...] = jnp.zeros(acc_scr.shape, F32)

    def online(h, s, v16):
        m_old = m_scr[h]
        m_new = jnp.maximum(m_old, jnp.max(s, axis=-1, keepdims=True))
        alpha = jnp.exp(m_old - m_new)
        pr = jnp.exp(s - m_new)
        l_scr[h] = alpha * l_scr[h] + jnp.sum(pr, axis=-1, keepdims=True)
        acc_scr[h] = alpha * acc_scr[h] + _mm(_b16(pr), v16)
        m_scr[h] = m_new

    last = jnp.where(c == NCH - 1, 1.0, 0.0)
    am = am_ref[0, :, pl.ds(pl.multiple_of(c * CW, CW), CW)]
    for h in range(N_HEADS):
        s = _nt(_b16(q_ref[0, h]), _b16(kbuf[slot, :, h, :])) + am
        near = last * bl_ref[h * GROUP:(h + 1) * GROUP, :]
        s = s + jnp.concatenate([jnp.zeros((GROUP, CW - 128), F32), near], axis=1)
        online(h, s, _b16(vbuf[slot, :, h, :]))

    @pl.when(c == NCH - 1)
    def _():
        zpad = jnp.zeros((128 - GROUP, 128), F32)
        am_tail = am_ref[0, :, NCH * CW:NCH * CW + 128]
        for h in range(N_HEADS):
            hs = slice(h * 128, (h + 1) * 128)
            kn = _b16(jnp.concatenate([knew_ref[:, hs], zpad], axis=0))
            vn = _b16(jnp.concatenate([vnew_ref[:, hs], zpad], axis=0))
            s2 = _nt(_b16(q_ref[0, h]), kn) + am_tail + bt_ref[h * GROUP:(h + 1) * GROUP, :]
            online(h, s2, vn)
            o_ref[:, hs] = acc_scr[h] / l_scr[h]


def sample_attention(page_table, q4, addmask, kf_s, vf_s, b_last, b_tail, cache_k, cache_v, PC=8):
    nb, npg = page_table.shape
    PC = min(PC, npg)
    nch = npg // PC
    w = addmask.shape[-1]
    grid_spec = pltpu.PrefetchScalarGridSpec(
        num_scalar_prefetch=1,
        grid=(nb, nch),
        in_specs=[pl.BlockSpec((1, N_HEADS, GROUP, HEAD_DIM), lambda b, c, pt: (b, 0, 0, 0)),
                  pl.BlockSpec((1, GROUP, w), lambda b, c, pt: (b, 0, 0)),
                  pl.BlockSpec((GROUP, 1024), lambda b, c, pt: (b, 0)),
                  pl.BlockSpec((GROUP, 1024), lambda b, c, pt: (b, 0)),
                  pl.BlockSpec((N_HEADS * GROUP, 128), lambda b, c, pt: (0, 0)),
                  pl.BlockSpec((N_HEADS * GROUP, 128), lambda b, c, pt: (0, 0)),
                  pl.BlockSpec(memory_space=pl.ANY),
                  pl.BlockSpec(memory_space=pl.ANY)],
        out_specs=pl.BlockSpec((GROUP, 1024), lambda b, c, pt: (b, 0)),
        scratch_shapes=[pltpu.VMEM((2, PC * PAGE, N_HEADS, HEAD_DIM), F32),
                        pltpu.VMEM((2, PC * PAGE, N_HEADS, HEAD_DIM), F32),
                        pltpu.SemaphoreType.DMA((2,)), pltpu.SemaphoreType.DMA((2,)),
                        pltpu.VMEM((N_HEADS, GROUP, 1), F32), pltpu.VMEM((N_HEADS, GROUP, 1), F32),
                        pltpu.VMEM((N_HEADS, GROUP, HEAD_DIM), F32)])
    return pl.pallas_call(
        functools.partial(_satt_kernel, PC=PC, NCH=nch, NB=nb),
        grid_spec=grid_spec,
        out_shape=jax.ShapeDtypeStruct((nb * GROUP, 1024), F32),
        compiler_params=_cp(("arbitrary", "arbitrary")),
        name="sample_attention",
    )(page_table, q4, addmask, kf_s, vf_s, b_last, b_tail, cache_k, cache_v)


def _mix_kernel(oa_ref, ob_ref, ga_ref, gb_ref, x_ref, g1_ref, wa_ref, wb_ref, wo_ref, o_ref):
    ya = _mm(oa_ref[...].astype(BF16), wa_ref[...])
    yb = _mm(ob_ref[...].astype(BF16), wb_ref[...])
    mixed = ga_ref[...] * ya + gb_ref[...] * yb
    o_ref[...] = x_ref[...] + g1_ref[...] * _mm(mixed.astype(BF16), wo_ref[...])


def mix(oa, ob, ga, gb, x, g1, wa, wb, wo, tm=512):
    t = x.shape[0]
    tm = min(tm, t)
    per_row = g1.shape[0] != 1
    mod_spec = (pl.BlockSpec((tm, D_MODEL), lambda i: (i, 0)) if per_row
                else pl.BlockSpec((1, D_MODEL), lambda i: (0, 0)))
    row = pl.BlockSpec((tm, D_MODEL), lambda i: (i, 0))
    wspec = _const_spec((D_MODEL, D_MODEL))
    return pl.pallas_call(
        _mix_kernel,
        grid=(t // tm,),
        in_specs=[row, row, row, row, row, mod_spec, wspec, wspec, wspec],
        out_specs=row,
        out_shape=jax.ShapeDtypeStruct((t, D_MODEL), F32),
        compiler_params=_cp(("parallel",)),
        name="mix",
    )(oa, ob, ga, gb, x, g1, wa, wb, wo)


def _ffn_kernel(*refs, TM, has_hist):
    if has_hist:
        (x_ref, sh_ref, sc_ref, g2_ref, hist_ref, n2_ref, wup_ref, cw_ref, cb_ref, wdn_ref, fw_ref,
         y_ref, u_ref, ubuf) = refs
    else:
        (x_ref, sh_ref, sc_ref, g2_ref, n2_ref, wup_ref, cw_ref, cb_ref, wdn_ref, fw_ref,
         y_ref, u_ref, ubuf) = refs
    i = pl.program_id(0)

    @pl.when(i == 0)
    def _():
        ubuf[0:8, :] = jnp.zeros((8, 2 * D_FF), F32)

    x = x_ref[...]
    h = x * lax.rsqrt(jnp.mean(x * x, -1, keepdims=True) + EPS) * n2_ref[...]
    h = h * (1.0 + sc_ref[...]) + sh_ref[...]
    hb = h.astype(BF16)
    if has_hist:
        is_hist = (lax.broadcasted_iota(I32, (TM, 1), 0) % GROUP) < FIRST_TOK
    CB = 256
    acc = jnp.zeros((TM, D_MODEL), F32)
    for c in range(0, D_FF, CB):
        halves = []
        for base in (c, D_FF + c):
            cs = slice(base, base + CB)
            u = _mm(hb, wup_ref[:, cs])
            if has_hist:
                u = jnp.where(is_hist, hist_ref[:, cs], u)
            ubuf[8:8 + TM, cs] = u
            halves.append(cw_ref[2:3, cs] * u + cw_ref[1:2, cs] * ubuf[7:7 + TM, cs]
                          + cw_ref[0:1, cs] * ubuf[6:6 + TM, cs] + cb_ref[:, cs])
        act = _silu(halves[0]) * halves[1]
        acc = acc + _mm(act.astype(BF16), wdn_ref[c:c + CB, :])
    if has_hist:
        u_ref[...] = ubuf[8:8 + TM, :]
    else:
        u_ref[...] = ubuf[TM:TM + 8, :]
    ubuf[0:8, :] = ubuf[TM:TM + 8, :]
    x2 = x + g2_ref[...] * acc
    y_ref[...] = x2 * lax.rsqrt(jnp.mean(x2 * x2, -1, keepdims=True) + EPS) * fw_ref[...]


def ffn(x1, sh, sc, g2, hist, norm2_w, w_up, conv_w, conv_b, w_down, final_w, tm=256):
    t = x1.shape[0]
    tm = min(tm, t)
    has_hist = hist is not None
    per_row = sh.shape[0] != 1
    mod_spec = (pl.BlockSpec((tm, D_MODEL), lambda i: (i, 0)) if per_row
                else pl.BlockSpec((1, D_MODEL), lambda i: (0, 0)))
    row = pl.BlockSpec((tm, D_MODEL), lambda i: (i, 0))
    in_specs = [row, mod_spec, mod_spec, mod_spec]
    args = [x1, sh, sc, g2]
    if has_hist:
        in_specs.append(pl.BlockSpec((tm, 2 * D_FF), lambda i: (i, 0)))
        args.append(hist)
    in_specs += [_const_spec((1, D_MODEL)), _const_spec((D_MODEL, 2 * D_FF)), _const_spec((FFN_CONV, 2 * D_FF)),
                 _const_spec((1, 2 * D_FF)), _const_spec((D_FF, D_MODEL)), _const_spec((1, D_MODEL))]
    args += [norm2_w, w_up, conv_w, conv_b, w_down, final_w]
    if has_hist:
        u_spec = pl.BlockSpec((tm, 2 * D_FF), lambda i: (i, 0))
        u_shape = jax.ShapeDtypeStruct((t, 2 * D_FF), F32)
    else:
        u_spec = pl.BlockSpec((8, 2 * D_FF), lambda i: (0, 0))
        u_shape = jax.ShapeDtypeStruct((8, 2 * D_FF), F32)
    return pl.pallas_call(
        functools.partial(_ffn_kernel, TM=tm, has_hist=has_hist),
        grid=(t // tm,),
        in_specs=in_specs,
        out_specs=[row, u_spec],
        out_shape=[jax.ShapeDtypeStruct((t, D_MODEL), F32), u_shape],
        scratch_shapes=[pltpu.VMEM((8 + tm, 2 * D_FF), F32)],
        compiler_params=_cp(("arbitrary",)),
        name="ffn",
    )(*args)


def _group_rows(a, first):
    b, n, c = a.shape
    return jnp.pad(a, ((0, 0), (first, GROUP - first - n), (0, 0))).reshape(b * GROUP, c)


def kernel(x_prompt, x_sample, cache_k, cache_v, cache_idx_k, state_gdn, state_gdn_conv, state_ffn_conv,
           page_table, c_prompt, c_sample, w_ada, b_ada, norm1_w, w_in, gdn_conv_w, gdn_A_log, gdn_dt_bias,
           gdn_norm_w, idx_knorm_w, idx_knorm_b, w_branch_a, w_branch_b, w_out, norm2_w, w_up, ffn_conv_w,
           ffn_conv_b, w_down, rel_bias, final_norm_w):
    bp, tp, _ = x_prompt.shape
    bs, ts, _ = x_sample.shape
    assert bp == 1 and ts == GROUP - FIRST_TOK and w_ada.shape[0] == 1
    npg = page_table.shape[1]
    past = npg * PAGE

    w = w_in[0]
    w_cat = jnp.concatenate(
        [w[:, 0:4096], w[:, 4112:7184], w[:, 7768:9816], w[:, 7184:7696],
         w[:, 7696:7760], w[:, 4096:4112], w[:, 7760:7768], jnp.zeros((D_MODEL, 40), F32),
         jnp.zeros((D_MODEL, 64), F32), w[:, 7696:7760]], axis=1).astype(BF16)
    w_smt = w_cat[:, C_SM:C_SM2].T
    z64 = jnp.zeros((IDX_DIM,), F32)
    lnw2 = jnp.stack([jnp.concatenate([idx_knorm_w[0], z64]), jnp.concatenate([z64, idx_knorm_w[0]])])
    lnb2 = jnp.stack([jnp.concatenate([idx_knorm_b[0], z64]), jnp.concatenate([z64, idx_knorm_b[0]])])
    misc = jnp.zeros((8, 128), F32)
    misc = misc.at[0, L_A:L_A + N_HEADS].set(gdn_A_log[0]).at[1, L_A:L_A + N_HEADS].set(gdn_dt_bias[0])
    wa16, wb16, wo16 = w_branch_a[0].astype(BF16), w_branch_b[0].astype(BF16), w_out[0].astype(BF16)
    wup16, wdn16 = w_up[0].astype(BF16), w_down[0].astype(BF16)
    n1 = norm1_w[0].reshape(1, D_MODEL)
    n2 = norm2_w[0].reshape(1, D_MODEL)
    fw = final_norm_w.reshape(1, D_MODEL)
    gnw = gdn_norm_w[0].reshape(1, 128)
    ffn_b = ffn_conv_b[0].reshape(1, 2 * D_FF)

    c_all = jnp.concatenate([c_prompt, c_sample], axis=0)
    pad_r = (-c_all.shape[0]) % 8
    mod = ada_mod(jnp.pad(c_all, ((0, pad_r), (0, 0))), w_ada[0], b_ada[0])
    mod_p = mod[0:1]
    mod_s = jnp.repeat(mod[1:1 + bs], GROUP, axis=0)

    def mods(m):
        return [m[:, i * D_MODEL:(i + 1) * D_MODEL] for i in range(6)]

    e_tab, b_last, b_tail = bias_tables(rel_bias)

    xp = x_prompt[0]
    sh1, sc1, g1, sh2, sc2, g2 = mods(mod_p)
    (conv_p, gz_p, q_p, kf_p, vf_p, k16_p, v16_p, ga_p, gb_p, qi_p, small_p, ki_p, kb0_p, kb1_p, smt_p) = in_proj(
        xp, sh1, sc1, n1, w_cat, w_smt, lnw2, lnb2)
    oa_p, s_p = gdn(conv_p, None, small_p, smt_p, gz_p, jnp.zeros((1, N_HEADS, 128, 128), F32), gdn_conv_w[0],
                    misc, gnw, nseq=1, rows=min(4 * GDN_CHUNK, tp), out_dtype=BF16)
    mask_p = prompt_topk_mask(qi_p, small_p, kb0_p, kb1_p, min(TOPK_MAX, tp // 4))
    ob_p = prompt_attention(q_p, k16_p, v16_p, mask_p, e_tab)
    x1_p = mix(oa_p, ob_p, ga_p, gb_p, xp, g1, wa16, wb16, wo16)
    y_p, utail_p = ffn(x1_p, sh2, sc2, g2, None, n2, wup16, ffn_conv_w[0], ffn_b, wdn16, fw)

    xs = _group_rows(x_sample, FIRST_TOK)
    sh1, sc1, g1, sh2, sc2, g2 = mods(mod_s)
    (conv_s, gz_s, q_s, kf_s, vf_s, _, _, ga_s, gb_s, qi_s, small_s, ki_s, _, _, _) = in_proj(
        xs, sh1, sc1, n1, w_cat, w_smt, lnw2, lnb2)
    hist_gdn = _group_rows(state_gdn_conv[0], FIRST_TOK - (GDN_CONV - 1))
    oa_s, s_s = gdn(conv_s, hist_gdn, small_s, None, gz_s, state_gdn[0], gdn_conv_w[0], misc, gnw,
                    nseq=bs, rows=GROUP, out_dtype=F32)
    qm = qi_s.reshape(bs, GROUP, IDX_HEADS, IDX_DIM).transpose(0, 2, 1, 3).reshape(bs, IDX_HEADS * GROUP, IDX_DIM)
    wcol = (small_s[:, L_WI:L_WI + IDX_HEADS] * IDX_W_SCALE).reshape(bs, GROUP, IDX_HEADS)
    wcol = wcol.transpose(0, 2, 1).reshape(bs, IDX_HEADS * GROUP, 1)
    scores = sample_scores(page_table, qm, wcol, ki_s, cache_idx_k)
    sc_tok = scores[:, FIRST_TOK:, :].reshape(bs * ts, past + 128)
    am = sample_topk_mask(sc_tok, min(TOPK_MAX, (past + ts) // 4))
    am = jnp.pad(am.reshape(bs, ts, past + 128), ((0, 0), (FIRST_TOK, 0), (0, 0)))
    q4 = q_s.astype(F32).reshape(bs, GROUP, N_HEADS, HEAD_DIM).transpose(0, 2, 1, 3)
    ob_s = sample_attention(page_table, q4, am, kf_s, vf_s, b_last, b_tail, cache_k, cache_v)
    x1_s = mix(oa_s, ob_s, ga_s, gb_s, xs, g1, wa16, wb16, wo16)
    hist_ffn = _group_rows(state_ffn_conv[0], FIRST_TOK - (FFN_CONV - 1))
    y_s, u_s = ffn(x1_s, sh2, sc2, g2, hist_ffn, n2, wup16, ffn_conv_w[0], ffn_b, wdn16, fw)

    def tok(a):
        return a.reshape(bs, GROUP, -1)[:, FIRST_TOK:, :]

    hd = (N_HEADS, HEAD_DIM)
    return (y_p[None], tok(y_s),
            kf_p.reshape((1, 1, tp) + hd), vf_p.reshape((1, 1, tp) + hd), ki_p.reshape(1, 1, tp, IDX_DIM),
            tok(kf_s).reshape((1, bs, ts) + hd), tok(vf_s).reshape((1, bs, ts) + hd), tok(ki_s)[None],
            s_p[None], s_s[None],
            conv_p[tp - (GDN_CONV - 1):][None, None],
            conv_s.reshape(bs, GROUP, -1)[:, GROUP - (GDN_CONV - 1):][None],
            utail_p[8 - (FFN_CONV - 1):][None, None],
            u_s.reshape(bs, GROUP, -1)[:, GROUP - (FFN_CONV - 1):][None])
```

```python
import functools
import math

import jax
import jax.numpy as jnp
import numpy as np
from jax import lax
from jax.experimental import pallas as pl
from jax.experimental.pallas import tpu as pltpu

F32 = jnp.float32
BF16 = jnp.bfloat16
I32 = jnp.int32
HI = lax.Precision.HIGHEST

D_MODEL = 1024
N_HEADS = 8
HEAD_DIM = 128
IDX_HEADS = 8
IDX_DIM = 64
TOPK_MAX = 256
D_FF = 2816
GDN_CONV = 4
FFN_CONV = 3
GDN_CHUNK = 64
PAGE = 128
REL_BUCKETS = 32
REL_MAX_DIST = 128
EPS = 1e-6
IDX_W_SCALE = (IDX_HEADS * IDX_DIM) ** -0.5
ATT_SCALE = HEAD_DIM ** -0.5

C_CONV, C_GZ, C_Q, C_K, C_V, C_GA, C_GB, C_QI, C_SM, C_SM2, C_END = (
    0, 3072, 4096, 5120, 6144, 7168, 8192, 9216, 9728, 9856, 9984)
L_KI, L_B, L_A, L_WI = 0, 64, 72, 80

GROUP = 8
FIRST_TOK = 4
NEG_INF = float("-inf")
KEY_NEG_INF = -2139095041
INT_MIN = -2147483648
VMEM_LIMIT = 56 * 1024 * 1024


def _cp(sem, vmem=VMEM_LIMIT):
    return pltpu.CompilerParams(dimension_semantics=sem, vmem_limit_bytes=vmem)


def _const_spec(shape):
    nd = len(shape)
    return pl.BlockSpec(shape, lambda *a: (0,) * nd, pipeline_mode=pl.Buffered(1))


def _sigmoid(x):
    return 1.0 / (1.0 + jnp.exp(-x))


def _silu(x):
    return x * _sigmoid(x)


def _softplus(x):
    return jnp.maximum(x, 0.0) + jnp.log(1.0 + jnp.exp(-jnp.abs(x)))


def _nt(a, b, precision=None):
    return lax.dot_general(a, b, (((1,), (1,)), ((), ())), precision=precision, preferred_element_type=F32)


def _tn(a, b, precision=None):
    return lax.dot_general(a, b, (((0,), (0,)), ((), ())), precision=precision, preferred_element_type=F32)


def _mm(a, b, precision=None):
    return jnp.dot(a, b, precision=precision, preferred_element_type=F32)


def _ada_kernel(c_ref, w_ref, b_ref, o_ref):
    c = c_ref[...]
    o_ref[...] = _mm(_silu(c), w_ref[...], HI) + b_ref[...]


def ada_mod(c_all, w_ada, b_ada):
    r, d = c_all.shape
    n = w_ada.shape[1]
    tn = 1024
    return pl.pallas_call(
        _ada_kernel,
        grid=(n // tn,),
        in_specs=[pl.BlockSpec((r, d), lambda j: (0, 0)),
                  pl.BlockSpec((d, tn), lambda j: (0, j)),
                  pl.BlockSpec((1, tn), lambda j: (0, j))],
        out_specs=pl.BlockSpec((r, tn), lambda j: (0, j)),
        out_shape=jax.ShapeDtypeStruct((r, n), F32),
        compiler_params=_cp(("parallel",)),
        name="ada_mod",
    )(c_all, w_ada, b_ada.reshape(1, n))


def _rel_bucket(d):
    n = jnp.maximum(d, 0)
    max_exact = REL_BUCKETS // 2
    nf = jnp.maximum(n, max_exact).astype(F32)
    large = max_exact + (jnp.log(nf / max_exact) / math.log(REL_MAX_DIST / max_exact)
                         * (REL_BUCKETS - max_exact)).astype(I32)
    large = jnp.minimum(large, REL_BUCKETS - 1)
    return jnp.where(n < max_exact, n, large)


def _bias_kernel(rb_ref, e_ref, bl_ref, bt_ref):
    def table(d, h):
        bk = _rel_bucket(d)
        acc = jnp.zeros(d.shape, F32)
        for b in range(REL_BUCKETS):
            acc = acc + jnp.where(bk == b, rb_ref[b, h], 0.0)
        return acc - rb_ref[REL_BUCKETS - 1, h]

    a = lax.broadcasted_iota(I32, (128, 128), 0)
    b = lax.broadcasted_iota(I32, (128, 128), 1)
    r = lax.broadcasted_iota(I32, (GROUP, 128), 0)
    c = lax.broadcasted_iota(I32, (GROUP, 128), 1)
    for h in range(N_HEADS):
        e_ref[0, h] = table(a - b, h)
        e_ref[1, h] = table(a - b + 128, h)
        bl_ref[h * GROUP:(h + 1) * GROUP, :] = table(128 + r - FIRST_TOK - c, h)
        bt_ref[h * GROUP:(h + 1) * GROUP, :] = table(r - c, h)


def bias_tables(rel_bias):
    return pl.pallas_call(
        _bias_kernel,
        in_specs=[pl.BlockSpec(memory_space=pltpu.SMEM)],
        out_shape=(jax.ShapeDtypeStruct((2, N_HEADS, 128, 128), F32),
                   jax.ShapeDtypeStruct((N_HEADS * GROUP, 128), F32),
                   jax.ShapeDtypeStruct((N_HEADS * GROUP, 128), F32)),
        name="bias_tables",
    )(rel_bias)


def _in_kernel(x_ref, sh_ref, sc_ref, nw_ref, w_ref, wt_ref, lnw_ref, lnb_ref,
               conv_ref, gz_ref, q_ref, kf_ref, vf_ref, k16_ref, v16_ref, ga_ref, gb_ref, qi_ref,
               small_ref, ki_ref, kb0_ref, kb1_ref, smt_ref):
    x = x_ref[...]
    h = x * lax.rsqrt(jnp.mean(x * x, -1, keepdims=True) + EPS) * nw_ref[...]
    h = h * (1.0 + sc_ref[...]) + sh_ref[...]
    hb = h.astype(BF16)

    def proj(a, b):
        return _mm(hb, w_ref[:, a:b])

    for c in range(0, 3072, 1024):
        conv_ref[:, c:c + 1024] = proj(C_CONV + c, C_CONV + c + 1024)
    gz_ref[...] = proj(C_GZ, C_Q)
    q_ref[...] = (proj(C_Q, C_K) * ATT_SCALE).astype(BF16)
    k = proj(C_K, C_V)
    kf_ref[...] = k
    k16_ref[...] = k.astype(BF16)
    v = proj(C_V, C_GA)
    vf_ref[...] = v
    v16_ref[...] = v.astype(BF16)
    ga_ref[...] = _sigmoid(proj(C_GA, C_GB))
    gb_ref[...] = _sigmoid(proj(C_GB, C_QI))
    qi_ref[...] = proj(C_QI, C_SM).astype(BF16)
    z = proj(C_SM, C_SM2)
    small_ref[...] = z
    smt_ref[...] = _nt(wt_ref[...], hb)
    lane = lax.broadcasted_iota(I32, z.shape, 1)
    for zz, lo, out_ref in ((z, True, kb0_ref), (proj(C_SM2, C_END), False, kb1_ref)):
        m = (lane < IDX_DIM) if lo else (lane >= IDX_DIM)
        zk = jnp.where(m, zz, 0.0)
        mean = jnp.sum(zk, -1, keepdims=True) * (1.0 / IDX_DIM)
        xc = jnp.where(m, zz - mean, 0.0)
        var = jnp.sum(xc * xc, -1, keepdims=True) * (1.0 / IDX_DIM)
        row = 0 if lo else 1
        kn = xc * lax.rsqrt(var + EPS) * lnw_ref[row:row + 1, :] + lnb_ref[row:row + 1, :]
        out_ref[...] = kn.astype(BF16)
        if lo:
            ki_ref[...] = kn[:, :IDX_DIM]


def in_proj(x, sh, sc, norm_w, w_cat, w_smt, lnw2, lnb2, tm=256):
    t = x.shape[0]
    tm = min(tm, t)
    per_row = sh.shape[0] != 1
    mod_spec = (pl.BlockSpec((tm, D_MODEL), lambda i: (i, 0)) if per_row
                else pl.BlockSpec((1, D_MODEL), lambda i: (0, 0)))

    def rows(n, dt):
        return pl.BlockSpec((tm, n), lambda i: (i, 0)), jax.ShapeDtypeStruct((t, n), dt)

    outs = [rows(3072, F32), rows(1024, F32), rows(1024, BF16), rows(1024, F32), rows(1024, F32),
            rows(1024, BF16), rows(1024, BF16), rows(1024, F32), rows(1024, F32), rows(512, BF16),
            rows(128, F32), rows(IDX_DIM, F32), rows(128, BF16), rows(128, BF16),
            (pl.BlockSpec((128, tm), lambda i: (0, i)), jax.ShapeDtypeStruct((128, t), F32))]
    return pl.pallas_call(
        _in_kernel,
        grid=(t // tm,),
        in_specs=[pl.BlockSpec((tm, D_MODEL), lambda i: (i, 0)), mod_spec, mod_spec,
                  _const_spec((1, D_MODEL)), _const_spec((D_MODEL, C_END)), _const_spec((128, D_MODEL)),
                  _const_spec((2, 128)), _const_spec((2, 128))],
        out_specs=[o[0] for o in outs],
        out_shape=[o[1] for o in outs],
        compiler_params=_cp(("parallel",)),
        name="in_proj",
    )(x, sh, sc, norm_w, w_cat, w_smt, lnw2, lnb2)


def _b16(a):
    return a.astype(BF16)


def _mmb(a, b):
    return jnp.dot(_b16(a), _b16(b), preferred_element_type=F32)


def _split(a):
    hi = _b16(a)
    return hi, _b16(a - hi.astype(F32))


def _mm3(a, b):
    ah, al = a
    bh, bl = b
    return (jnp.dot(ah, bh, preferred_element_type=F32) + jnp.dot(ah, bl, preferred_element_type=F32)
            + jnp.dot(al, bh, preferred_element_type=F32))


def _gdn_kernel(*refs, R, C, N, has_hist):
    s_scr = refs[-N_HEADS:]
    refs = refs[:-N_HEADS]
    if has_hist:
        conv_ref, hist_ref, small_ref, gz_ref, s0_ref, cw_ref, misc_ref, nw_ref, o_ref, sout_ref, xbuf = refs
    else:
        (conv_ref, small_ref, smt_ref, at_ref, dtt_ref, gz_ref, s0_ref, cw_ref, misc_ref, nw_ref,
         o_ref, sout_ref, xbuf) = refs
    nch = max(R // C, 1)
    RC = nch * C
    shift = C.bit_length() - 1
    n = pl.program_id(1)

    @pl.when(n == 0)
    def _():
        xbuf[...] = jnp.zeros_like(xbuf)
        for h in range(N_HEADS):
            s_scr[h][...] = s0_ref[0, h]

    x = conv_ref[...]
    validf = None
    if has_hist:
        rows_r = lax.broadcasted_iota(I32, (R, 1), 0)
        x = jnp.where((rows_r % GROUP) < FIRST_TOK, hist_ref[...], x)
        rows_c = lax.broadcasted_iota(I32, (RC, 1), 0)
        validf = jnp.where((rows_c < R) & ((rows_c % GROUP) >= FIRST_TOK), 1.0, 0.0)
    xbuf[8:8 + R, :] = x

    small = small_ref[...]
    if R < RC:
        small = jnp.concatenate([small, jnp.zeros((RC - R, 128), F32)], axis=0)
    misc = misc_ref[...]
    gfull = -jnp.exp(misc[0:1]) * _softplus(small + misc[1:2])
    betaf = _sigmoid(small)
    if validf is not None:
        gfull = gfull * validf
        betaf = betaf * validf
    ri = lax.broadcasted_iota(I32, (RC, RC), 0)
    ci = lax.broadcasted_iota(I32, (RC, RC), 1)
    same = (ri >> shift) == (ci >> shift)
    gcol = _mm(jnp.where(same & (ri >= ci), 1.0, 0.0), gfull, HI)
    if not has_hist:
        gt = -jnp.exp(at_ref[...]) * _softplus(smt_ref[...] + dtt_ref[...])
        grow_all = _mm(gt[L_A:L_A + N_HEADS, :], jnp.where(same & (ri <= ci), 1.0, 0.0), HI)
    rc = lax.broadcasted_iota(I32, (C, C), 0)
    cc = lax.broadcasted_iota(I32, (C, C), 1)
    tril = rc >= cc
    strict = rc > cc
    eye = rc == cc
    ones_cc = jnp.ones((C, C), F32)
    w = cw_ref[...]

    pre = []
    zrows = jnp.zeros((128 - C, 128), F32)
    for c in range(nch):
        r0 = c * C
        y = (w[3:4] * xbuf[8 + r0:8 + r0 + C, :] + w[2:3] * xbuf[7 + r0:7 + r0 + C, :]
             + w[1:2] * xbuf[6 + r0:6 + r0 + C, :] + w[0:1] * xbuf[5 + r0:5 + r0 + C, :])
        y = _silu(y)
        vf = None if validf is None else validf[r0:r0 + C]
        for h in range(N_HEADS):
            q = y[:, h * 128:(h + 1) * 128]
            k = y[:, 1024 + h * 128:1024 + (h + 1) * 128]
            v = y[:, 2048 + h * 128:2048 + (h + 1) * 128]
            q = q * lax.rsqrt(jnp.sum(q * q, -1, keepdims=True) + EPS) * (HEAD_DIM ** -0.5)
            k = k * lax.rsqrt(jnp.sum(k * k, -1, keepdims=True) + EPS)
            if vf is not None:
                k = k * vf
                v = v * vf
            bc = jnp.broadcast_to(betaf[r0:r0 + C, L_B + h:L_B + h + 1], (C, 128))
            gc = jnp.broadcast_to(gcol[r0:r0 + C, L_A + h:L_A + h + 1], (C, 128))
            gb = gc[:, :C]
            if has_hist:
                grow = _mm(ones_cc, jnp.where(eye, gb, 0.0), HI)
            else:
                grow = jnp.broadcast_to(grow_all[h:h + 1, r0:r0 + C], (C, C))
            dec = jnp.where(tril, jnp.exp(jnp.minimum(gb - grow, 0.0)), 0.0)
            eg = jnp.exp(gc)
            glast = gc[C - 1:C, :]
            kd = k * jnp.exp(glast - gc)
            kdt = jnp.concatenate([kd, zrows], axis=0).T[:, :C]
            pre.append((q, k, v, bc, dec, eg, jnp.exp(glast), _b16(kdt)))

    nprob = nch * N_HEADS
    a_l = []
    for q, k, v, bc, dec, eg, ge, kdt in pre:
        kh, kl = _split(k)
        a_l.append(jnp.where(strict, bc[:, :C] * (_nt(kh, kh) + _nt(kh, kl) + _nt(kl, kh)) * dec, 0.0))
    xo_l = [-a for a in a_l]
    pw_l = []
    for a in a_l:
        a_s = _split(a)
        pw_l.append(_mm3(a_s, a_s))
    span = 2
    while span < C:
        span *= 2
        pws_l = [_split(pw) for pw in pw_l]
        xo_l = [xo + pw + _mm3(_split(xo), pws) for xo, pw, pws in zip(xo_l, pw_l, pws_l)]
        if span < C:
            pw_l = [_mm3(pws, pws) for pws in pws_l]
    um_l, wm_l, qk_l = [], [], []
    for (q, k, v, bc, dec, eg, ge, kdt), xo in zip(pre, xo_l):
        rv = v * bc
        rk = k * (bc * eg)
        xo_s = _split(xo)
        um_l.append(rv + _mm3(xo_s, _split(rv)))
        wm_l.append(rk + _mm3(xo_s, _split(rk)))
        qk_l.append(_nt(_b16(q), _b16(k)) * dec)

    outs = [None] * nprob
    for c in range(nch):
        probs = range(c * N_HEADS, (c + 1) * N_HEADS)
        s_l = [s_scr[h][...] for h in range(N_HEADS)]
        sb_l = [_b16(s) for s in s_l]
        u_l = [um_l[p] - _mm(_b16(wm_l[p]), sb) for p, sb in zip(probs, sb_l)]
        for h, p in enumerate(probs):
            q, k, v, bc, dec, eg, ge, kdt = pre[p]
            outs[p] = _mm(_b16(q * eg), sb_l[h]) + _mmb(qk_l[p], u_l[h])
            s_scr[h][...] = s_l[h] * ge + _mm(kdt, _b16(u_l[h]))

    ro = min(R, C)
    for c in range(nch):
        r0 = c * C
        for h in range(N_HEADS):
            o = outs[c * N_HEADS + h][:ro]
            on = o * lax.rsqrt(jnp.mean(o * o, -1, keepdims=True) + EPS) * nw_ref[...]
            gz = gz_ref[r0:r0 + ro, h * 128:(h + 1) * 128]
            o_ref[r0:r0 + ro, h * 128:(h + 1) * 128] = (on * _silu(gz)).astype(o_ref.dtype)

    if N > 1:
        xbuf[0:8, :] = xbuf[RC:RC + 8, :]

    @pl.when(n == N - 1)
    def _():
        for h in range(N_HEADS):
            sout_ref[0, h] = s_scr[h][...]


def gdn(conv_in, hist, small, smt, gz, s0, conv_w, misc, norm_w, *, nseq, rows, out_dtype):
    C = GDN_CHUNK
    t = conv_in.shape[0]
    nstep = t // (nseq * rows)
    has_hist = hist is not None
    rc = max(rows // C, 1) * C

    def rspec(n):
        return pl.BlockSpec((rows, n), lambda b, c: (b * nstep + c, 0))

    def cspec(shape):
        return pl.BlockSpec(shape, lambda b, c: (0,) * len(shape))

    if has_hist:
        in_specs = [rspec(3072), rspec(3072), rspec(128), rspec(1024)]
        args = [conv_in, hist, small, gz]
    else:
        lane_rep = jnp.ones((1, rows), F32)
        at = misc[0].reshape(128, 1) * lane_rep
        dtt = misc[1].reshape(128, 1) * lane_rep
        in_specs = [rspec(3072), rspec(128), pl.BlockSpec((128, rows), lambda b, c: (0, b * nstep + c)),
                    cspec((128, rows)), cspec((128, rows)), rspec(1024)]
        args = [conv_in, small, smt, at, dtt, gz]
    in_specs += [pl.BlockSpec((1, N_HEADS, 128, 128), lambda b, c: (b, 0, 0, 0)),
                 cspec((GDN_CONV, 3072)), cspec((8, 128)), cspec((1, 128))]
    args += [s0, conv_w, misc, norm_w]
    return pl.pallas_call(
        functools.partial(_gdn_kernel, R=rows, C=C, N=nstep, has_hist=has_hist),
        grid=(nseq, nstep),
        in_specs=in_specs,
        out_specs=[rspec(1024), pl.BlockSpec((1, N_HEADS, 128, 128), lambda b, c: (b, 0, 0, 0))],
        out_shape=[jax.ShapeDtypeStruct((t, 1024), out_dtype),
                   jax.ShapeDtypeStruct((nseq, N_HEADS, 128, 128), F32)],
        scratch_shapes=[pltpu.VMEM((8 + rc, 3072), F32)] + [pltpu.VMEM((128, 128), F32)] * N_HEADS,
        compiler_params=_cp(("arbitrary", "arbitrary")),
        name="gdn",
    )(*args)


def _sort_key(s):
    bits = pltpu.bitcast(s + 0.0, I32)
    return bits ^ ((bits >> 31) & 0x7FFFFFFF)


def _count_ge(keys_ref, nkt, cand, tq, tk):
    def body(j, acc):
        kt = keys_ref[:, pl.ds(pl.multiple_of(j * tk, tk), tk)]
        c = jnp.where(kt >= cand, 1, 0)
        for g in range(tk // 128):
            acc = acc + c[:, g * 128:(g + 1) * 128]
        return acc

    acc = lax.fori_loop(0, nkt, body, jnp.zeros((tq, 128), I32))
    return jnp.sum(acc, axis=-1, keepdims=True)


def _kth_largest(keys_ref, nkt, rmax, tq, tk, k):
    hi = _sort_key(rmax) + 1
    lo0 = _sort_key(jnp.where(rmax >= 0.0, rmax * 0.125, rmax * 8.0))
    c0 = _count_ge(keys_ref, nkt, lo0, tq, tk)
    ok0 = c0 >= k
    lo = jnp.where(ok0, lo0, INT_MIN)
    cnt = jnp.where(ok0, c0, nkt * tk)
    hi = jnp.where(ok0, hi, lo0)
    c_zero = _count_ge(keys_ref, nkt, jnp.zeros((tq, 1), I32), tq, tk)
    c_pos = _count_ge(keys_ref, nkt, jnp.ones((tq, 1), I32), tq, tk)
    up = (c_pos >= k) & (lo < 1)
    lo = jnp.where(up, 1, lo)
    cnt = jnp.where(up, c_pos, cnt)
    hi = jnp.where((c_zero < k) & (hi > 0), 0, hi)
    at_zero = (c_zero >= k) & (c_pos < k)
    lo = jnp.where(at_zero, 0, lo)
    cnt = jnp.where(at_zero, c_zero, cnt)
    hi = jnp.where(at_zero, 1, hi)

    def mid_of(lo, hi):
        return (lo >> 1) + (hi >> 1) + (lo & hi & 1)

    def n_active(lo, hi, cnt):
        return jnp.max(jnp.where((cnt != k) & (mid_of(lo, hi) != lo), 1, 0))

    def cond(st):
        return st[3] > 0

    def body(st):
        lo, hi, cnt, _ = st
        mid = mid_of(lo, hi)
        active = (cnt != k) & (mid != lo)
        c = _count_ge(keys_ref, nkt, mid, tq, tk)
        up = active & (c >= k)
        down = active & (c < k)
        lo = jnp.where(up, mid, lo)
        cnt = jnp.where(up, c, cnt)
        hi = jnp.where(down, mid, hi)
        return lo, hi, cnt, n_active(lo, hi, cnt)

    lo, _, cnt, _ = lax.while_loop(cond, body, (lo, hi, cnt, n_active(lo, hi, cnt)))
    return lo, cnt


def _emit_selection(keys_ref, nkt, thr, cnt, tq, tk, k, emit):
    tied = (cnt > k) & (thr > KEY_NEG_INF)
    any_tied = jnp.max(jnp.where(tied, 1, 0))

    @pl.when(any_tied == 0)
    def _():
        def body(j, carry):
            kt = keys_ref[:, pl.ds(pl.multiple_of(j * tk, tk), tk)]
            emit(j, kt >= thr)
            return carry
        lax.fori_loop(0, nkt, body, 0)

    @pl.when(any_tied != 0)
    def _():
        c_gt = _count_ge(keys_ref, nkt, thr + 1, tq, tk)
        budget = jnp.where(tied, k - c_gt, nkt * tk).astype(F32)
        ri = lax.broadcasted_iota(I32, (tk, tk), 0)
        ci = lax.broadcasted_iota(I32, (tk, tk), 1)
        upper = jnp.where(ri <= ci, 1.0, 0.0).astype(BF16)

        def body(j, seen):
            kt = keys_ref[:, pl.ds(pl.multiple_of(j * tk, tk), tk)]
            eq = jnp.where(kt == thr, 1.0, 0.0)
            rank = _mm(eq.astype(BF16), upper) + seen
            sel = jnp.where(kt > thr, 1.0, jnp.where(rank <= budget, eq, 0.0))
            emit(j, sel > 0.5)
            return seen + jnp.sum(eq, axis=-1, keepdims=True)
        lax.fori_loop(0, nkt, body, jnp.zeros((tq, 1), F32))


def _idx_kernel(qi_ref, small_ref, kb0_ref, kb1_ref, mask_ref, keys_ref, *, TQ, TK, T, k):
    i = pl.program_id(0)
    nkt = (i * TQ + TQ + TK - 1) // TK
    w = small_ref[:, L_WI:L_WI + IDX_HEADS] * IDX_W_SCALE
    pos_q = i * TQ + lax.broadcasted_iota(I32, (TQ, 1), 0)

    def score_tile(j, rmax):
        off = pl.multiple_of(j * TK, TK)
        ka = kb0_ref[pl.ds(off, TK), :]
        kb = kb1_ref[pl.ds(off, TK), :]
        acc = jnp.zeros((TQ, TK), F32)
        for p in range(IDX_HEADS // 2):
            q2 = qi_ref[:, p * 128:(p + 1) * 128]
            acc = acc + w[:, 2 * p:2 * p + 1] * jnp.maximum(_nt(q2, ka), 0.0)
            acc = acc + w[:, 2 * p + 1:2 * p + 2] * jnp.maximum(_nt(q2, kb), 0.0)
        pos_k = off + lax.broadcasted_iota(I32, (1, TK), 1)
        acc = jnp.where(pos_k <= pos_q, acc, NEG_INF)
        keys_ref[:, pl.ds(off, TK)] = _sort_key(acc)
        for g in range(TK // 128):
            rmax = jnp.maximum(rmax, acc[:, g * 128:(g + 1) * 128])
        return rmax

    rmax = lax.fori_loop(0, nkt, score_tile, jnp.full((TQ, 128), NEG_INF, F32))
    thr, cnt = _kth_largest(keys_ref, nkt, jnp.max(rmax, axis=-1, keepdims=True), TQ, TK, k)

    def emit(j, sel):
        off = pl.multiple_of(j * TK, TK)
        pos_k = off + lax.broadcasted_iota(I32, (1, TK), 1)
        m = jnp.where(pos_k <= pos_q, jnp.where(sel, 1, 0), 0)
        mask_ref[:, pl.ds(off, TK)] = m.astype(jnp.int8)

    _emit_selection(keys_ref, nkt, thr, cnt, TQ, TK, k, emit)

    def zero_tile(j, carry):
        mask_ref[:, pl.ds(pl.multiple_of(j * TK, TK), TK)] = jnp.zeros((TQ, TK), jnp.int8)
        return carry

    lax.fori_loop(nkt, T // TK, zero_tile, 0)


def prompt_topk_mask(qi16, small, kb0, kb1, k, TQ=128, TK=512):
    t = qi16.shape[0]
    TK = min(TK, t)
    return pl.pallas_call(
        functools.partial(_idx_kernel, TQ=TQ, TK=TK, T=t, k=k),
        grid=(t // TQ,),
        in_specs=[pl.BlockSpec((TQ, 512), lambda i: (i, 0)),
                  pl.BlockSpec((TQ, 128), lambda i: (i, 0)),
                  _const_spec((t, 128)), _const_spec((t, 128))],
        out_specs=pl.BlockSpec((TQ, t), lambda i: (i, 0)),
        out_shape=jax.ShapeDtypeStruct((t, t), jnp.int8),
        scratch_shapes=[pltpu.VMEM((TQ, t), I32)],
        compiler_params=_cp(("parallel",)),
        name="prompt_topk_mask",
    )(qi16, small, kb0, kb1)


def _att_kernel(qa_ref, ka_ref, q_ref, k_ref, v_ref, mask_ref, e_ref, o_ref, *scr, TA, RS):
    m_scr, l_scr, acc_scr = scr[0:8], scr[8:16], scr[16:24]
    s_scr, p_scr, b_scr = scr[24:26], scr[26:28], scr[28]
    p = pl.program_id(0)
    i = qa_ref[p]
    j = ka_ref[p]

    @pl.when(j == 0)
    def _():
        for h in range(N_HEADS):
            m_scr[h][...] = jnp.full((TA, 1), -1e30, F32)
            l_scr[h][...] = jnp.zeros((TA, HEAD_DIM), F32)
            acc_scr[h][...] = jnp.zeros((TA, HEAD_DIM), F32)

    for r0 in range(0, TA, 32):
        b_scr[r0:r0 + 32, :] = jnp.where(mask_ref[r0:r0 + 32, :].astype(I32) != 0, 0.0, NEG_INF)

    def update(near):
        nb = TA // 128
        if near:
            fd = jnp.where(j == i, 1.0, 0.0)
        for h in range(N_HEADS):
            hs = slice(h * 128, (h + 1) * 128)
            sb = s_scr[h % 2]
            pb = p_scr[h % 2]
            sb[...] = _nt(q_ref[:, hs], k_ref[:, hs])
            if near:
                for b in range(nb):
                    bs = slice(b * 128, (b + 1) * 128)
                    sb[bs, bs] += fd * e_ref[0, h]
                    if b + 1 < nb:
                        sb[(b + 1) * 128:(b + 2) * 128, bs] += fd * e_ref[1, h]
                sb[0:128, TA - 128:TA] += (1.0 - fd) * e_ref[1, h]
            m_all = m_scr[h][...]
            m_parts = []
            for r0 in range(0, TA, RS):
                rs = slice(r0, r0 + RS)
                sm = sb[rs, :] + b_scr[rs, :]
                m_parts.append(jnp.maximum(m_all[rs], jnp.max(sm, axis=-1, keepdims=True)))
            for n, r0 in enumerate(range(0, TA, RS)):
                rs = slice(r0, r0 + RS)
                sm = sb[rs, :] + b_scr[rs, :]
                pb[rs, :] = jnp.exp(sm - m_parts[n]).astype(BF16)
            m_new = jnp.concatenate(m_parts, axis=0)
            alpha = jnp.exp(m_all - m_new)
            m_scr[h][...] = m_new
            pv = _mm(pb[...], jnp.concatenate([v_ref[:, hs], jnp.ones((TA, HEAD_DIM), BF16)], axis=1))
            acc_scr[h][...] = alpha * acc_scr[h][...] + pv[:, :HEAD_DIM]
            l_scr[h][...] = alpha * l_scr[h][...] + pv[:, HEAD_DIM:]

    @pl.when(j >= i - 1)
    def _():
        update(True)

    @pl.when(j < i - 1)
    def _():
        update(False)

    @pl.when(j == i)
    def _():
        for h in range(N_HEADS):
            o_ref[:, h * 128:(h + 1) * 128] = (acc_scr[h][...] / l_scr[h][...]).astype(o_ref.dtype)


def prompt_attention(q16, k16, v16, mask, e_tab, TA=512, RS=32):
    t = q16.shape[0]
    TA = min(TA, t)
    nb = t // TA
    qa = np.concatenate([np.full(i + 1, i, np.int32) for i in range(nb)])
    ka = np.concatenate([np.arange(i + 1, dtype=np.int32) for i in range(nb)])
    grid_spec = pltpu.PrefetchScalarGridSpec(
        num_scalar_prefetch=2,
        grid=(len(qa),),
        in_specs=[pl.BlockSpec((TA, 1024), lambda p, qa, ka: (qa[p], 0)),
                  pl.BlockSpec((TA, 1024), lambda p, qa, ka: (ka[p], 0)),
                  pl.BlockSpec((TA, 1024), lambda p, qa, ka: (ka[p], 0)),
                  pl.BlockSpec((TA, TA), lambda p, qa, ka: (qa[p], ka[p])),
                  pl.BlockSpec((2, N_HEADS, 128, 128), lambda p, qa, ka: (0, 0, 0, 0))],
        out_specs=pl.BlockSpec((TA, 1024), lambda p, qa, ka: (qa[p], 0)),
        scratch_shapes=([pltpu.VMEM((TA, 1), F32)] * N_HEADS + [pltpu.VMEM((TA, HEAD_DIM), F32)] * (2 * N_HEADS)
                        + [pltpu.VMEM((TA, TA), F32)] * 2 + [pltpu.VMEM((TA, TA), BF16)] * 2
                        + [pltpu.VMEM((TA, TA), F32)]))
    return pl.pallas_call(
        functools.partial(_att_kernel, TA=TA, RS=RS),
        grid_spec=grid_spec,
        out_shape=jax.ShapeDtypeStruct((t, 1024), BF16),
        compiler_params=_cp(("arbitrary",)),
        name="prompt_attention",
    )(jnp.asarray(qa), jnp.asarray(ka), q16, k16, v16, mask, e_tab)


def _page_copies(pt_ref, cache_ref, buf_ref, sem_ref, b, first_page, npages, slot):
    out = []
    for g in range(npages):
        pid = pt_ref[b, first_page + g]
        out.append(pltpu.make_async_copy(cache_ref.at[0, pid], buf_ref.at[slot, pl.ds(g * PAGE, PAGE)],
                                         sem_ref.at[slot]))
    return out


def _sidx_kernel(pt_ref, qm_ref, w_ref, knew_ref, cache_ref, sc_ref, kbuf, sem, *, NP, NB):
    b = pl.program_id(0)
    slot = b % 2

    def copies(bq, sl):
        return [pltpu.make_async_copy(cache_ref.at[0, pt_ref[bq, g]], kbuf.at[sl, :, pl.ds(g * PAGE, PAGE)],
                                      sem.at[sl]) for g in range(NP)]

    @pl.when(b == 0)
    def _():
        for c in copies(0, 0):
            c.start()

    @pl.when(b + 1 < NB)
    def _():
        for c in copies(b + 1, 1 - slot):
            c.start()

    for c in copies(b, slot):
        c.wait()

    qm = qm_ref[0]
    w = w_ref[0]

    def fold(s):
        s = jnp.maximum(s, 0.0) * w
        out = s[0:GROUP]
        for h in range(1, IDX_HEADS):
            out = out + s[h * GROUP:(h + 1) * GROUP]
        return out

    past = NP * PAGE
    CH = 2048
    for c0 in range(0, past, CH):
        kc = kbuf[slot, :, c0:c0 + CH].astype(BF16)
        sc_ref[0, :, c0:c0 + CH] = fold(_mm(qm, kc))
    knew = jnp.concatenate([knew_ref[...], jnp.zeros((128 - GROUP, IDX_DIM), F32)], axis=0).astype(BF16)
    tail = fold(_nt(qm, knew))
    r = lax.broadcasted_iota(I32, (GROUP, 128), 0)
    c = lax.broadcasted_iota(I32, (GROUP, 128), 1)
    sc_ref[0, :, past:past + 128] = jnp.where((c >= FIRST_TOK) & (c <= r), tail, NEG_INF)


def sample_scores(page_table, qm, wcol, ki_s, cache_idx):
    nb, npg = page_table.shape
    past = npg * PAGE
    grid_spec = pltpu.PrefetchScalarGridSpec(
        num_scalar_prefetch=1,
        grid=(nb,),
        in_specs=[pl.BlockSpec((1, IDX_HEADS * GROUP, IDX_DIM), lambda b, pt: (b, 0, 0)),
                  pl.BlockSpec((1, IDX_HEADS * GROUP, 1), lambda b, pt: (b, 0, 0)),
                  pl.BlockSpec((GROUP, IDX_DIM), lambda b, pt: (b, 0)),
                  pl.BlockSpec(memory_space=pl.ANY)],
        out_specs=pl.BlockSpec((1, GROUP, past + 128), lambda b, pt: (b, 0, 0)),
        scratch_shapes=[pltpu.VMEM((2, IDX_DIM, past), F32), pltpu.SemaphoreType.DMA((2,))])
    return pl.pallas_call(
        functools.partial(_sidx_kernel, NP=npg, NB=nb),
        grid_spec=grid_spec,
        out_shape=jax.ShapeDtypeStruct((nb, GROUP, past + 128), F32),
        compiler_params=_cp(("arbitrary",)),
        name="sample_scores",
    )(page_table, qm, wcol, ki_s, jnp.swapaxes(cache_idx, 2, 3))


def _ssel_kernel(sc_ref, am_ref, keys_ref, *, TQ, TK, W, k):
    nkt = W // TK

    def to_keys(j, rmax):
        off = pl.multiple_of(j * TK, TK)
        s = sc_ref[:, pl.ds(off, TK)]
        keys_ref[:, pl.ds(off, TK)] = _sort_key(s)
        for g in range(TK // 128):
            rmax = jnp.maximum(rmax, s[:, g * 128:(g + 1) * 128])
        return rmax

    rmax = lax.fori_loop(0, nkt, to_keys, jnp.full((TQ, 128), NEG_INF, F32))
    thr, cnt = _kth_largest(keys_ref, nkt, jnp.max(rmax, axis=-1, keepdims=True), TQ, TK, k)

    def emit(j, sel):
        off = pl.multiple_of(j * TK, TK)
        valid = sc_ref[:, pl.ds(off, TK)] > NEG_INF
        am_ref[:, pl.ds(off, TK)] = jnp.where(valid, jnp.where(sel, 1.0, 0.0), 0.0)

    _emit_selection(keys_ref, nkt, thr, cnt, TQ, TK, k, emit)


def sample_topk_mask(scores, k, TQ=32, TK=128):
    r, w = scores.shape
    TQ = min(TQ, r)
    return pl.pallas_call(
        functools.partial(_ssel_kernel, TQ=TQ, TK=TK, W=w, k=k),
        grid=(r // TQ,),
        in_specs=[pl.BlockSpec((TQ, w), lambda i: (i, 0))],
        out_specs=pl.BlockSpec((TQ, w), lambda i: (i, 0)),
        out_shape=jax.ShapeDtypeStruct((r, w), F32),
        scratch_shapes=[pltpu.VMEM((TQ, w), I32)],
        compiler_params=_cp(("parallel",)),
        name="sample_topk_mask",
    )(scores)


def _satt_kernel(pt_ref, q_ref, sel_ref, knew_ref, vnew_ref, tm_ref, hp_ref, blx_ref, ck_ref, cv_ref, o_ref,
                 kbuf, vbuf, ksem, vsem, m_scr, l_scr, acc_scr, *, PC, NCH, NB):
    b = pl.program_id(0)
    c = pl.program_id(1)
    g = b * NCH + c
    slot = g % 2
    CW = PC * PAGE

    def copies(bq, cq, sl):
        return (_page_copies(pt_ref, ck_ref, kbuf, ksem, bq, cq * PC, PC, sl)
                + _page_copies(pt_ref, cv_ref, vbuf, vsem, bq, cq * PC, PC, sl))

    @pl.when(g == 0)
    def _():
        for cp in copies(0, 0, 0):
            cp.start()

    @pl.when(g + 1 < NB * NCH)
    def _():
        nxt = g + 1
        for cp in copies(nxt // NCH, nxt % NCH, 1 - slot):
            cp.start()

    for cp in copies(b, c, slot):
        cp.wait()

    @pl.when(c == 0)
    def _():
        m_scr[...] = jnp.full(m_scr.shape, -1e30, F32)
        l_scr[...] = jnp.zeros(l_scr.shape, F32)
        acc_scr[...] = jnp.zeros(acc_scr.shape, F32)

    def online(s, v16):
        m_old = m_scr[...]
        m_new = jnp.maximum(m_old, jnp.max(s, axis=-1, keepdims=True))
        alpha = jnp.exp(m_old - m_new)
        pr = jnp.exp(s - m_new)
        l_scr[...] = alpha * l_scr[...] + jnp.sum(pr, axis=-1, keepdims=True)
        acc_scr[...] = alpha * acc_scr[...] + _mm(_b16(pr), v16)
        m_scr[...] = m_new

    NX = CW * N_HEADS
    q64 = q_ref[0]
    last = jnp.where(c == NCH - 1, 1.0, 0.0)
    ri = lax.broadcasted_iota(I32, (PAGE, PAGE * N_HEADS), 0)
    ci = lax.broadcasted_iota(I32, (PAGE, PAGE * N_HEADS), 1)
    spread = jnp.where((ci >> 3) == ri, 1.0, 0.0).astype(BF16)
    sx = _mm(_b16(sel_ref[0].reshape(PC * GROUP, PAGE)), spread)
    selx = jnp.concatenate([sx[p * GROUP:(p + 1) * GROUP] for p in range(PC)], axis=1)
    addm = jnp.where(jnp.concatenate([selx] * N_HEADS, axis=0) > 0.5, hp_ref[...], NEG_INF)
    s = _nt(q64, _b16(kbuf[slot].reshape(NX, HEAD_DIM))) + addm
    s = s + last * jnp.concatenate([jnp.zeros((N_HEADS * GROUP, NX - PAGE * N_HEADS), F32), blx_ref[...]], axis=1)
    online(s, _b16(vbuf[slot].reshape(NX, HEAD_DIM)))

    @pl.when(c == NCH - 1)
    def _():
        zpad = jnp.zeros((128 - N_HEADS * GROUP, HEAD_DIM), F32)
        kn = _b16(jnp.concatenate([knew_ref[0], zpad], axis=0))
        vn = _b16(jnp.concatenate([vnew_ref[0], zpad], axis=0))
        online(_nt(q64, kn) + tm_ref[0], vn)
        out = acc_scr[...] / l_scr[...]
        for h in range(N_HEADS):
            o_ref[:, h * 128:(h + 1) * 128] = out[h * GROUP:(h + 1) * GROUP, :]


def sample_attention(page_table, q64, selp, kn64, vn64, tmask, hp, blx, cache_k, cache_v, PC=8):
    nb, npg = page_table.shape
    PC = min(PC, npg)
    nch = npg // PC
    R = N_HEADS * GROUP
    grid_spec = pltpu.PrefetchScalarGridSpec(
        num_scalar_prefetch=1,
        grid=(nb, nch),
        in_specs=[pl.BlockSpec((1, R, HEAD_DIM), lambda b, c, pt: (b, 0, 0)),
                  pl.BlockSpec((1, PC, GROUP, PAGE), lambda b, c, pt: (b, c, 0, 0)),
                  pl.BlockSpec((1, R, HEAD_DIM), lambda b, c, pt: (b, 0, 0)),
                  pl.BlockSpec((1, R, HEAD_DIM), lambda b, c, pt: (b, 0, 0)),
                  pl.BlockSpec((1, R, 128), lambda b, c, pt: (b, 0, 0)),
                  pl.BlockSpec((R, PC * PAGE * N_HEADS), lambda b, c, pt: (0, 0), pipeline_mode=pl.Buffered(1)),
                  pl.BlockSpec((R, PAGE * N_HEADS), lambda b, c, pt: (0, 0), pipeline_mode=pl.Buffered(1)),
                  pl.BlockSpec(memory_space=pl.ANY),
                  pl.BlockSpec(memory_space=pl.ANY)],
        out_specs=pl.BlockSpec((GROUP, 1024), lambda b, c, pt: (b, 0)),
        scratch_shapes=[pltpu.VMEM((2, PC * PAGE, N_HEADS, HEAD_DIM), F32),
                        pltpu.VMEM((2, PC * PAGE, N_HEADS, HEAD_DIM), F32),
                        pltpu.SemaphoreType.DMA((2,)), pltpu.SemaphoreType.DMA((2,)),
                        pltpu.VMEM((R, 1), F32), pltpu.VMEM((R, 1), F32), pltpu.VMEM((R, HEAD_DIM), F32)])
    return pl.pallas_call(
        functools.partial(_satt_kernel, PC=PC, NCH=nch, NB=nb),
        grid_spec=grid_spec,
        out_shape=jax.ShapeDtypeStruct((nb * GROUP, 1024), F32),
        compiler_params=_cp(("arbitrary", "arbitrary")),
        name="sample_attention",
    )(page_table, q64, selp, kn64, vn64, tmask, hp, blx, cache_k, cache_v)


def _mix_kernel(oa_ref, ob_ref, ga_ref, gb_ref, x_ref, g1_ref, wa_ref, wb_ref, wo_ref, o_ref):
    ya = _mm(oa_ref[...].astype(BF16), wa_ref[...])
    yb = _mm(ob_ref[...].astype(BF16), wb_ref[...])
    mixed = ga_ref[...] * ya + gb_ref[...] * yb
    o_ref[...] = x_ref[...] + g1_ref[...] * _mm(mixed.astype(BF16), wo_ref[...])


def mix(oa, ob, ga, gb, x, g1, wa, wb, wo, tm=512):
    t = x.shape[0]
    tm = min(tm, t)
    per_row = g1.shape[0] != 1
    mod_spec = (pl.BlockSpec((tm, D_MODEL), lambda i: (i, 0)) if per_row
                else pl.BlockSpec((1, D_MODEL), lambda i: (0, 0)))
    row = pl.BlockSpec((tm, D_MODEL), lambda i: (i, 0))
    wspec = _const_spec((D_MODEL, D_MODEL))
    return pl.pallas_call(
        _mix_kernel,
        grid=(t // tm,),
        in_specs=[row, row, row, row, row, mod_spec, wspec, wspec, wspec],
        out_specs=row,
        out_shape=jax.ShapeDtypeStruct((t, D_MODEL), F32),
        compiler_params=_cp(("parallel",)),
        name="mix",
    )(oa, ob, ga, gb, x, g1, wa, wb, wo)


def _ffn_kernel(*refs, TM, has_hist):
    if has_hist:
        (x_ref, sh_ref, sc_ref, g2_ref, hist_ref, n2_ref, wup_ref, cw_ref, cb_ref, wdn_ref, fw_ref,
         y_ref, u_ref, ubuf) = refs
    else:
        (x_ref, sh_ref, sc_ref, g2_ref, n2_ref, wup_ref, cw_ref, cb_ref, wdn_ref, fw_ref,
         y_ref, u_ref, ubuf) = refs
    i = pl.program_id(0)

    @pl.when(i == 0)
    def _():
        ubuf[0:8, :] = jnp.zeros((8, 2 * D_FF), F32)

    x = x_ref[...]
    h = x * lax.rsqrt(jnp.mean(x * x, -1, keepdims=True) + EPS) * n2_ref[...]
    h = h * (1.0 + sc_ref[...]) + sh_ref[...]
    hb = h.astype(BF16)
    if has_hist:
        is_hist = (lax.broadcasted_iota(I32, (TM, 1), 0) % GROUP) < FIRST_TOK
    CB = 256
    acc = jnp.zeros((TM, D_MODEL), F32)
    for c in range(0, D_FF, CB):
        halves = []
        for base in (c, D_FF + c):
            cs = slice(base, base + CB)
            u = _mm(hb, wup_ref[:, cs])
            if has_hist:
                u = jnp.where(is_hist, hist_ref[:, cs], u)
            ubuf[8:8 + TM, cs] = u
            halves.append(cw_ref[2:3, cs] * u + cw_ref[1:2, cs] * ubuf[7:7 + TM, cs]
                          + cw_ref[0:1, cs] * ubuf[6:6 + TM, cs] + cb_ref[:, cs])
        act = _silu(halves[0]) * halves[1]
        acc = acc + _mm(act.astype(BF16), wdn_ref[c:c + CB, :])
    if has_hist:
        u_ref[...] = ubuf[8:8 + TM, :]
    else:
        u_ref[...] = ubuf[TM:TM + 8, :]
    ubuf[0:8, :] = ubuf[TM:TM + 8, :]
    x2 = x + g2_ref[...] * acc
    y_ref[...] = x2 * lax.rsqrt(jnp.mean(x2 * x2, -1, keepdims=True) + EPS) * fw_ref[...]


def ffn(x1, sh, sc, g2, hist, norm2_w, w_up, conv_w, conv_b, w_down, final_w, tm=256):
    t = x1.shape[0]
    tm = min(tm, t)
    has_hist = hist is not None
    per_row = sh.shape[0] != 1
    mod_spec = (pl.BlockSpec((tm, D_MODEL), lambda i: (i, 0)) if per_row
                else pl.BlockSpec((1, D_MODEL), lambda i: (0, 0)))
    row = pl.BlockSpec((tm, D_MODEL), lambda i: (i, 0))
    in_specs = [row, mod_spec, mod_spec, mod_spec]
    args = [x1, sh, sc, g2]
    if has_hist:
        in_specs.append(pl.BlockSpec((tm, 2 * D_FF), lambda i: (i, 0)))
        args.append(hist)
    in_specs += [_const_spec((1, D_MODEL)), _const_spec((D_MODEL, 2 * D_FF)), _const_spec((FFN_CONV, 2 * D_FF)),
                 _const_spec((1, 2 * D_FF)), _const_spec((D_FF, D_MODEL)), _const_spec((1, D_MODEL))]
    args += [norm2_w, w_up, conv_w, conv_b, w_down, final_w]
    if has_hist:
        u_spec = pl.BlockSpec((tm, 2 * D_FF), lambda i: (i, 0))
        u_shape = jax.ShapeDtypeStruct((t, 2 * D_FF), F32)
    else:
        u_spec = pl.BlockSpec((8, 2 * D_FF), lambda i: (0, 0))
        u_shape = jax.ShapeDtypeStruct((8, 2 * D_FF), F32)
    return pl.pallas_call(
        functools.partial(_ffn_kernel, TM=tm, has_hist=has_hist),
        grid=(t // tm,),
        in_specs=in_specs,
        out_specs=[row, u_spec],
        out_shape=[jax.ShapeDtypeStruct((t, D_MODEL), F32), u_shape],
        scratch_shapes=[pltpu.VMEM((8 + tm, 2 * D_FF), F32)],
        compiler_params=_cp(("arbitrary",)),
        name="ffn",
    )(*args)


def _group_rows(a, first):
    b, n, c = a.shape
    return jnp.pad(a, ((0, 0), (first, GROUP - first - n), (0, 0))).reshape(b * GROUP, c)


def kernel(x_prompt, x_sample, cache_k, cache_v, cache_idx_k, state_gdn, state_gdn_conv, state_ffn_conv,
           page_table, c_prompt, c_sample, w_ada, b_ada, norm1_w, w_in, gdn_conv_w, gdn_A_log, gdn_dt_bias,
           gdn_norm_w, idx_knorm_w, idx_knorm_b, w_branch_a, w_branch_b, w_out, norm2_w, w_up, ffn_conv_w,
           ffn_conv_b, w_down, rel_bias, final_norm_w):
    bp, tp, _ = x_prompt.shape
    bs, ts, _ = x_sample.shape
    assert bp == 1 and ts == GROUP - FIRST_TOK and w_ada.shape[0] == 1
    npg = page_table.shape[1]
    past = npg * PAGE

    w = w_in[0]
    w_cat = jnp.concatenate(
        [w[:, 0:4096], w[:, 4112:7184], w[:, 7768:9816], w[:, 7184:7696],
         w[:, 7696:7760], w[:, 4096:4112], w[:, 7760:7768], jnp.zeros((D_MODEL, 40), F32),
         jnp.zeros((D_MODEL, 64), F32), w[:, 7696:7760]], axis=1).astype(BF16)
    w_smt = w_cat[:, C_SM:C_SM2].T
    z64 = jnp.zeros((IDX_DIM,), F32)
    lnw2 = jnp.stack([jnp.concatenate([idx_knorm_w[0], z64]), jnp.concatenate([z64, idx_knorm_w[0]])])
    lnb2 = jnp.stack([jnp.concatenate([idx_knorm_b[0], z64]), jnp.concatenate([z64, idx_knorm_b[0]])])
    misc = jnp.zeros((8, 128), F32)
    misc = misc.at[0, L_A:L_A + N_HEADS].set(gdn_A_log[0]).at[1, L_A:L_A + N_HEADS].set(gdn_dt_bias[0])
    wa16, wb16, wo16 = w_branch_a[0].astype(BF16), w_branch_b[0].astype(BF16), w_out[0].astype(BF16)
    wup16, wdn16 = w_up[0].astype(BF16), w_down[0].astype(BF16)
    n1 = norm1_w[0].reshape(1, D_MODEL)
    n2 = norm2_w[0].reshape(1, D_MODEL)
    fw = final_norm_w.reshape(1, D_MODEL)
    gnw = gdn_norm_w[0].reshape(1, 128)
    ffn_b = ffn_conv_b[0].reshape(1, 2 * D_FF)

    c_all = jnp.concatenate([c_prompt, c_sample], axis=0)
    pad_r = (-c_all.shape[0]) % 8
    mod = ada_mod(jnp.pad(c_all, ((0, pad_r), (0, 0))), w_ada[0], b_ada[0])
    mod_p = mod[0:1]
    mod_s = jnp.repeat(mod[1:1 + bs], GROUP, axis=0)

    def mods(m):
        return [m[:, i * D_MODEL:(i + 1) * D_MODEL] for i in range(6)]

    e_tab, b_last, b_tail = bias_tables(rel_bias)

    xp = x_prompt[0]
    sh1, sc1, g1, sh2, sc2, g2 = mods(mod_p)
    (conv_p, gz_p, q_p, kf_p, vf_p, k16_p, v16_p, ga_p, gb_p, qi_p, small_p, ki_p, kb0_p, kb1_p, smt_p) = in_proj(
        xp, sh1, sc1, n1, w_cat, w_smt, lnw2, lnb2)
    oa_p, s_p = gdn(conv_p, None, small_p, smt_p, gz_p, jnp.zeros((1, N_HEADS, 128, 128), F32), gdn_conv_w[0],
                    misc, gnw, nseq=1, rows=min(4 * GDN_CHUNK, tp), out_dtype=BF16)
    mask_p = prompt_topk_mask(qi_p, small_p, kb0_p, kb1_p, min(TOPK_MAX, tp // 4))
    ob_p = prompt_attention(q_p, k16_p, v16_p, mask_p, e_tab)
    x1_p = mix(oa_p, ob_p, ga_p, gb_p, xp, g1, wa16, wb16, wo16)
    y_p, utail_p = ffn(x1_p, sh2, sc2, g2, None, n2, wup16, ffn_conv_w[0], ffn_b, wdn16, fw)

    xs = _group_rows(x_sample, FIRST_TOK)
    sh1, sc1, g1, sh2, sc2, g2 = mods(mod_s)
    (conv_s, gz_s, q_s, kf_s, vf_s, _, _, ga_s, gb_s, qi_s, small_s, ki_s, _, _, _) = in_proj(
        xs, sh1, sc1, n1, w_cat, w_smt, lnw2, lnb2)
    hist_gdn = _group_rows(state_gdn_conv[0], FIRST_TOK - (GDN_CONV - 1))
    oa_s, s_s = gdn(conv_s, hist_gdn, small_s, None, gz_s, state_gdn[0], gdn_conv_w[0], misc, gnw,
                    nseq=bs, rows=GROUP, out_dtype=F32)
    qm = qi_s.reshape(bs, GROUP, IDX_HEADS, IDX_DIM).transpose(0, 2, 1, 3).reshape(bs, IDX_HEADS * GROUP, IDX_DIM)
    wcol = (small_s[:, L_WI:L_WI + IDX_HEADS] * IDX_W_SCALE).reshape(bs, GROUP, IDX_HEADS)
    wcol = wcol.transpose(0, 2, 1).reshape(bs, IDX_HEADS * GROUP, 1)
    scores = sample_scores(page_table, qm, wcol, ki_s, cache_idx_k)
    sc_tok = scores[:, FIRST_TOK:, :].reshape(bs * ts, past + 128)
    sel = sample_topk_mask(sc_tok, min(TOPK_MAX, (past + ts) // 4))
    sel = jnp.pad(sel.reshape(bs, ts, past + 128), ((0, 0), (FIRST_TOK, 0), (0, 0)), constant_values=1.0)
    selp = sel[:, :, :past].reshape(bs, GROUP, npg, PAGE).transpose(0, 2, 1, 3)
    heads = jnp.arange(N_HEADS)
    ok = (sel[:, None, :, past:past + GROUP, None] > 0.5) & (heads[:, None] == heads[None, :])[None, :, None, None, :]
    bt = b_tail.reshape(N_HEADS, GROUP, 128)[None, :, :, :GROUP, None]
    tmask = jnp.where(ok, bt, NEG_INF).reshape(bs, N_HEADS * GROUP, GROUP * N_HEADS)
    tmask = jnp.pad(tmask, ((0, 0), (0, 0), (0, 128 - GROUP * N_HEADS)), constant_values=NEG_INF)
    nx = min(8, npg) * PAGE * N_HEADS
    hp = jnp.asarray(np.where((np.arange(nx)[None, :] & 7) == (np.arange(N_HEADS * GROUP)[:, None] >> 3),
                              0.0, NEG_INF).astype(np.float32))
    blx = jnp.repeat(b_last, N_HEADS, axis=1)
    q64 = q_s.reshape(bs, GROUP, N_HEADS, HEAD_DIM).transpose(0, 2, 1, 3).reshape(bs, N_HEADS * GROUP, HEAD_DIM)
    ob_s = sample_attention(page_table, q64, selp, kf_s.reshape(bs, GROUP * N_HEADS, HEAD_DIM),
                            vf_s.reshape(bs, GROUP * N_HEADS, HEAD_DIM), tmask, hp, blx, cache_k, cache_v)
    x1_s = mix(oa_s, ob_s, ga_s, gb_s, xs, g1, wa16, wb16, wo16)
    hist_ffn = _group_rows(state_ffn_conv[0], FIRST_TOK - (FFN_CONV - 1))
    y_s, u_s = ffn(x1_s, sh2, sc2, g2, hist_ffn, n2, wup16, ffn_conv_w[0], ffn_b, wdn16, fw)

    def tok(a):
        return a.reshape(bs, GROUP, -1)[:, FIRST_TOK:, :]

    hd = (N_HEADS, HEAD_DIM)
    return (y_p[None], tok(y_s),
            kf_p.reshape((1, 1, tp) + hd), vf_p.reshape((1, 1, tp) + hd), ki_p.reshape(1, 1, tp, IDX_DIM),
            tok(kf_s).reshape((1, bs, ts) + hd), tok(vf_s).reshape((1, bs, ts) + hd), tok(ki_s)[None],
            s_p[None], s_s[None],
            conv_p[tp - (GDN_CONV - 1):][None, None],
            conv_s.reshape(bs, GROUP, -1)[:, GROUP - (GDN_CONV - 1):][None],
            utail_p[8 - (FFN_CONV - 1):][None, None],
            u_s.reshape(bs, GROUP, -1)[:, GROUP - (FFN_CONV - 1):][None])
```

```python
import functools
import math

import jax
import jax.numpy as jnp
import numpy as np
from jax import lax
from jax.experimental import pallas as pl
from jax.experimental.pallas import tpu as pltpu

F32 = jnp.float32
BF16 = jnp.bfloat16
I32 = jnp.int32
HI = lax.Precision.HIGHEST

D_MODEL = 1024
N_HEADS = 8
HEAD_DIM = 128
IDX_HEADS = 8
IDX_DIM = 64
TOPK_MAX = 256
D_FF = 2816
GDN_CONV = 4
FFN_CONV = 3
GDN_CHUNK = 64
PAGE = 128
REL_BUCKETS = 32
REL_MAX_DIST = 128
EPS = 1e-6
IDX_W_SCALE = (IDX_HEADS * IDX_DIM) ** -0.5
ATT_SCALE = HEAD_DIM ** -0.5

C_CONV, C_GZ, C_Q, C_K, C_V, C_GA, C_GB, C_QI, C_SM, C_SM2, C_END = (
    0, 3072, 4096, 5120, 6144, 7168, 8192, 9216, 9728, 9856, 9984)
L_KI, L_B, L_A, L_WI = 0, 64, 72, 80

GROUP = 8
FIRST_TOK = 4
NEG_INF = float("-inf")
KEY_NEG_INF = -2139095041
INT_MIN = -2147483648
VMEM_LIMIT = 56 * 1024 * 1024


def _cp(sem, vmem=VMEM_LIMIT):
    return pltpu.CompilerParams(dimension_semantics=sem, vmem_limit_bytes=vmem)


def _const_spec(shape):
    nd = len(shape)
    return pl.BlockSpec(shape, lambda *a: (0,) * nd, pipeline_mode=pl.Buffered(1))


def _sigmoid(x):
    return 1.0 / (1.0 + jnp.exp(-x))


def _silu(x):
    return x * _sigmoid(x)


def _softplus(x):
    return jnp.maximum(x, 0.0) + jnp.log(1.0 + jnp.exp(-jnp.abs(x)))


def _nt(a, b, precision=None):
    return lax.dot_general(a, b, (((1,), (1,)), ((), ())), precision=precision, preferred_element_type=F32)


def _tn(a, b, precision=None):
    return lax.dot_general(a, b, (((0,), (0,)), ((), ())), precision=precision, preferred_element_type=F32)


def _mm(a, b, precision=None):
    return jnp.dot(a, b, precision=precision, preferred_element_type=F32)


def _ada_kernel(c_ref, w_ref, b_ref, o_ref):
    c = c_ref[...]
    o_ref[...] = _mm(_silu(c), w_ref[...], HI) + b_ref[...]


def ada_mod(c_all, w_ada, b_ada):
    r, d = c_all.shape
    n = w_ada.shape[1]
    tn = 1024
    return pl.pallas_call(
        _ada_kernel,
        grid=(n // tn,),
        in_specs=[pl.BlockSpec((r, d), lambda j: (0, 0)),
                  pl.BlockSpec((d, tn), lambda j: (0, j)),
                  pl.BlockSpec((1, tn), lambda j: (0, j))],
        out_specs=pl.BlockSpec((r, tn), lambda j: (0, j)),
        out_shape=jax.ShapeDtypeStruct((r, n), F32),
        compiler_params=_cp(("parallel",)),
        name="ada_mod",
    )(c_all, w_ada, b_ada.reshape(1, n))


def _rel_bucket(d):
    n = jnp.maximum(d, 0)
    max_exact = REL_BUCKETS // 2
    nf = jnp.maximum(n, max_exact).astype(F32)
    large = max_exact + (jnp.log(nf / max_exact) / math.log(REL_MAX_DIST / max_exact)
                         * (REL_BUCKETS - max_exact)).astype(I32)
    large = jnp.minimum(large, REL_BUCKETS - 1)
    return jnp.where(n < max_exact, n, large)


def _bias_kernel(rb_ref, e_ref, bl_ref, bt_ref):
    def table(d, h):
        bk = _rel_bucket(d)
        acc = jnp.zeros(d.shape, F32)
        for b in range(REL_BUCKETS):
            acc = acc + jnp.where(bk == b, rb_ref[b, h], 0.0)
        return acc - rb_ref[REL_BUCKETS - 1, h]

    a = lax.broadcasted_iota(I32, (128, 128), 0)
    b = lax.broadcasted_iota(I32, (128, 128), 1)
    r = lax.broadcasted_iota(I32, (GROUP, 128), 0)
    c = lax.broadcasted_iota(I32, (GROUP, 128), 1)
    for h in range(N_HEADS):
        e_ref[0, h] = table(a - b, h)
        e_ref[1, h] = table(a - b + 128, h)
        bl_ref[h * GROUP:(h + 1) * GROUP, :] = table(128 + r - FIRST_TOK - c, h)
        bt_ref[h * GROUP:(h + 1) * GROUP, :] = table(r - c, h)


def bias_tables(rel_bias):
    return pl.pallas_call(
        _bias_kernel,
        in_specs=[pl.BlockSpec(memory_space=pltpu.SMEM)],
        out_shape=(jax.ShapeDtypeStruct((2, N_HEADS, 128, 128), F32),
                   jax.ShapeDtypeStruct((N_HEADS * GROUP, 128), F32),
                   jax.ShapeDtypeStruct((N_HEADS * GROUP, 128), F32)),
        name="bias_tables",
    )(rel_bias)


def _in_kernel(x_ref, sh_ref, sc_ref, nw_ref, w_ref, wt_ref, lnw_ref, lnb_ref,
               conv_ref, gz_ref, q_ref, kf_ref, vf_ref, k16_ref, v16_ref, ga_ref, gb_ref, qi_ref,
               small_ref, ki_ref, kb0_ref, kb1_ref, smt_ref):
    x = x_ref[...]
    h = x * lax.rsqrt(jnp.mean(x * x, -1, keepdims=True) + EPS) * nw_ref[...]
    h = h * (1.0 + sc_ref[...]) + sh_ref[...]
    hb = h.astype(BF16)

    def proj(a, b):
        return _mm(hb, w_ref[:, a:b])

    for c in range(0, 3072, 1024):
        conv_ref[:, c:c + 1024] = proj(C_CONV + c, C_CONV + c + 1024)
    gz_ref[...] = proj(C_GZ, C_Q)
    q_ref[...] = (proj(C_Q, C_K) * ATT_SCALE).astype(BF16)
    k = proj(C_K, C_V)
    kf_ref[...] = k
    k16_ref[...] = k.astype(BF16)
    v = proj(C_V, C_GA)
    vf_ref[...] = v
    v16_ref[...] = v.astype(BF16)
    ga_ref[...] = _sigmoid(proj(C_GA, C_GB))
    gb_ref[...] = _sigmoid(proj(C_GB, C_QI))
    qi_ref[...] = proj(C_QI, C_SM).astype(BF16)
    z = proj(C_SM, C_SM2)
    small_ref[...] = z
    smt_ref[...] = _nt(wt_ref[...], hb)
    lane = lax.broadcasted_iota(I32, z.shape, 1)
    for zz, lo, out_ref in ((z, True, kb0_ref), (proj(C_SM2, C_END), False, kb1_ref)):
        m = (lane < IDX_DIM) if lo else (lane >= IDX_DIM)
        zk = jnp.where(m, zz, 0.0)
        mean = jnp.sum(zk, -1, keepdims=True) * (1.0 / IDX_DIM)
        xc = jnp.where(m, zz - mean, 0.0)
        var = jnp.sum(xc * xc, -1, keepdims=True) * (1.0 / IDX_DIM)
        row = 0 if lo else 1
        kn = xc * lax.rsqrt(var + EPS) * lnw_ref[row:row + 1, :] + lnb_ref[row:row + 1, :]
        out_ref[...] = kn.astype(BF16)
        if lo:
            ki_ref[...] = kn[:, :IDX_DIM]


def in_proj(x, sh, sc, norm_w, w_cat, w_smt, lnw2, lnb2, tm=256):
    t = x.shape[0]
    tm = min(tm, t)
    per_row = sh.shape[0] != 1
    mod_spec = (pl.BlockSpec((tm, D_MODEL), lambda i: (i, 0)) if per_row
                else pl.BlockSpec((1, D_MODEL), lambda i: (0, 0)))

    def rows(n, dt):
        return pl.BlockSpec((tm, n), lambda i: (i, 0)), jax.ShapeDtypeStruct((t, n), dt)

    outs = [rows(3072, F32), rows(1024, F32), rows(1024, BF16), rows(1024, F32), rows(1024, F32),
            rows(1024, BF16), rows(1024, BF16), rows(1024, F32), rows(1024, F32), rows(512, BF16),
            rows(128, F32), rows(IDX_DIM, F32), rows(128, BF16), rows(128, BF16),
            (pl.BlockSpec((128, tm), lambda i: (0, i)), jax.ShapeDtypeStruct((128, t), F32))]
    return pl.pallas_call(
        _in_kernel,
        grid=(t // tm,),
        in_specs=[pl.BlockSpec((tm, D_MODEL), lambda i: (i, 0)), mod_spec, mod_spec,
                  _const_spec((1, D_MODEL)), _const_spec((D_MODEL, C_END)), _const_spec((128, D_MODEL)),
                  _const_spec((2, 128)), _const_spec((2, 128))],
        out_specs=[o[0] for o in outs],
        out_shape=[o[1] for o in outs],
        compiler_params=_cp(("parallel",)),
        name="in_proj",
    )(x, sh, sc, norm_w, w_cat, w_smt, lnw2, lnb2)


def _b16(a):
    return a.astype(BF16)


def _mmb(a, b):
    return jnp.dot(_b16(a), _b16(b), preferred_element_type=F32)


def _split(a):
    hi = _b16(a)
    return hi, _b16(a - hi.astype(F32))


def _mm3(a, b):
    ah, al = a
    bh, bl = b
    return (jnp.dot(ah, bh, preferred_element_type=F32) + jnp.dot(ah, bl, preferred_element_type=F32)
            + jnp.dot(al, bh, preferred_element_type=F32))


def _gdn_kernel(*refs, R, C, N, has_hist):
    s_scr = refs[-N_HEADS:]
    refs = refs[:-N_HEADS]
    if has_hist:
        conv_ref, hist_ref, small_ref, gz_ref, s0_ref, cw_ref, misc_ref, nw_ref, o_ref, sout_ref, xbuf = refs
    else:
        (conv_ref, small_ref, smt_ref, at_ref, dtt_ref, gz_ref, s0_ref, cw_ref, misc_ref, nw_ref,
         o_ref, sout_ref, xbuf) = refs
    nch = max(R // C, 1)
    RC = nch * C
    shift = C.bit_length() - 1
    n = pl.program_id(1)

    @pl.when(n == 0)
    def _():
        xbuf[...] = jnp.zeros_like(xbuf)
        for h in range(N_HEADS):
            s_scr[h][...] = s0_ref[0, h]

    x = conv_ref[...]
    validf = None
    if has_hist:
        rows_r = lax.broadcasted_iota(I32, (R, 1), 0)
        x = jnp.where((rows_r % GROUP) < FIRST_TOK, hist_ref[...], x)
        rows_c = lax.broadcasted_iota(I32, (RC, 1), 0)
        validf = jnp.where((rows_c < R) & ((rows_c % GROUP) >= FIRST_TOK), 1.0, 0.0)
    xbuf[8:8 + R, :] = x

    small = small_ref[...]
    if R < RC:
        small = jnp.concatenate([small, jnp.zeros((RC - R, 128), F32)], axis=0)
    misc = misc_ref[...]
    gfull = -jnp.exp(misc[0:1]) * _softplus(small + misc[1:2])
    betaf = _sigmoid(small)
    if validf is not None:
        gfull = gfull * validf
        betaf = betaf * validf
    ri = lax.broadcasted_iota(I32, (RC, RC), 0)
    ci = lax.broadcasted_iota(I32, (RC, RC), 1)
    same = (ri >> shift) == (ci >> shift)
    gcol = _mm(jnp.where(same & (ri >= ci), 1.0, 0.0), gfull, HI)
    if not has_hist:
        gt = -jnp.exp(at_ref[...]) * _softplus(smt_ref[...] + dtt_ref[...])
        grow_all = _mm(gt[L_A:L_A + N_HEADS, :], jnp.where(same & (ri <= ci), 1.0, 0.0), HI)
    rc = lax.broadcasted_iota(I32, (C, C), 0)
    cc = lax.broadcasted_iota(I32, (C, C), 1)
    tril = rc >= cc
    strict = rc > cc
    eye = rc == cc
    ones_cc = jnp.ones((C, C), F32)
    w = cw_ref[...]

    pre = []
    zrows = jnp.zeros((128 - C, 128), F32)
    for c in range(nch):
        r0 = c * C
        y = (w[3:4] * xbuf[8 + r0:8 + r0 + C, :] + w[2:3] * xbuf[7 + r0:7 + r0 + C, :]
             + w[1:2] * xbuf[6 + r0:6 + r0 + C, :] + w[0:1] * xbuf[5 + r0:5 + r0 + C, :])
        y = _silu(y)
        vf = None if validf is None else validf[r0:r0 + C]
        for h in range(N_HEADS):
            q = y[:, h * 128:(h + 1) * 128]
            k = y[:, 1024 + h * 128:1024 + (h + 1) * 128]
            v = y[:, 2048 + h * 128:2048 + (h + 1) * 128]
            q = q * lax.rsqrt(jnp.sum(q * q, -1, keepdims=True) + EPS) * (HEAD_DIM ** -0.5)
            k = k * lax.rsqrt(jnp.sum(k * k, -1, keepdims=True) + EPS)
            if vf is not None:
                k = k * vf
                v = v * vf
            bc = jnp.broadcast_to(betaf[r0:r0 + C, L_B + h:L_B + h + 1], (C, 128))
            gc = jnp.broadcast_to(gcol[r0:r0 + C, L_A + h:L_A + h + 1], (C, 128))
            gb = gc[:, :C]
            if has_hist:
                grow = _mm(ones_cc, jnp.where(eye, gb, 0.0), HI)
            else:
                grow = jnp.broadcast_to(grow_all[h:h + 1, r0:r0 + C], (C, C))
            dec = jnp.where(tril, jnp.exp(jnp.minimum(gb - grow, 0.0)), 0.0)
            eg = jnp.exp(gc)
            glast = gc[C - 1:C, :]
            kd = k * jnp.exp(glast - gc)
            kdt = jnp.concatenate([kd, zrows], axis=0).T[:, :C]
            pre.append((q, k, v, bc, dec, eg, jnp.exp(glast), _b16(kdt)))

    nprob = nch * N_HEADS
    a_l = []
    for q, k, v, bc, dec, eg, ge, kdt in pre:
        kh, kl = _split(k)
        a_l.append(jnp.where(strict, bc[:, :C] * (_nt(kh, kh) + _nt(kh, kl) + _nt(kl, kh)) * dec, 0.0))
    xo_l = [-a for a in a_l]
    pw_l = []
    for a in a_l:
        a_s = _split(a)
        pw_l.append(_mm3(a_s, a_s))
    span = 2
    while span < C:
        span *= 2
        pws_l = [_split(pw) for pw in pw_l]
        xo_l = [xo + pw + _mm3(_split(xo), pws) for xo, pw, pws in zip(xo_l, pw_l, pws_l)]
        if span < C:
            pw_l = [_mm3(pws, pws) for pws in pws_l]
    um_l, wm_l, qk_l = [], [], []
    for (q, k, v, bc, dec, eg, ge, kdt), xo in zip(pre, xo_l):
        rv = v * bc
        rk = k * (bc * eg)
        xo_s = _split(xo)
        um_l.append(rv + _mm3(xo_s, _split(rv)))
        wm_l.append(rk + _mm3(xo_s, _split(rk)))
        qk_l.append(_nt(_b16(q), _b16(k)) * dec)

    outs = [None] * nprob
    for c in range(nch):
        probs = range(c * N_HEADS, (c + 1) * N_HEADS)
        s_l = [s_scr[h][...] for h in range(N_HEADS)]
        sb_l = [_b16(s) for s in s_l]
        u_l = [um_l[p] - _mm(_b16(wm_l[p]), sb) for p, sb in zip(probs, sb_l)]
        for h, p in enumerate(probs):
            q, k, v, bc, dec, eg, ge, kdt = pre[p]
            outs[p] = _mm(_b16(q * eg), sb_l[h]) + _mmb(qk_l[p], u_l[h])
            s_scr[h][...] = s_l[h] * ge + _mm(kdt, _b16(u_l[h]))

    ro = min(R, C)
    for c in range(nch):
        r0 = c * C
        for h in range(N_HEADS):
            o = outs[c * N_HEADS + h][:ro]
            on = o * lax.rsqrt(jnp.mean(o * o, -1, keepdims=True) + EPS) * nw_ref[...]
            gz = gz_ref[r0:r0 + ro, h * 128:(h + 1) * 128]
            o_ref[r0:r0 + ro, h * 128:(h + 1) * 128] = (on * _silu(gz)).astype(o_ref.dtype)

    if N > 1:
        xbuf[0:8, :] = xbuf[RC:RC + 8, :]

    @pl.when(n == N - 1)
    def _():
        for h in range(N_HEADS):
            sout_ref[0, h] = s_scr[h][...]


def gdn(conv_in, hist, small, smt, gz, s0, conv_w, misc, norm_w, *, nseq, rows, out_dtype):
    C = GDN_CHUNK
    t = conv_in.shape[0]
    nstep = t // (nseq * rows)
    has_hist = hist is not None
    rc = max(rows // C, 1) * C

    def rspec(n):
        return pl.BlockSpec((rows, n), lambda b, c: (b * nstep + c, 0))

    def cspec(shape):
        return pl.BlockSpec(shape, lambda b, c: (0,) * len(shape))

    if has_hist:
        in_specs = [rspec(3072), rspec(3072), rspec(128), rspec(1024)]
        args = [conv_in, hist, small, gz]
    else:
        lane_rep = jnp.ones((1, rows), F32)
        at = misc[0].reshape(128, 1) * lane_rep
        dtt = misc[1].reshape(128, 1) * lane_rep
        in_specs = [rspec(3072), rspec(128), pl.BlockSpec((128, rows), lambda b, c: (0, b * nstep + c)),
                    cspec((128, rows)), cspec((128, rows)), rspec(1024)]
        args = [conv_in, small, smt, at, dtt, gz]
    in_specs += [pl.BlockSpec((1, N_HEADS, 128, 128), lambda b, c: (b, 0, 0, 0)),
                 cspec((GDN_CONV, 3072)), cspec((8, 128)), cspec((1, 128))]
    args += [s0, conv_w, misc, norm_w]
    return pl.pallas_call(
        functools.partial(_gdn_kernel, R=rows, C=C, N=nstep, has_hist=has_hist),
        grid=(nseq, nstep),
        in_specs=in_specs,
        out_specs=[rspec(1024), pl.BlockSpec((1, N_HEADS, 128, 128), lambda b, c: (b, 0, 0, 0))],
        out_shape=[jax.ShapeDtypeStruct((t, 1024), out_dtype),
                   jax.ShapeDtypeStruct((nseq, N_HEADS, 128, 128), F32)],
        scratch_shapes=[pltpu.VMEM((8 + rc, 3072), F32)] + [pltpu.VMEM((128, 128), F32)] * N_HEADS,
        compiler_params=_cp(("arbitrary", "arbitrary")),
        name="gdn",
    )(*args)


def _sort_key(s, col=None):
    bits = pltpu.bitcast(s + 0.0, I32)
    key = bits ^ ((bits >> 31) & 0x7FFFFFFF)
    if col is None:
        return key
    return jnp.where(s == 0.0, -1 - col, key)


def _count_ge(keys_ref, nkt, cand, tq, tk):
    def body(j, acc):
        kt = keys_ref[:, pl.ds(pl.multiple_of(j * tk, tk), tk)]
        c = jnp.where(kt >= cand, 1, 0)
        for g in range(tk // 128):
            acc = acc + c[:, g * 128:(g + 1) * 128]
        return acc

    acc = lax.fori_loop(0, nkt, body, jnp.zeros((tq, 128), I32))
    return jnp.sum(acc, axis=-1, keepdims=True)


def _kth_largest(keys_ref, nkt, rmax, tq, tk, k):
    hi = _sort_key(rmax) + 1
    lo0 = _sort_key(jnp.where(rmax >= 0.0, rmax * 0.125, rmax * 8.0))
    c0 = _count_ge(keys_ref, nkt, lo0, tq, tk)
    ok0 = c0 >= k
    lo = jnp.where(ok0, lo0, INT_MIN)
    cnt = jnp.where(ok0, c0, nkt * tk)
    hi = jnp.where(ok0, hi, lo0)
    c_pos = _count_ge(keys_ref, nkt, jnp.ones((tq, 1), I32), tq, tk)
    up = (c_pos >= k) & (lo < 1)
    lo = jnp.where(up, 1, lo)
    cnt = jnp.where(up, c_pos, cnt)
    hi = jnp.where((c_pos < k) & (hi > 1), 1, hi)

    def mid_of(lo, hi):
        return (lo >> 1) + (hi >> 1) + (lo & hi & 1)

    def n_active(lo, hi, cnt):
        return jnp.max(jnp.where((cnt != k) & (mid_of(lo, hi) != lo), 1, 0))

    def cond(st):
        return st[3] > 0

    def body(st):
        lo, hi, cnt, _ = st
        mid = mid_of(lo, hi)
        active = (cnt != k) & (mid != lo)
        c = _count_ge(keys_ref, nkt, mid, tq, tk)
        up = active & (c >= k)
        down = active & (c < k)
        lo = jnp.where(up, mid, lo)
        cnt = jnp.where(up, c, cnt)
        hi = jnp.where(down, mid, hi)
        return lo, hi, cnt, n_active(lo, hi, cnt)

    lo, _, cnt, _ = lax.while_loop(cond, body, (lo, hi, cnt, n_active(lo, hi, cnt)))
    return lo, cnt


def _emit_selection(keys_ref, nkt, thr, cnt, tq, tk, k, emit):
    tied = (cnt > k) & (thr > KEY_NEG_INF)
    any_tied = jnp.max(jnp.where(tied, 1, 0))

    @pl.when(any_tied == 0)
    def _():
        def body(j, carry):
            kt = keys_ref[:, pl.ds(pl.multiple_of(j * tk, tk), tk)]
            emit(j, kt >= thr)
            return carry
        lax.fori_loop(0, nkt, body, 0)

    @pl.when(any_tied != 0)
    def _():
        c_gt = _count_ge(keys_ref, nkt, thr + 1, tq, tk)
        budget = jnp.where(tied, k - c_gt, nkt * tk).astype(F32)
        ri = lax.broadcasted_iota(I32, (tk, tk), 0)
        ci = lax.broadcasted_iota(I32, (tk, tk), 1)
        upper = jnp.where(ri <= ci, 1.0, 0.0).astype(BF16)

        def body(j, seen):
            kt = keys_ref[:, pl.ds(pl.multiple_of(j * tk, tk), tk)]
            eq = jnp.where(kt == thr, 1.0, 0.0)
            rank = _mm(eq.astype(BF16), upper) + seen
            sel = jnp.where(kt > thr, 1.0, jnp.where(rank <= budget, eq, 0.0))
            emit(j, sel > 0.5)
            return seen + jnp.sum(eq, axis=-1, keepdims=True)
        lax.fori_loop(0, nkt, body, jnp.zeros((tq, 1), F32))


def _idx_kernel(qi_ref, small_ref, kb0_ref, kb1_ref, mask_ref, keys_ref, *, TQ, TK, T, k):
    i = pl.program_id(0)
    nkt = (i * TQ + TQ + TK - 1) // TK
    w = small_ref[:, L_WI:L_WI + IDX_HEADS] * IDX_W_SCALE
    pos_q = i * TQ + lax.broadcasted_iota(I32, (TQ, 1), 0)

    def score_tile(j, rmax):
        off = pl.multiple_of(j * TK, TK)
        ka = kb0_ref[pl.ds(off, TK), :]
        kb = kb1_ref[pl.ds(off, TK), :]
        acc = jnp.zeros((TQ, TK), F32)
        for p in range(IDX_HEADS // 2):
            q2 = qi_ref[:, p * 128:(p + 1) * 128]
            acc = acc + w[:, 2 * p:2 * p + 1] * jnp.maximum(_nt(q2, ka), 0.0)
            acc = acc + w[:, 2 * p + 1:2 * p + 2] * jnp.maximum(_nt(q2, kb), 0.0)
        pos_k = off + lax.broadcasted_iota(I32, (1, TK), 1)
        acc = jnp.where(pos_k <= pos_q, acc, NEG_INF)
        keys_ref[:, pl.ds(off, TK)] = _sort_key(acc, pos_k)
        for g in range(TK // 128):
            rmax = jnp.maximum(rmax, acc[:, g * 128:(g + 1) * 128])
        return rmax

    rmax = lax.fori_loop(0, nkt, score_tile, jnp.full((TQ, 128), NEG_INF, F32))
    thr, cnt = _kth_largest(keys_ref, nkt, jnp.max(rmax, axis=-1, keepdims=True), TQ, TK, k)

    def emit(j, sel):
        off = pl.multiple_of(j * TK, TK)
        pos_k = off + lax.broadcasted_iota(I32, (1, TK), 1)
        m = jnp.where(pos_k <= pos_q, jnp.where(sel, 1, 0), 0)
        mask_ref[:, pl.ds(off, TK)] = m.astype(jnp.int8)

    _emit_selection(keys_ref, nkt, thr, cnt, TQ, TK, k, emit)

    def zero_tile(j, carry):
        mask_ref[:, pl.ds(pl.multiple_of(j * TK, TK), TK)] = jnp.zeros((TQ, TK), jnp.int8)
        return carry

    lax.fori_loop(nkt, T // TK, zero_tile, 0)


def prompt_topk_mask(qi16, small, kb0, kb1, k, TQ=256, TK=512):
    t = qi16.shape[0]
    TK = min(TK, t)
    TQ = min(TQ, t)
    return pl.pallas_call(
        functools.partial(_idx_kernel, TQ=TQ, TK=TK, T=t, k=k),
        grid=(t // TQ,),
        in_specs=[pl.BlockSpec((TQ, 512), lambda i: (i, 0)),
                  pl.BlockSpec((TQ, 128), lambda i: (i, 0)),
                  _const_spec((t, 128)), _const_spec((t, 128))],
        out_specs=pl.BlockSpec((TQ, t), lambda i: (i, 0)),
        out_shape=jax.ShapeDtypeStruct((t, t), jnp.int8),
        scratch_shapes=[pltpu.VMEM((TQ, t), I32)],
        compiler_params=_cp(("parallel",)),
        name="prompt_topk_mask",
    )(qi16, small, kb0, kb1)


def _att_kernel(qa_ref, ka_ref, q_ref, k_ref, v_ref, mask_ref, e_ref, o_ref, *scr, TA, RS):
    m_scr, l_scr, acc_scr = scr[0:8], scr[8:16], scr[16:24]
    s_scr, p_scr, b_scr = scr[24:26], scr[26:28], scr[28]
    p = pl.program_id(0)
    i = qa_ref[p]
    j = ka_ref[p]

    @pl.when(j == 0)
    def _():
        for h in range(N_HEADS):
            m_scr[h][...] = jnp.full((TA, 1), -1e30, F32)
            l_scr[h][...] = jnp.zeros((TA, HEAD_DIM), F32)
            acc_scr[h][...] = jnp.zeros((TA, HEAD_DIM), F32)

    for r0 in range(0, TA, 32):
        b_scr[r0:r0 + 32, :] = jnp.where(mask_ref[r0:r0 + 32, :].astype(I32) != 0, 0.0, NEG_INF)

    def update(near):
        nb = TA // 128
        if near:
            fd = jnp.where(j == i, 1.0, 0.0)
        for h in range(N_HEADS):
            hs = slice(h * 128, (h + 1) * 128)
            sb = s_scr[h % 2]
            pb = p_scr[h % 2]
            sb[...] = _nt(q_ref[:, hs], k_ref[:, hs]) + b_scr[...]
            if near:
                for b in range(nb):
                    bs = slice(b * 128, (b + 1) * 128)
                    sb[bs, bs] += fd * e_ref[0, h]
                    if b + 1 < nb:
                        sb[(b + 1) * 128:(b + 2) * 128, bs] += fd * e_ref[1, h]
                sb[0:128, TA - 128:TA] += (1.0 - fd) * e_ref[1, h]
            m_all = m_scr[h][...]
            m_parts = []
            for r0 in range(0, TA, RS):
                rs = slice(r0, r0 + RS)
                m_parts.append(jnp.maximum(m_all[rs], jnp.max(sb[rs, :], axis=-1, keepdims=True)))
            for n, r0 in enumerate(range(0, TA, RS)):
                rs = slice(r0, r0 + RS)
                pb[rs, :] = jnp.exp(sb[rs, :] - m_parts[n]).astype(BF16)
            m_new = jnp.concatenate(m_parts, axis=0)
            alpha = jnp.exp(m_all - m_new)
            m_scr[h][...] = m_new
            pv = _mm(pb[...], jnp.concatenate([v_ref[:, hs], jnp.ones((TA, HEAD_DIM), BF16)], axis=1))
            acc_scr[h][...] = alpha * acc_scr[h][...] + pv[:, :HEAD_DIM]
            l_scr[h][...] = alpha * l_scr[h][...] + pv[:, HEAD_DIM:]

    @pl.when(j >= i - 1)
    def _():
        update(True)

    @pl.when(j < i - 1)
    def _():
        update(False)

    @pl.when(j == i)
    def _():
        for h in range(N_HEADS):
            o_ref[:, h * 128:(h + 1) * 128] = (acc_scr[h][...] / l_scr[h][...]).astype(o_ref.dtype)


def prompt_attention(q16, k16, v16, mask, e_tab, TA=512, RS=32):
    t = q16.shape[0]
    TA = min(TA, t)
    nb = t // TA
    qa = np.concatenate([np.full(i + 1, i, np.int32) for i in range(nb)])
    ka = np.concatenate([np.arange(i + 1, dtype=np.int32) for i in range(nb)])
    grid_spec = pltpu.PrefetchScalarGridSpec(
        num_scalar_prefetch=2,
        grid=(len(qa),),
        in_specs=[pl.BlockSpec((TA, 1024), lambda p, qa, ka: (qa[p], 0)),
                  pl.BlockSpec((TA, 1024), lambda p, qa, ka: (ka[p], 0)),
                  pl.BlockSpec((TA, 1024), lambda p, qa, ka: (ka[p], 0)),
                  pl.BlockSpec((TA, TA), lambda p, qa, ka: (qa[p], ka[p])),
                  pl.BlockSpec((2, N_HEADS, 128, 128), lambda p, qa, ka: (0, 0, 0, 0))],
        out_specs=pl.BlockSpec((TA, 1024), lambda p, qa, ka: (qa[p], 0)),
        scratch_shapes=([pltpu.VMEM((TA, 1), F32)] * N_HEADS + [pltpu.VMEM((TA, HEAD_DIM), F32)] * (2 * N_HEADS)
                        + [pltpu.VMEM((TA, TA), F32)] * 2 + [pltpu.VMEM((TA, TA), BF16)] * 2
                        + [pltpu.VMEM((TA, TA), F32)]))
    return pl.pallas_call(
        functools.partial(_att_kernel, TA=TA, RS=RS),
        grid_spec=grid_spec,
        out_shape=jax.ShapeDtypeStruct((t, 1024), BF16),
        compiler_params=_cp(("arbitrary",)),
        name="prompt_attention",
    )(jnp.asarray(qa), jnp.asarray(ka), q16, k16, v16, mask, e_tab)


def _page_copies(pt_ref, cache_ref, buf_ref, sem_ref, b, first_page, npages, slot):
    out = []
    for g in range(npages):
        pid = pt_ref[b, first_page + g]
        out.append(pltpu.make_async_copy(cache_ref.at[0, pid], buf_ref.at[slot, pl.ds(g * PAGE, PAGE)],
                                         sem_ref.at[slot]))
    return out


def _sidx_kernel(pt_ref, qm_ref, w_ref, knew_ref, cache_ref, sc_ref, kbuf, sem, *, NP, NB):
    b = pl.program_id(0)
    slot = b % 2

    def copies(bq, sl):
        return [pltpu.make_async_copy(cache_ref.at[0, pt_ref[bq, g]], kbuf.at[sl, :, pl.ds(g * PAGE, PAGE)],
                                      sem.at[sl]) for g in range(NP)]

    @pl.when(b == 0)
    def _():
        for c in copies(0, 0):
            c.start()

    @pl.when(b + 1 < NB)
    def _():
        for c in copies(b + 1, 1 - slot):
            c.start()

    for c in copies(b, slot):
        c.wait()

    qm = qm_ref[0]
    w = w_ref[0]

    def fold(s):
        s = jnp.maximum(s, 0.0) * w
        out = s[0:GROUP]
        for h in range(1, IDX_HEADS):
            out = out + s[h * GROUP:(h + 1) * GROUP]
        return out

    past = NP * PAGE
    CH = 2048
    for c0 in range(0, past, CH):
        kc = kbuf[slot, :, c0:c0 + CH].astype(BF16)
        sc_ref[0, :, c0:c0 + CH] = fold(_mm(qm, kc))
    knew = jnp.concatenate([knew_ref[...], jnp.zeros((128 - GROUP, IDX_DIM), F32)], axis=0).astype(BF16)
    tail = fold(_nt(qm, knew))
    r = lax.broadcasted_iota(I32, (GROUP, 128), 0)
    c = lax.broadcasted_iota(I32, (GROUP, 128), 1)
    sc_ref[0, :, past:past + 128] = jnp.where((c >= FIRST_TOK) & (c <= r), tail, NEG_INF)


def sample_scores(page_table, qm, wcol, ki_s, cache_idx):
    nb, npg = page_table.shape
    past = npg * PAGE
    grid_spec = pltpu.PrefetchScalarGridSpec(
        num_scalar_prefetch=1,
        grid=(nb,),
        in_specs=[pl.BlockSpec((1, IDX_HEADS * GROUP, IDX_DIM), lambda b, pt: (b, 0, 0)),
                  pl.BlockSpec((1, IDX_HEADS * GROUP, 1), lambda b, pt: (b, 0, 0)),
                  pl.BlockSpec((GROUP, IDX_DIM), lambda b, pt: (b, 0)),
                  pl.BlockSpec(memory_space=pl.ANY)],
        out_specs=pl.BlockSpec((1, GROUP, past + 128), lambda b, pt: (b, 0, 0)),
        scratch_shapes=[pltpu.VMEM((2, IDX_DIM, past), F32), pltpu.SemaphoreType.DMA((2,))])
    return pl.pallas_call(
        functools.partial(_sidx_kernel, NP=npg, NB=nb),
        grid_spec=grid_spec,
        out_shape=jax.ShapeDtypeStruct((nb, GROUP, past + 128), F32),
        compiler_params=_cp(("arbitrary",)),
        name="sample_scores",
    )(page_table, qm, wcol, ki_s, jnp.swapaxes(cache_idx, 2, 3))


def _ssel_kernel(sc_ref, am_ref, keys_ref, *, TQ, TK, W, k):
    nkt = W // TK

    def to_keys(j, rmax):
        off = pl.multiple_of(j * TK, TK)
        s = sc_ref[:, pl.ds(off, TK)]
        keys_ref[:, pl.ds(off, TK)] = _sort_key(s, off + lax.broadcasted_iota(I32, (1, TK), 1))
        for g in range(TK // 128):
            rmax = jnp.maximum(rmax, s[:, g * 128:(g + 1) * 128])
        return rmax

    rmax = lax.fori_loop(0, nkt, to_keys, jnp.full((TQ, 128), NEG_INF, F32))
    thr, cnt = _kth_largest(keys_ref, nkt, jnp.max(rmax, axis=-1, keepdims=True), TQ, TK, k)

    def emit(j, sel):
        off = pl.multiple_of(j * TK, TK)
        valid = sc_ref[:, pl.ds(off, TK)] > NEG_INF
        am_ref[:, pl.ds(off, TK)] = jnp.where(valid, jnp.where(sel, 1.0, 0.0), 0.0)

    _emit_selection(keys_ref, nkt, thr, cnt, TQ, TK, k, emit)


def sample_topk_mask(scores, k, TQ=32, TK=128):
    r, w = scores.shape
    TQ = min(TQ, r)
    return pl.pallas_call(
        functools.partial(_ssel_kernel, TQ=TQ, TK=TK, W=w, k=k),
        grid=(r // TQ,),
        in_specs=[pl.BlockSpec((TQ, w), lambda i: (i, 0))],
        out_specs=pl.BlockSpec((TQ, w), lambda i: (i, 0)),
        out_shape=jax.ShapeDtypeStruct((r, w), F32),
        scratch_shapes=[pltpu.VMEM((TQ, w), I32)],
        compiler_params=_cp(("parallel",)),
        name="sample_topk_mask",
    )(scores)


def _satt_kernel(pt_ref, q_ref, sel_ref, knew_ref, vnew_ref, tm_ref, hp_ref, blx_ref, ck_ref, cv_ref, o_ref,
                 kbuf, vbuf, ksem, vsem, m_scr, l_scr, acc_scr, *, PC, NCH, NB):
    b = pl.program_id(0)
    c = pl.program_id(1)
    g = b * NCH + c
    slot = g % 2
    CW = PC * PAGE

    def copies(bq, cq, sl):
        return (_page_copies(pt_ref, ck_ref, kbuf, ksem, bq, cq * PC, PC, sl)
                + _page_copies(pt_ref, cv_ref, vbuf, vsem, bq, cq * PC, PC, sl))

    @pl.when(g == 0)
    def _():
        for cp in copies(0, 0, 0):
            cp.start()

    @pl.when(g + 1 < NB * NCH)
    def _():
        nxt = g + 1
        for cp in copies(nxt // NCH, nxt % NCH, 1 - slot):
            cp.start()

    for cp in copies(b, c, slot):
        cp.wait()

    @pl.when(c == 0)
    def _():
        m_scr[...] = jnp.full(m_scr.shape, -1e30, F32)
        l_scr[...] = jnp.zeros(l_scr.shape, F32)
        acc_scr[...] = jnp.zeros(acc_scr.shape, F32)

    def online(s, v16):
        m_old = m_scr[...]
        m_new = jnp.maximum(m_old, jnp.max(s, axis=-1, keepdims=True))
        alpha = jnp.exp(m_old - m_new)
        pr = jnp.exp(s - m_new)
        l_scr[...] = alpha * l_scr[...] + jnp.sum(pr, axis=-1, keepdims=True)
        acc_scr[...] = alpha * acc_scr[...] + _mm(_b16(pr), v16)
        m_scr[...] = m_new

    NX = CW * N_HEADS
    q64 = q_ref[0]
    last = jnp.where(c == NCH - 1, 1.0, 0.0)
    ri = lax.broadcasted_iota(I32, (PAGE, PAGE * N_HEADS), 0)
    ci = lax.broadcasted_iota(I32, (PAGE, PAGE * N_HEADS), 1)
    spread = jnp.where((ci >> 3) == ri, 1.0, 0.0).astype(BF16)
    sx = _mm(_b16(sel_ref[0].reshape(PC * GROUP, PAGE)), spread)
    selx = jnp.concatenate([sx[p * GROUP:(p + 1) * GROUP] for p in range(PC)], axis=1)
    addm = jnp.where(jnp.concatenate([selx] * N_HEADS, axis=0) > 0.5, hp_ref[...], NEG_INF)
    s = _nt(q64, _b16(kbuf[slot].reshape(NX, HEAD_DIM))) + addm
    s = s + last * jnp.concatenate([jnp.zeros((N_HEADS * GROUP, NX - PAGE * N_HEADS), F32), blx_ref[...]], axis=1)
    online(s, _b16(vbuf[slot].reshape(NX, HEAD_DIM)))

    @pl.when(c == NCH - 1)
    def _():
        zpad = jnp.zeros((128 - N_HEADS * GROUP, HEAD_DIM), F32)
        kn = _b16(jnp.concatenate([knew_ref[0], zpad], axis=0))
        vn = _b16(jnp.concatenate([vnew_ref[0], zpad], axis=0))
        online(_nt(q64, kn) + tm_ref[0], vn)
        out = acc_scr[...] / l_scr[...]
        for h in range(N_HEADS):
            o_ref[:, h * 128:(h + 1) * 128] = out[h * GROUP:(h + 1) * GROUP, :]


def sample_attention(page_table, q64, selp, kn64, vn64, tmask, hp, blx, cache_k, cache_v, PC=8):
    nb, npg = page_table.shape
    PC = min(PC, npg)
    nch = npg // PC
    R = N_HEADS * GROUP
    grid_spec = pltpu.PrefetchScalarGridSpec(
        num_scalar_prefetch=1,
        grid=(nb, nch),
        in_specs=[pl.BlockSpec((1, R, HEAD_DIM), lambda b, c, pt: (b, 0, 0)),
                  pl.BlockSpec((1, PC, GROUP, PAGE), lambda b, c, pt: (b, c, 0, 0)),
                  pl.BlockSpec((1, R, HEAD_DIM), lambda b, c, pt: (b, 0, 0)),
                  pl.BlockSpec((1, R, HEAD_DIM), lambda b, c, pt: (b, 0, 0)),
                  pl.BlockSpec((1, R, 128), lambda b, c, pt: (b, 0, 0)),
                  pl.BlockSpec((R, PC * PAGE * N_HEADS), lambda b, c, pt: (0, 0), pipeline_mode=pl.Buffered(1)),
                  pl.BlockSpec((R, PAGE * N_HEADS), lambda b, c, pt: (0, 0), pipeline_mode=pl.Buffered(1)),
                  pl.BlockSpec(memory_space=pl.ANY),
                  pl.BlockSpec(memory_space=pl.ANY)],
        out_specs=pl.BlockSpec((GROUP, 1024), lambda b, c, pt: (b, 0)),
        scratch_shapes=[pltpu.VMEM((2, PC * PAGE, N_HEADS, HEAD_DIM), F32),
                        pltpu.VMEM((2, PC * PAGE, N_HEADS, HEAD_DIM), F32),
                        pltpu.SemaphoreType.DMA((2,)), pltpu.SemaphoreType.DMA((2,)),
                        pltpu.VMEM((R, 1), F32), pltpu.VMEM((R, 1), F32), pltpu.VMEM((R, HEAD_DIM), F32)])
    return pl.pallas_call(
        functools.partial(_satt_kernel, PC=PC, NCH=nch, NB=nb),
        grid_spec=grid_spec,
        out_shape=jax.ShapeDtypeStruct((nb * GROUP, 1024), F32),
        compiler_params=_cp(("arbitrary", "arbitrary")),
        name="sample_attention",
    )(page_table, q64, selp, kn64, vn64, tmask, hp, blx, cache_k, cache_v)


def _mix_kernel(oa_ref, ob_ref, ga_ref, gb_ref, x_ref, g1_ref, wa_ref, wb_ref, wo_ref, o_ref):
    ya = _mm(oa_ref[...].astype(BF16), wa_ref[...])
    yb = _mm(ob_ref[...].astype(BF16), wb_ref[...])
    mixed = ga_ref[...] * ya + gb_ref[...] * yb
    o_ref[...] = x_ref[...] + g1_ref[...] * _mm(mixed.astype(BF16), wo_ref[...])


def mix(oa, ob, ga, gb, x, g1, wa, wb, wo, tm=512):
    t = x.shape[0]
    tm = min(tm, t)
    per_row = g1.shape[0] != 1
    mod_spec = (pl.BlockSpec((tm, D_MODEL), lambda i: (i, 0)) if per_row
                else pl.BlockSpec((1, D_MODEL), lambda i: (0, 0)))
    row = pl.BlockSpec((tm, D_MODEL), lambda i: (i, 0))
    wspec = _const_spec((D_MODEL, D_MODEL))
    return pl.pallas_call(
        _mix_kernel,
        grid=(t // tm,),
        in_specs=[row, row, row, row, row, mod_spec, wspec, wspec, wspec],
        out_specs=row,
        out_shape=jax.ShapeDtypeStruct((t, D_MODEL), F32),
        compiler_params=_cp(("parallel",)),
        name="mix",
    )(oa, ob, ga, gb, x, g1, wa, wb, wo)


def _ffn_kernel(*refs, TM, has_hist):
    if has_hist:
        (x_ref, sh_ref, sc_ref, g2_ref, hist_ref, n2_ref, wup_ref, cw_ref, cb_ref, wdn_ref, fw_ref,
         y_ref, u_ref, ubuf) = refs
    else:
        (x_ref, sh_ref, sc_ref, g2_ref, n2_ref, wup_ref, cw_ref, cb_ref, wdn_ref, fw_ref,
         y_ref, u_ref, ubuf) = refs
    i = pl.program_id(0)

    @pl.when(i == 0)
    def _():
        ubuf[0:8, :] = jnp.zeros((8, 2 * D_FF), F32)

    x = x_ref[...]
    h = x * lax.rsqrt(jnp.mean(x * x, -1, keepdims=True) + EPS) * n2_ref[...]
    h = h * (1.0 + sc_ref[...]) + sh_ref[...]
    hb = h.astype(BF16)
    if has_hist:
        is_hist = (lax.broadcasted_iota(I32, (TM, 1), 0) % GROUP) < FIRST_TOK
    CB = 256
    acc = jnp.zeros((TM, D_MODEL), F32)
    def up(c):
        us = []
        for base in (c, D_FF + c):
            cs = slice(base, base + CB)
            u = _mm(hb, wup_ref[:, cs])
            if has_hist:
                u = jnp.where(is_hist, hist_ref[:, cs], u)
            ubuf[8:8 + TM, cs] = u
            us.append(u)
        return us

    starts = list(range(0, D_FF, CB))
    nxt = up(starts[0])
    for n, c in enumerate(starts):
        cur = nxt
        if n + 1 < len(starts):
            nxt = up(starts[n + 1])
        halves = []
        for u, base in zip(cur, (c, D_FF + c)):
            cs = slice(base, base + CB)
            halves.append(cw_ref[2:3, cs] * u + cw_ref[1:2, cs] * ubuf[7:7 + TM, cs]
                          + cw_ref[0:1, cs] * ubuf[6:6 + TM, cs] + cb_ref[:, cs])
        act = _silu(halves[0]) * halves[1]
        acc = acc + _mm(act.astype(BF16), wdn_ref[c:c + CB, :])
    if has_hist:
        u_ref[...] = ubuf[8:8 + TM, :]
    else:
        u_ref[...] = ubuf[TM:TM + 8, :]
    ubuf[0:8, :] = ubuf[TM:TM + 8, :]
    x2 = x + g2_ref[...] * acc
    y_ref[...] = x2 * lax.rsqrt(jnp.mean(x2 * x2, -1, keepdims=True) + EPS) * fw_ref[...]


def ffn(x1, sh, sc, g2, hist, norm2_w, w_up, conv_w, conv_b, w_down, final_w, tm=256):
    t = x1.shape[0]
    tm = min(tm, t)
    has_hist = hist is not None
    per_row = sh.shape[0] != 1
    mod_spec = (pl.BlockSpec((tm, D_MODEL), lambda i: (i, 0)) if per_row
                else pl.BlockSpec((1, D_MODEL), lambda i: (0, 0)))
    row = pl.BlockSpec((tm, D_MODEL), lambda i: (i, 0))
    in_specs = [row, mod_spec, mod_spec, mod_spec]
    args = [x1, sh, sc, g2]
    if has_hist:
        in_specs.append(pl.BlockSpec((tm, 2 * D_FF), lambda i: (i, 0)))
        args.append(hist)
    in_specs += [_const_spec((1, D_MODEL)), _const_spec((D_MODEL, 2 * D_FF)), _const_spec((FFN_CONV, 2 * D_FF)),
                 _const_spec((1, 2 * D_FF)), _const_spec((D_FF, D_MODEL)), _const_spec((1, D_MODEL))]
    args += [norm2_w, w_up, conv_w, conv_b, w_down, final_w]
    if has_hist:
        u_spec = pl.BlockSpec((tm, 2 * D_FF), lambda i: (i, 0))
        u_shape = jax.ShapeDtypeStruct((t, 2 * D_FF), F32)
    else:
        u_spec = pl.BlockSpec((8, 2 * D_FF), lambda i: (0, 0))
        u_shape = jax.ShapeDtypeStruct((8, 2 * D_FF), F32)
    return pl.pallas_call(
        functools.partial(_ffn_kernel, TM=tm, has_hist=has_hist),
        grid=(t // tm,),
        in_specs=in_specs,
        out_specs=[row, u_spec],
        out_shape=[jax.ShapeDtypeStruct((t, D_MODEL), F32), u_shape],
        scratch_shapes=[pltpu.VMEM((8 + tm, 2 * D_FF), F32)],
        compiler_params=_cp(("arbitrary",)),
        name="ffn",
    )(*args)


def _group_rows(a, first):
    b, n, c = a.shape
    return jnp.pad(a, ((0, 0), (first, GROUP - first - n), (0, 0))).reshape(b * GROUP, c)


def kernel(x_prompt, x_sample, cache_k, cache_v, cache_idx_k, state_gdn, state_gdn_conv, state_ffn_conv,
           page_table, c_prompt, c_sample, w_ada, b_ada, norm1_w, w_in, gdn_conv_w, gdn_A_log, gdn_dt_bias,
           gdn_norm_w, idx_knorm_w, idx_knorm_b, w_branch_a, w_branch_b, w_out, norm2_w, w_up, ffn_conv_w,
           ffn_conv_b, w_down, rel_bias, final_norm_w):
    bp, tp, _ = x_prompt.shape
    bs, ts, _ = x_sample.shape
    assert bp == 1 and ts == GROUP - FIRST_TOK and w_ada.shape[0] == 1
    npg = page_table.shape[1]
    past = npg * PAGE

    w = w_in[0]
    w_cat = jnp.concatenate(
        [w[:, 0:4096], w[:, 4112:7184], w[:, 7768:9816], w[:, 7184:7696],
         w[:, 7696:7760], w[:, 4096:4112], w[:, 7760:7768], jnp.zeros((D_MODEL, 40), F32),
         jnp.zeros((D_MODEL, 64), F32), w[:, 7696:7760]], axis=1).astype(BF16)
    w_smt = w_cat[:, C_SM:C_SM2].T
    z64 = jnp.zeros((IDX_DIM,), F32)
    lnw2 = jnp.stack([jnp.concatenate([idx_knorm_w[0], z64]), jnp.concatenate([z64, idx_knorm_w[0]])])
    lnb2 = jnp.stack([jnp.concatenate([idx_knorm_b[0], z64]), jnp.concatenate([z64, idx_knorm_b[0]])])
    misc = jnp.zeros((8, 128), F32)
    misc = misc.at[0, L_A:L_A + N_HEADS].set(gdn_A_log[0]).at[1, L_A:L_A + N_HEADS].set(gdn_dt_bias[0])
    wa16, wb16, wo16 = w_branch_a[0].astype(BF16), w_branch_b[0].astype(BF16), w_out[0].astype(BF16)
    wup16, wdn16 = w_up[0].astype(BF16), w_down[0].astype(BF16)
    n1 = norm1_w[0].reshape(1, D_MODEL)
    n2 = norm2_w[0].reshape(1, D_MODEL)
    fw = final_norm_w.reshape(1, D_MODEL)
    gnw = gdn_norm_w[0].reshape(1, 128)
    ffn_b = ffn_conv_b[0].reshape(1, 2 * D_FF)

    c_all = jnp.concatenate([c_prompt, c_sample], axis=0)
    pad_r = (-c_all.shape[0]) % 8
    mod = ada_mod(jnp.pad(c_all, ((0, pad_r), (0, 0))), w_ada[0], b_ada[0])
    mod_p = mod[0:1]
    mod_s = jnp.repeat(mod[1:1 + bs], GROUP, axis=0)

    def mods(m):
        return [m[:, i * D_MODEL:(i + 1) * D_MODEL] for i in range(6)]

    e_tab, b_last, b_tail = bias_tables(rel_bias)

    xp = x_prompt[0]
    sh1, sc1, g1, sh2, sc2, g2 = mods(mod_p)
    (conv_p, gz_p, q_p, kf_p, vf_p, k16_p, v16_p, ga_p, gb_p, qi_p, small_p, ki_p, kb0_p, kb1_p, smt_p) = in_proj(
        xp, sh1, sc1, n1, w_cat, w_smt, lnw2, lnb2)
    oa_p, s_p = gdn(conv_p, None, small_p, smt_p, gz_p, jnp.zeros((1, N_HEADS, 128, 128), F32), gdn_conv_w[0],
                    misc, gnw, nseq=1, rows=min(4 * GDN_CHUNK, tp), out_dtype=BF16)
    mask_p = prompt_topk_mask(qi_p, small_p, kb0_p, kb1_p, min(TOPK_MAX, tp // 4))
    ob_p = prompt_attention(q_p, k16_p, v16_p, mask_p, e_tab)
    x1_p = mix(oa_p, ob_p, ga_p, gb_p, xp, g1, wa16, wb16, wo16)
    y_p, utail_p = ffn(x1_p, sh2, sc2, g2, None, n2, wup16, ffn_conv_w[0], ffn_b, wdn16, fw)

    xs = _group_rows(x_sample, FIRST_TOK)
    sh1, sc1, g1, sh2, sc2, g2 = mods(mod_s)
    (conv_s, gz_s, q_s, kf_s, vf_s, _, _, ga_s, gb_s, qi_s, small_s, ki_s, _, _, _) = in_proj(
        xs, sh1, sc1, n1, w_cat, w_smt, lnw2, lnb2)
    hist_gdn = _group_rows(state_gdn_conv[0], FIRST_TOK - (GDN_CONV - 1))
    oa_s, s_s = gdn(conv_s, hist_gdn, small_s, None, gz_s, state_gdn[0], gdn_conv_w[0], misc, gnw,
                    nseq=bs, rows=GROUP, out_dtype=F32)
    qm = qi_s.reshape(bs, GROUP, IDX_HEADS, IDX_DIM).transpose(0, 2, 1, 3).reshape(bs, IDX_HEADS * GROUP, IDX_DIM)
    wcol = (small_s[:, L_WI:L_WI + IDX_HEADS] * IDX_W_SCALE).reshape(bs, GROUP, IDX_HEADS)
    wcol = wcol.transpose(0, 2, 1).reshape(bs, IDX_HEADS * GROUP, 1)
    scores = sample_scores(page_table, qm, wcol, ki_s, cache_idx_k)
    sc_tok = scores[:, FIRST_TOK:, :].reshape(bs * ts, past + 128)
    sel = sample_topk_mask(sc_tok, min(TOPK_MAX, (past + ts) // 4))
    sel = jnp.pad(sel.reshape(bs, ts, past + 128), ((0, 0), (FIRST_TOK, 0), (0, 0)), constant_values=1.0)
    selp = sel[:, :, :past].reshape(bs, GROUP, npg, PAGE).transpose(0, 2, 1, 3)
    heads = jnp.arange(N_HEADS)
    ok = (sel[:, None, :, past:past + GROUP, None] > 0.5) & (heads[:, None] == heads[None, :])[None, :, None, None, :]
    bt = b_tail.reshape(N_HEADS, GROUP, 128)[None, :, :, :GROUP, None]
    tmask = jnp.where(ok, bt, NEG_INF).reshape(bs, N_HEADS * GROUP, GROUP * N_HEADS)
    tmask = jnp.pad(tmask, ((0, 0), (0, 0), (0, 128 - GROUP * N_HEADS)), constant_values=NEG_INF)
    nx = min(8, npg) * PAGE * N_HEADS
    hp = jnp.asarray(np.where((np.arange(nx)[None, :] & 7) == (np.arange(N_HEADS * GROUP)[:, None] >> 3),
                              0.0, NEG_INF).astype(np.float32))
    blx = jnp.repeat(b_last, N_HEADS, axis=1)
    q64 = q_s.reshape(bs, GROUP, N_HEADS, HEAD_DIM).transpose(0, 2, 1, 3).reshape(bs, N_HEADS * GROUP, HEAD_DIM)
    ob_s = sample_attention(page_table, q64, selp, kf_s.reshape(bs, GROUP * N_HEADS, HEAD_DIM),
                            vf_s.reshape(bs, GROUP * N_HEADS, HEAD_DIM), tmask, hp, blx, cache_k, cache_v)
    x1_s = mix(oa_s, ob_s, ga_s, gb_s, xs, g1, wa16, wb16, wo16)
    hist_ffn = _group_rows(state_ffn_conv[0], FIRST_TOK - (FFN_CONV - 1))
    y_s, u_s = ffn(x1_s, sh2, sc2, g2, hist_ffn, n2, wup16, ffn_conv_w[0], ffn_b, wdn16, fw)

    def tok(a):
        return a.reshape(bs, GROUP, -1)[:, FIRST_TOK:, :]

    hd = (N_HEADS, HEAD_DIM)
    return (y_p[None], tok(y_s),
            kf_p.reshape((1, 1, tp) + hd), vf_p.reshape((1, 1, tp) + hd), ki_p.reshape(1, 1, tp, IDX_DIM),
            tok(kf_s).reshape((1, bs, ts) + hd), tok(vf_s).reshape((1, bs, ts) + hd), tok(ki_s)[None],
            s_p[None], s_s[None],
            conv_p[tp - (GDN_CONV - 1):][None, None],
            conv_s.reshape(bs, GROUP, -1)[:, GROUP - (GDN_CONV - 1):][None],
            utail_p[8 - (FFN_CONV - 1):][None, None],
            u_s.reshape(bs, GROUP, -1)[:, GROUP - (FFN_CONV - 1):][None])
```

```python
import functools
import math

import jax
import jax.numpy as jnp
import numpy as np
from jax import lax
from jax.experimental import pallas as pl
from jax.experimental.pallas import tpu as pltpu

F32 = jnp.float32
BF16 = jnp.bfloat16
I32 = jnp.int32
HI = lax.Precision.HIGHEST

D_MODEL = 1024
N_HEADS = 8
HEAD_DIM = 128
IDX_HEADS = 8
IDX_DIM = 64
TOPK_MAX = 256
D_FF = 2816
GDN_CONV = 4
FFN_CONV = 3
GDN_CHUNK = 64
PAGE = 128
REL_BUCKETS = 32
REL_MAX_DIST = 128
EPS = 1e-6
IDX_W_SCALE = (IDX_HEADS * IDX_DIM) ** -0.5
ATT_SCALE = HEAD_DIM ** -0.5

C_CONV, C_GZ, C_Q, C_K, C_V, C_GA, C_GB, C_QI, C_SM, C_SM2, C_END = (
    0, 3072, 4096, 5120, 6144, 7168, 8192, 9216, 9728, 9856, 9984)
L_KI, L_B, L_A, L_WI = 0, 64, 72, 80

GROUP = 8
FIRST_TOK = 4
NEG_INF = float("-inf")
KEY_NEG_INF = -2139095041
INT_MIN = -2147483648
VMEM_LIMIT = 56 * 1024 * 1024


def _cp(sem, vmem=VMEM_LIMIT):
    return pltpu.CompilerParams(dimension_semantics=sem, vmem_limit_bytes=vmem)


def _const_spec(shape):
    nd = len(shape)
    return pl.BlockSpec(shape, lambda *a: (0,) * nd, pipeline_mode=pl.Buffered(1))


def _sigmoid(x):
    return 1.0 / (1.0 + jnp.exp(-x))


def _silu(x):
    return x * _sigmoid(x)


def _softplus(x):
    return jnp.maximum(x, 0.0) + jnp.log(1.0 + jnp.exp(-jnp.abs(x)))


def _nt(a, b, precision=None):
    return lax.dot_general(a, b, (((1,), (1,)), ((), ())), precision=precision, preferred_element_type=F32)


def _tn(a, b, precision=None):
    return lax.dot_general(a, b, (((0,), (0,)), ((), ())), precision=precision, preferred_element_type=F32)


def _mm(a, b, precision=None):
    return jnp.dot(a, b, precision=precision, preferred_element_type=F32)


def _ada_kernel(c_ref, w_ref, b_ref, o_ref):
    c = c_ref[...]
    o_ref[...] = _mm(_silu(c), w_ref[...], HI) + b_ref[...]


def ada_mod(c_all, w_ada, b_ada):
    r, d = c_all.shape
    n = w_ada.shape[1]
    tn = 1024
    return pl.pallas_call(
        _ada_kernel,
        grid=(n // tn,),
        in_specs=[pl.BlockSpec((r, d), lambda j: (0, 0)),
                  pl.BlockSpec((d, tn), lambda j: (0, j)),
                  pl.BlockSpec((1, tn), lambda j: (0, j))],
        out_specs=pl.BlockSpec((r, tn), lambda j: (0, j)),
        out_shape=jax.ShapeDtypeStruct((r, n), F32),
        compiler_params=_cp(("parallel",)),
        name="ada_mod",
    )(c_all, w_ada, b_ada.reshape(1, n))


def _rel_bucket(d):
    n = jnp.maximum(d, 0)
    max_exact = REL_BUCKETS // 2
    nf = jnp.maximum(n, max_exact).astype(F32)
    large = max_exact + (jnp.log(nf / max_exact) / math.log(REL_MAX_DIST / max_exact)
                         * (REL_BUCKETS - max_exact)).astype(I32)
    large = jnp.minimum(large, REL_BUCKETS - 1)
    return jnp.where(n < max_exact, n, large)


def _bias_kernel(rb_ref, e_ref, bl_ref, bt_ref):
    def table(d, h):
        bk = _rel_bucket(d)
        acc = jnp.zeros(d.shape, F32)
        for b in range(REL_BUCKETS):
            acc = acc + jnp.where(bk == b, rb_ref[b, h], 0.0)
        return acc - rb_ref[REL_BUCKETS - 1, h]

    a = lax.broadcasted_iota(I32, (128, 128), 0)
    b = lax.broadcasted_iota(I32, (128, 128), 1)
    r = lax.broadcasted_iota(I32, (GROUP, 128), 0)
    c = lax.broadcasted_iota(I32, (GROUP, 128), 1)
    for h in range(N_HEADS):
        e_ref[0, h] = table(a - b, h)
        e_ref[1, h] = table(a - b + 128, h)
        bl_ref[h * GROUP:(h + 1) * GROUP, :] = table(128 + r - FIRST_TOK - c, h)
        bt_ref[h * GROUP:(h + 1) * GROUP, :] = table(r - c, h)


def bias_tables(rel_bias):
    return pl.pallas_call(
        _bias_kernel,
        in_specs=[pl.BlockSpec(memory_space=pltpu.SMEM)],
        out_shape=(jax.ShapeDtypeStruct((2, N_HEADS, 128, 128), F32),
                   jax.ShapeDtypeStruct((N_HEADS * GROUP, 128), F32),
                   jax.ShapeDtypeStruct((N_HEADS * GROUP, 128), F32)),
        name="bias_tables",
    )(rel_bias)


def _in_kernel(x_ref, sh_ref, sc_ref, nw_ref, w_ref, wt_ref, lnw_ref, lnb_ref,
               conv_ref, gz_ref, q_ref, kf_ref, vf_ref, k16_ref, v16_ref, ga_ref, gb_ref, qi_ref,
               small_ref, ki_ref, kb0_ref, kb1_ref, smt_ref):
    x = x_ref[...]
    h = x * lax.rsqrt(jnp.mean(x * x, -1, keepdims=True) + EPS) * nw_ref[...]
    h = h * (1.0 + sc_ref[...]) + sh_ref[...]
    hb = h.astype(BF16)

    def proj(a, b):
        return _mm(hb, w_ref[:, a:b])

    for c in range(0, 3072, 1024):
        conv_ref[:, c:c + 1024] = proj(C_CONV + c, C_CONV + c + 1024)
    gz_ref[...] = proj(C_GZ, C_Q)
    q_ref[...] = (proj(C_Q, C_K) * ATT_SCALE).astype(BF16)
    k = proj(C_K, C_V)
    kf_ref[...] = k
    k16_ref[...] = k.astype(BF16)
    v = proj(C_V, C_GA)
    vf_ref[...] = v
    v16_ref[...] = v.astype(BF16)
    ga_ref[...] = _sigmoid(proj(C_GA, C_GB))
    gb_ref[...] = _sigmoid(proj(C_GB, C_QI))
    qi_ref[...] = proj(C_QI, C_SM).astype(BF16)
    z = proj(C_SM, C_SM2)
    small_ref[...] = z
    smt_ref[...] = _nt(wt_ref[...], hb)
    lane = lax.broadcasted_iota(I32, z.shape, 1)
    for zz, lo, out_ref in ((z, True, kb0_ref), (proj(C_SM2, C_END), False, kb1_ref)):
        m = (lane < IDX_DIM) if lo else (lane >= IDX_DIM)
        zk = jnp.where(m, zz, 0.0)
        mean = jnp.sum(zk, -1, keepdims=True) * (1.0 / IDX_DIM)
        xc = jnp.where(m, zz - mean, 0.0)
        var = jnp.sum(xc * xc, -1, keepdims=True) * (1.0 / IDX_DIM)
        row = 0 if lo else 1
        kn = xc * lax.rsqrt(var + EPS) * lnw_ref[row:row + 1, :] + lnb_ref[row:row + 1, :]
        out_ref[...] = kn.astype(BF16)
        if lo:
            ki_ref[...] = kn[:, :IDX_DIM]


def in_proj(x, sh, sc, norm_w, w_cat, w_smt, lnw2, lnb2, tm=256):
    t = x.shape[0]
    tm = min(tm, t)
    per_row = sh.shape[0] != 1
    mod_spec = (pl.BlockSpec((tm, D_MODEL), lambda i: (i, 0)) if per_row
                else pl.BlockSpec((1, D_MODEL), lambda i: (0, 0)))

    def rows(n, dt):
        return pl.BlockSpec((tm, n), lambda i: (i, 0)), jax.ShapeDtypeStruct((t, n), dt)

    outs = [rows(3072, F32), rows(1024, F32), rows(1024, BF16), rows(1024, F32), rows(1024, F32),
            rows(1024, BF16), rows(1024, BF16), rows(1024, F32), rows(1024, F32), rows(512, BF16),
            rows(128, F32), rows(IDX_DIM, F32), rows(128, BF16), rows(128, BF16),
            (pl.BlockSpec((128, tm), lambda i: (0, i)), jax.ShapeDtypeStruct((128, t), F32))]
    return pl.pallas_call(
        _in_kernel,
        grid=(t // tm,),
        in_specs=[pl.BlockSpec((tm, D_MODEL), lambda i: (i, 0)), mod_spec, mod_spec,
                  _const_spec((1, D_MODEL)), _const_spec((D_MODEL, C_END)), _const_spec((128, D_MODEL)),
                  _const_spec((2, 128)), _const_spec((2, 128))],
        out_specs=[o[0] for o in outs],
        out_shape=[o[1] for o in outs],
        compiler_params=_cp(("parallel",)),
        name="in_proj",
    )(x, sh, sc, norm_w, w_cat, w_smt, lnw2, lnb2)


def _b16(a):
    return a.astype(BF16)


def _mmb(a, b):
    return jnp.dot(_b16(a), _b16(b), preferred_element_type=F32)


def _split(a):
    hi = _b16(a)
    return hi, _b16(a - hi.astype(F32))


def _mm3(a, b):
    ah, al = a
    bh, bl = b
    return (jnp.dot(ah, bh, preferred_element_type=F32) + jnp.dot(ah, bl, preferred_element_type=F32)
            + jnp.dot(al, bh, preferred_element_type=F32))


def _gdn_kernel(*refs, R, C, N, has_hist):
    s_scr = refs[-N_HEADS:]
    refs = refs[:-N_HEADS]
    if has_hist:
        conv_ref, hist_ref, small_ref, gz_ref, s0_ref, cw_ref, misc_ref, nw_ref, o_ref, sout_ref, xbuf = refs
    else:
        (conv_ref, small_ref, smt_ref, at_ref, dtt_ref, gz_ref, s0_ref, cw_ref, misc_ref, nw_ref,
         o_ref, sout_ref, xbuf) = refs
    nch = max(R // C, 1)
    RC = nch * C
    shift = C.bit_length() - 1
    n = pl.program_id(1)

    @pl.when(n == 0)
    def _():
        xbuf[...] = jnp.zeros_like(xbuf)
        for h in range(N_HEADS):
            s_scr[h][...] = s0_ref[0, h]

    x = conv_ref[...]
    validf = None
    if has_hist:
        rows_r = lax.broadcasted_iota(I32, (R, 1), 0)
        x = jnp.where((rows_r % GROUP) < FIRST_TOK, hist_ref[...], x)
        rows_c = lax.broadcasted_iota(I32, (RC, 1), 0)
        validf = jnp.where((rows_c < R) & ((rows_c % GROUP) >= FIRST_TOK), 1.0, 0.0)
    xbuf[8:8 + R, :] = x

    small = small_ref[...]
    if R < RC:
        small = jnp.concatenate([small, jnp.zeros((RC - R, 128), F32)], axis=0)
    misc = misc_ref[...]
    gfull = -jnp.exp(misc[0:1]) * _softplus(small + misc[1:2])
    betaf = _sigmoid(small)
    if validf is not None:
        gfull = gfull * validf
        betaf = betaf * validf
    ri = lax.broadcasted_iota(I32, (RC, RC), 0)
    ci = lax.broadcasted_iota(I32, (RC, RC), 1)
    same = (ri >> shift) == (ci >> shift)
    gcol = _mm(jnp.where(same & (ri >= ci), 1.0, 0.0), gfull, HI)
    if not has_hist:
        gt = -jnp.exp(at_ref[...]) * _softplus(smt_ref[...] + dtt_ref[...])
        grow_all = _mm(gt[L_A:L_A + N_HEADS, :], jnp.where(same & (ri <= ci), 1.0, 0.0), HI)
    rc = lax.broadcasted_iota(I32, (C, C), 0)
    cc = lax.broadcasted_iota(I32, (C, C), 1)
    tril = rc >= cc
    strict = rc > cc
    eye = rc == cc
    ones_cc = jnp.ones((C, C), F32)
    w = cw_ref[...]

    pre = []
    zrows = jnp.zeros((128 - C, 128), F32)
    for c in range(nch):
        r0 = c * C
        y = (w[3:4] * xbuf[8 + r0:8 + r0 + C, :] + w[2:3] * xbuf[7 + r0:7 + r0 + C, :]
             + w[1:2] * xbuf[6 + r0:6 + r0 + C, :] + w[0:1] * xbuf[5 + r0:5 + r0 + C, :])
        y = _silu(y)
        vf = None if validf is None else validf[r0:r0 + C]
        for h in range(N_HEADS):
            q = y[:, h * 128:(h + 1) * 128]
            k = y[:, 1024 + h * 128:1024 + (h + 1) * 128]
            v = y[:, 2048 + h * 128:2048 + (h + 1) * 128]
            q = q * lax.rsqrt(jnp.sum(q * q, -1, keepdims=True) + EPS) * (HEAD_DIM ** -0.5)
            k = k * lax.rsqrt(jnp.sum(k * k, -1, keepdims=True) + EPS)
            if vf is not None:
                k = k * vf
                v = v * vf
            bc = jnp.broadcast_to(betaf[r0:r0 + C, L_B + h:L_B + h + 1], (C, 128))
            gc = jnp.broadcast_to(gcol[r0:r0 + C, L_A + h:L_A + h + 1], (C, 128))
            gb = gc[:, :C]
            if has_hist:
                grow = _mm(ones_cc, jnp.where(eye, gb, 0.0), HI)
            else:
                grow = jnp.broadcast_to(grow_all[h:h + 1, r0:r0 + C], (C, C))
            dec = jnp.where(tril, jnp.exp(jnp.minimum(gb - grow, 0.0)), 0.0)
            eg = jnp.exp(gc)
            glast = gc[C - 1:C, :]
            kd = k * jnp.exp(glast - gc)
            kdt = jnp.concatenate([kd, zrows], axis=0).T[:, :C]
            pre.append((q, k, v, bc, dec, eg, jnp.exp(glast), _b16(kdt)))

    nprob = nch * N_HEADS
    a_l = []
    for q, k, v, bc, dec, eg, ge, kdt in pre:
        kh, kl = _split(k)
        a_l.append(jnp.where(strict, bc[:, :C] * (_nt(kh, kh) + _nt(kh, kl) + _nt(kl, kh)) * dec, 0.0))
    xo_l = [-a for a in a_l]
    pw_l = []
    for a in a_l:
        a_s = _split(a)
        pw_l.append(_mm3(a_s, a_s))
    span = 2
    while span < C:
        span *= 2
        pws_l = [_split(pw) for pw in pw_l]
        xo_l = [xo + pw + _mm3(_split(xo), pws) for xo, pw, pws in zip(xo_l, pw_l, pws_l)]
        if span < C:
            pw_l = [_mm3(pws, pws) for pws in pws_l]
    um_l, wm_l, qk_l = [], [], []
    for (q, k, v, bc, dec, eg, ge, kdt), xo in zip(pre, xo_l):
        rv = v * bc
        rk = k * (bc * eg)
        xo_s = _split(xo)
        um_l.append(rv + _mm3(xo_s, _split(rv)))
        wm_l.append(rk + _mm3(xo_s, _split(rk)))
        qk_l.append(_nt(_b16(q), _b16(k)) * dec)

    outs = [None] * nprob
    for c in range(nch):
        probs = range(c * N_HEADS, (c + 1) * N_HEADS)
        s_l = [s_scr[h][...] for h in range(N_HEADS)]
        sb_l = [_b16(s) for s in s_l]
        u_l = [um_l[p] - _mm(_b16(wm_l[p]), sb) for p, sb in zip(probs, sb_l)]
        for h, p in enumerate(probs):
            q, k, v, bc, dec, eg, ge, kdt = pre[p]
            outs[p] = _mm(_b16(q * eg), sb_l[h]) + _mmb(qk_l[p], u_l[h])
            s_scr[h][...] = s_l[h] * ge + _mm(kdt, _b16(u_l[h]))

    ro = min(R, C)
    for c in range(nch):
        r0 = c * C
        for h in range(N_HEADS):
            o = outs[c * N_HEADS + h][:ro]
            on = o * lax.rsqrt(jnp.mean(o * o, -1, keepdims=True) + EPS) * nw_ref[...]
            gz = gz_ref[r0:r0 + ro, h * 128:(h + 1) * 128]
            o_ref[r0:r0 + ro, h * 128:(h + 1) * 128] = (on * _silu(gz)).astype(o_ref.dtype)

    if N > 1:
        xbuf[0:8, :] = xbuf[RC:RC + 8, :]

    @pl.when(n == N - 1)
    def _():
        for h in range(N_HEADS):
            sout_ref[0, h] = s_scr[h][...]


def gdn(conv_in, hist, small, smt, gz, s0, conv_w, misc, norm_w, *, nseq, rows, out_dtype):
    C = GDN_CHUNK
    t = conv_in.shape[0]
    nstep = t // (nseq * rows)
    has_hist = hist is not None
    rc = max(rows // C, 1) * C

    def rspec(n):
        return pl.BlockSpec((rows, n), lambda b, c: (b * nstep + c, 0))

    def cspec(shape):
        return pl.BlockSpec(shape, lambda b, c: (0,) * len(shape))

    if has_hist:
        in_specs = [rspec(3072), rspec(3072), rspec(128), rspec(1024)]
        args = [conv_in, hist, small, gz]
    else:
        lane_rep = jnp.ones((1, rows), F32)
        at = misc[0].reshape(128, 1) * lane_rep
        dtt = misc[1].reshape(128, 1) * lane_rep
        in_specs = [rspec(3072), rspec(128), pl.BlockSpec((128, rows), lambda b, c: (0, b * nstep + c)),
                    cspec((128, rows)), cspec((128, rows)), rspec(1024)]
        args = [conv_in, small, smt, at, dtt, gz]
    in_specs += [pl.BlockSpec((1, N_HEADS, 128, 128), lambda b, c: (b, 0, 0, 0)),
                 cspec((GDN_CONV, 3072)), cspec((8, 128)), cspec((1, 128))]
    args += [s0, conv_w, misc, norm_w]
    return pl.pallas_call(
        functools.partial(_gdn_kernel, R=rows, C=C, N=nstep, has_hist=has_hist),
        grid=(nseq, nstep),
        in_specs=in_specs,
        out_specs=[rspec(1024), pl.BlockSpec((1, N_HEADS, 128, 128), lambda b, c: (b, 0, 0, 0))],
        out_shape=[jax.ShapeDtypeStruct((t, 1024), out_dtype),
                   jax.ShapeDtypeStruct((nseq, N_HEADS, 128, 128), F32)],
        scratch_shapes=[pltpu.VMEM((8 + rc, 3072), F32)] + [pltpu.VMEM((128, 128), F32)] * N_HEADS,
        compiler_params=_cp(("arbitrary", "arbitrary")),
        name="gdn",
    )(*args)


def _sort_key(s, col=None):
    bits = pltpu.bitcast(s + 0.0, I32)
    key = bits ^ ((bits >> 31) & 0x7FFFFFFF)
    if col is None:
        return key
    return jnp.where(s == 0.0, -1 - col, key)


def _count_ge(keys_ref, nkt, cand, tq, tk):
    rb = min(128, tq)
    parts = []
    for r0 in range(0, tq, rb):
        cr = jnp.broadcast_to(cand[r0:r0 + rb], (rb, 128))

        def body(j, acc, r0=r0, cr=cr):
            kt = keys_ref[r0:r0 + rb, pl.ds(pl.multiple_of(j * tk, tk), tk)]
            for g in range(tk // 128):
                acc = acc + jnp.where(kt[:, g * 128:(g + 1) * 128] >= cr, 1, 0)
            return acc

        acc = lax.fori_loop(0, nkt, body, jnp.zeros((rb, 128), I32))
        parts.append(jnp.sum(acc, axis=-1, keepdims=True))
    return parts[0] if len(parts) == 1 else jnp.concatenate(parts, axis=0)


def _kth_largest(keys_ref, nkt, rmax, tq, tk, k):
    hi = _sort_key(rmax) + 1
    lo0 = _sort_key(jnp.where(rmax >= 0.0, rmax * 0.125, rmax * 8.0))
    c0 = _count_ge(keys_ref, nkt, lo0, tq, tk)
    ok0 = c0 >= k
    lo = jnp.where(ok0, lo0, INT_MIN)
    cnt = jnp.where(ok0, c0, nkt * tk)
    hi = jnp.where(ok0, hi, lo0)
    c_pos = _count_ge(keys_ref, nkt, jnp.ones((tq, 1), I32), tq, tk)
    up = (c_pos >= k) & (lo < 1)
    lo = jnp.where(up, 1, lo)
    cnt = jnp.where(up, c_pos, cnt)
    hi = jnp.where((c_pos < k) & (hi > 1), 1, hi)

    def mid_of(lo, hi):
        return (lo >> 1) + (hi >> 1) + (lo & hi & 1)

    def n_active(lo, hi, cnt):
        return jnp.max(jnp.where((cnt != k) & (mid_of(lo, hi) != lo), 1, 0))

    def cond(st):
        return st[3] > 0

    def body(st):
        lo, hi, cnt, _ = st
        mid = mid_of(lo, hi)
        active = (cnt != k) & (mid != lo)
        c = _count_ge(keys_ref, nkt, mid, tq, tk)
        up = active & (c >= k)
        down = active & (c < k)
        lo = jnp.where(up, mid, lo)
        cnt = jnp.where(up, c, cnt)
        hi = jnp.where(down, mid, hi)
        return lo, hi, cnt, n_active(lo, hi, cnt)

    lo, _, cnt, _ = lax.while_loop(cond, body, (lo, hi, cnt, n_active(lo, hi, cnt)))
    return lo, cnt


def _emit_selection(keys_ref, nkt, thr, cnt, tq, tk, k, emit):
    tied = (cnt > k) & (thr > KEY_NEG_INF)
    any_tied = jnp.max(jnp.where(tied, 1, 0))

    @pl.when(any_tied == 0)
    def _():
        def body(j, carry):
            kt = keys_ref[:, pl.ds(pl.multiple_of(j * tk, tk), tk)]
            emit(j, kt >= thr)
            return carry
        lax.fori_loop(0, nkt, body, 0)

    @pl.when(any_tied != 0)
    def _():
        c_gt = _count_ge(keys_ref, nkt, thr + 1, tq, tk)
        budget = jnp.where(tied, k - c_gt, nkt * tk).astype(F32)
        ri = lax.broadcasted_iota(I32, (tk, tk), 0)
        ci = lax.broadcasted_iota(I32, (tk, tk), 1)
        upper = jnp.where(ri <= ci, 1.0, 0.0).astype(BF16)

        def body(j, seen):
            kt = keys_ref[:, pl.ds(pl.multiple_of(j * tk, tk), tk)]
            eq = jnp.where(kt == thr, 1.0, 0.0)
            rank = _mm(eq.astype(BF16), upper) + seen
            sel = jnp.where(kt > thr, 1.0, jnp.where(rank <= budget, eq, 0.0))
            emit(j, sel > 0.5)
            return seen + jnp.sum(eq, axis=-1, keepdims=True)
        lax.fori_loop(0, nkt, body, jnp.zeros((tq, 1), F32))


def _idx_kernel(qi_ref, small_ref, kb0_ref, kb1_ref, mask_ref, keys_ref, *, TQ, TK, T, k):
    i = pl.program_id(0)
    nkt = (i * TQ + TQ + TK - 1) // TK
    w = small_ref[:, L_WI:L_WI + IDX_HEADS] * IDX_W_SCALE
    pos_q = i * TQ + lax.broadcasted_iota(I32, (TQ, 1), 0)

    def score_tile(j, rmax):
        off = pl.multiple_of(j * TK, TK)
        ka = kb0_ref[pl.ds(off, TK), :]
        kb = kb1_ref[pl.ds(off, TK), :]
        acc = jnp.zeros((TQ, TK), F32)
        for p in range(IDX_HEADS // 2):
            q2 = qi_ref[:, p * 128:(p + 1) * 128]
            acc = acc + w[:, 2 * p:2 * p + 1] * jnp.maximum(_nt(q2, ka), 0.0)
            acc = acc + w[:, 2 * p + 1:2 * p + 2] * jnp.maximum(_nt(q2, kb), 0.0)
        pos_k = off + lax.broadcasted_iota(I32, (1, TK), 1)
        acc = jnp.where(pos_k <= pos_q, acc, NEG_INF)
        keys_ref[:, pl.ds(off, TK)] = _sort_key(acc, pos_k)
        for g in range(TK // 128):
            rmax = jnp.maximum(rmax, acc[:, g * 128:(g + 1) * 128])
        return rmax

    rmax = lax.fori_loop(0, nkt, score_tile, jnp.full((TQ, 128), NEG_INF, F32))
    thr, cnt = _kth_largest(keys_ref, nkt, jnp.max(rmax, axis=-1, keepdims=True), TQ, TK, k)

    def emit(j, sel):
        off = pl.multiple_of(j * TK, TK)
        pos_k = off + lax.broadcasted_iota(I32, (1, TK), 1)
        m = jnp.where(pos_k <= pos_q, jnp.where(sel, 1, 0), 0)
        mask_ref[:, pl.ds(off, TK)] = m.astype(jnp.int8)

    _emit_selection(keys_ref, nkt, thr, cnt, TQ, TK, k, emit)

    def zero_tile(j, carry):
        mask_ref[:, pl.ds(pl.multiple_of(j * TK, TK), TK)] = jnp.zeros((TQ, TK), jnp.int8)
        return carry

    lax.fori_loop(nkt, T // TK, zero_tile, 0)


def prompt_topk_mask(qi16, small, kb0, kb1, k, TQ=256, TK=512):
    t = qi16.shape[0]
    TK = min(TK, t)
    TQ = min(TQ, t)
    return pl.pallas_call(
        functools.partial(_idx_kernel, TQ=TQ, TK=TK, T=t, k=k),
        grid=(t // TQ,),
        in_specs=[pl.BlockSpec((TQ, 512), lambda i: (i, 0)),
                  pl.BlockSpec((TQ, 128), lambda i: (i, 0)),
                  _const_spec((t, 128)), _const_spec((t, 128))],
        out_specs=pl.BlockSpec((TQ, t), lambda i: (i, 0)),
        out_shape=jax.ShapeDtypeStruct((t, t), jnp.int8),
        scratch_shapes=[pltpu.VMEM((TQ, t), I32)],
        compiler_params=_cp(("parallel",)),
        name="prompt_topk_mask",
    )(qi16, small, kb0, kb1)


def _att_kernel(qa_ref, ka_ref, q_ref, k_ref, v_ref, mask_ref, e_ref, o_ref, *scr, TA, RS):
    m_scr, l_scr, acc_scr = scr[0:8], scr[8:16], scr[16:24]
    s_scr, p_scr, b_scr = scr[24:26], scr[26:28], scr[28]
    p = pl.program_id(0)
    i = qa_ref[p]
    j = ka_ref[p]

    @pl.when(j == 0)
    def _():
        for h in range(N_HEADS):
            m_scr[h][...] = jnp.full((TA, 1), -1e30, F32)
            l_scr[h][...] = jnp.zeros((TA, HEAD_DIM), F32)
            acc_scr[h][...] = jnp.zeros((TA, HEAD_DIM), F32)

    for r0 in range(0, TA, 32):
        b_scr[r0:r0 + 32, :] = jnp.where(mask_ref[r0:r0 + 32, :].astype(I32) != 0, 0.0, NEG_INF)

    def update(near):
        nb = TA // 128
        if near:
            fd = jnp.where(j == i, 1.0, 0.0)
        def logits(h):
            hs = slice(h * 128, (h + 1) * 128)
            sb = s_scr[h % 2]
            sb[...] = _nt(q_ref[:, hs], k_ref[:, hs]) + b_scr[...]
            if near:
                for b in range(nb):
                    bs = slice(b * 128, (b + 1) * 128)
                    sb[bs, bs] += fd * e_ref[0, h]
                    if b + 1 < nb:
                        sb[(b + 1) * 128:(b + 2) * 128, bs] += fd * e_ref[1, h]
                sb[0:128, TA - 128:TA] += (1.0 - fd) * e_ref[1, h]

        def row_max(h):
            m_all = m_scr[h][...]
            return m_all, [jnp.maximum(m_all[r0:r0 + RS], jnp.max(s_scr[h % 2][r0:r0 + RS, :], axis=-1, keepdims=True))
                           for r0 in range(0, TA, RS)]

        def probs(h, m_parts):
            for n, r0 in enumerate(range(0, TA, RS)):
                rs = slice(r0, r0 + RS)
                p_scr[h % 2][rs, :] = jnp.exp(s_scr[h % 2][rs, :] - m_parts[n]).astype(BF16)

        def accumulate(h, m_all, m_parts):
            hs = slice(h * 128, (h + 1) * 128)
            m_new = jnp.concatenate(m_parts, axis=0)
            alpha = jnp.exp(m_all - m_new)
            m_scr[h][...] = m_new
            pv = _mm(p_scr[h % 2][...], jnp.concatenate([v_ref[:, hs], jnp.ones((TA, HEAD_DIM), BF16)], axis=1))
            acc_scr[h][...] = alpha * acc_scr[h][...] + pv[:, :HEAD_DIM]
            l_scr[h][...] = alpha * l_scr[h][...] + pv[:, HEAD_DIM:]

        for h in range(N_HEADS):
            logits(h)
            m_all, m_parts = row_max(h)
            probs(h, m_parts)
            accumulate(h, m_all, m_parts)

    @pl.when(j >= i - 1)
    def _():
        update(True)

    @pl.when(j < i - 1)
    def _():
        update(False)

    @pl.when(j == i)
    def _():
        for h in range(N_HEADS):
            o_ref[:, h * 128:(h + 1) * 128] = (acc_scr[h][...] / l_scr[h][...]).astype(o_ref.dtype)


def prompt_attention(q16, k16, v16, mask, e_tab, TA=512, RS=16):
    t = q16.shape[0]
    TA = min(TA, t)
    nb = t // TA
    qa = np.concatenate([np.full(i + 1, i, np.int32) for i in range(nb)])
    ka = np.concatenate([np.arange(i + 1, dtype=np.int32) for i in range(nb)])
    grid_spec = pltpu.PrefetchScalarGridSpec(
        num_scalar_prefetch=2,
        grid=(len(qa),),
        in_specs=[pl.BlockSpec((TA, 1024), lambda p, qa, ka: (qa[p], 0)),
                  pl.BlockSpec((TA, 1024), lambda p, qa, ka: (ka[p], 0)),
                  pl.BlockSpec((TA, 1024), lambda p, qa, ka: (ka[p], 0)),
                  pl.BlockSpec((TA, TA), lambda p, qa, ka: (qa[p], ka[p])),
                  pl.BlockSpec((2, N_HEADS, 128, 128), lambda p, qa, ka: (0, 0, 0, 0))],
        out_specs=pl.BlockSpec((TA, 1024), lambda p, qa, ka: (qa[p], 0)),
        scratch_shapes=([pltpu.VMEM((TA, 1), F32)] * N_HEADS + [pltpu.VMEM((TA, HEAD_DIM), F32)] * (2 * N_HEADS)
                        + [pltpu.VMEM((TA, TA), F32)] * 2 + [pltpu.VMEM((TA, TA), BF16)] * 2
                        + [pltpu.VMEM((TA, TA), F32)]))
    return pl.pallas_call(
        functools.partial(_att_kernel, TA=TA, RS=RS),
        grid_spec=grid_spec,
        out_shape=jax.ShapeDtypeStruct((t, 1024), BF16),
        compiler_params=_cp(("arbitrary",)),
        name="prompt_attention",
    )(jnp.asarray(qa), jnp.asarray(ka), q16, k16, v16, mask, e_tab)


def _page_copies(pt_ref, cache_ref, buf_ref, sem_ref, b, first_page, npages, slot):
    out = []
    for g in range(npages):
        pid = pt_ref[b, first_page + g]
        out.append(pltpu.make_async_copy(cache_ref.at[0, pid], buf_ref.at[slot, pl.ds(g * PAGE, PAGE)],
                                         sem_ref.at[slot]))
    return out


def _sidx_kernel(pt_ref, qm_ref, w_ref, knew_ref, cache_ref, sc_ref, kbuf, sem, *, NP, NB):
    b = pl.program_id(0)
    slot = b % 2

    def copies(bq, sl):
        return [pltpu.make_async_copy(cache_ref.at[0, pt_ref[bq, g]], kbuf.at[sl, :, pl.ds(g * PAGE, PAGE)],
                                      sem.at[sl]) for g in range(NP)]

    @pl.when(b == 0)
    def _():
        for c in copies(0, 0):
            c.start()

    @pl.when(b + 1 < NB)
    def _():
        for c in copies(b + 1, 1 - slot):
            c.start()

    for c in copies(b, slot):
        c.wait()

    qm = qm_ref[0]
    w = w_ref[0]

    def fold(s):
        s = jnp.maximum(s, 0.0) * w
        out = s[0:GROUP]
        for h in range(1, IDX_HEADS):
            out = out + s[h * GROUP:(h + 1) * GROUP]
        return out

    past = NP * PAGE
    CH = 2048
    for c0 in range(0, past, CH):
        kc = kbuf[slot, :, c0:c0 + CH].astype(BF16)
        sc_ref[0, :, c0:c0 + CH] = fold(_mm(qm, kc))
    knew = jnp.concatenate([knew_ref[...], jnp.zeros((128 - GROUP, IDX_DIM), F32)], axis=0).astype(BF16)
    tail = fold(_nt(qm, knew))
    r = lax.broadcasted_iota(I32, (GROUP, 128), 0)
    c = lax.broadcasted_iota(I32, (GROUP, 128), 1)
    sc_ref[0, :, past:past + 128] = jnp.where((c >= FIRST_TOK) & (c <= r), tail, NEG_INF)


def sample_scores(page_table, qm, wcol, ki_s, cache_idx):
    nb, npg = page_table.shape
    past = npg * PAGE
    grid_spec = pltpu.PrefetchScalarGridSpec(
        num_scalar_prefetch=1,
        grid=(nb,),
        in_specs=[pl.BlockSpec((1, IDX_HEADS * GROUP, IDX_DIM), lambda b, pt: (b, 0, 0)),
                  pl.BlockSpec((1, IDX_HEADS * GROUP, 1), lambda b, pt: (b, 0, 0)),
                  pl.BlockSpec((GROUP, IDX_DIM), lambda b, pt: (b, 0)),
                  pl.BlockSpec(memory_space=pl.ANY)],
        out_specs=pl.BlockSpec((1, GROUP, past + 128), lambda b, pt: (b, 0, 0)),
        scratch_shapes=[pltpu.VMEM((2, IDX_DIM, past), F32), pltpu.SemaphoreType.DMA((2,))])
    return pl.pallas_call(
        functools.partial(_sidx_kernel, NP=npg, NB=nb),
        grid_spec=grid_spec,
        out_shape=jax.ShapeDtypeStruct((nb, GROUP, past + 128), F32),
        compiler_params=_cp(("arbitrary",)),
        name="sample_scores",
    )(page_table, qm, wcol, ki_s, jnp.swapaxes(cache_idx, 2, 3))


def _ssel_kernel(sc_ref, am_ref, keys_ref, *, TQ, TK, W, k):
    nkt = W // TK

    def to_keys(j, rmax):
        off = pl.multiple_of(j * TK, TK)
        s = sc_ref[:, pl.ds(off, TK)]
        keys_ref[:, pl.ds(off, TK)] = _sort_key(s, off + lax.broadcasted_iota(I32, (1, TK), 1))
        for g in range(TK // 128):
            rmax = jnp.maximum(rmax, s[:, g * 128:(g + 1) * 128])
        return rmax

    rmax = lax.fori_loop(0, nkt, to_keys, jnp.full((TQ, 128), NEG_INF, F32))
    thr, cnt = _kth_largest(keys_ref, nkt, jnp.max(rmax, axis=-1, keepdims=True), TQ, TK, k)

    def emit(j, sel):
        off = pl.multiple_of(j * TK, TK)
        valid = sc_ref[:, pl.ds(off, TK)] > NEG_INF
        am_ref[:, pl.ds(off, TK)] = jnp.where(valid, jnp.where(sel, 1.0, 0.0), 0.0)

    _emit_selection(keys_ref, nkt, thr, cnt, TQ, TK, k, emit)


def sample_topk_mask(scores, k, TQ=32, TK=128):
    r, w = scores.shape
    TQ = min(TQ, r)
    return pl.pallas_call(
        functools.partial(_ssel_kernel, TQ=TQ, TK=TK, W=w, k=k),
        grid=(r // TQ,),
        in_specs=[pl.BlockSpec((TQ, w), lambda i: (i, 0))],
        out_specs=pl.BlockSpec((TQ, w), lambda i: (i, 0)),
        out_shape=jax.ShapeDtypeStruct((r, w), F32),
        scratch_shapes=[pltpu.VMEM((TQ, w), I32)],
        compiler_params=_cp(("parallel",)),
        name="sample_topk_mask",
    )(scores)


def _satt_kernel(pt_ref, q_ref, sel_ref, knew_ref, vnew_ref, tm_ref, hp_ref, blx_ref, ck_ref, cv_ref, o_ref,
                 kbuf, vbuf, ksem, vsem, m_scr, l_scr, acc_scr, *, PC, NCH, NB):
    b = pl.program_id(0)
    c = pl.program_id(1)
    g = b * NCH + c
    slot = g % 2
    CW = PC * PAGE

    def copies(bq, cq, sl):
        return (_page_copies(pt_ref, ck_ref, kbuf, ksem, bq, cq * PC, PC, sl)
                + _page_copies(pt_ref, cv_ref, vbuf, vsem, bq, cq * PC, PC, sl))

    @pl.when(g == 0)
    def _():
        for cp in copies(0, 0, 0):
            cp.start()

    @pl.when(g + 1 < NB * NCH)
    def _():
        nxt = g + 1
        for cp in copies(nxt // NCH, nxt % NCH, 1 - slot):
            cp.start()

    for cp in copies(b, c, slot):
        cp.wait()

    @pl.when(c == 0)
    def _():
        m_scr[...] = jnp.full(m_scr.shape, -1e30, F32)
        l_scr[...] = jnp.zeros(l_scr.shape, F32)
        acc_scr[...] = jnp.zeros(acc_scr.shape, F32)

    def online(s, v16):
        m_old = m_scr[...]
        m_new = jnp.maximum(m_old, jnp.max(s, axis=-1, keepdims=True))
        alpha = jnp.exp(m_old - m_new)
        pr = jnp.exp(s - m_new)
        l_scr[...] = alpha * l_scr[...] + jnp.sum(pr, axis=-1, keepdims=True)
        acc_scr[...] = alpha * acc_scr[...] + _mm(_b16(pr), v16)
        m_scr[...] = m_new

    NX = CW * N_HEADS
    q64 = q_ref[0]
    last = jnp.where(c == NCH - 1, 1.0, 0.0)
    ri = lax.broadcasted_iota(I32, (PAGE, PAGE * N_HEADS), 0)
    ci = lax.broadcasted_iota(I32, (PAGE, PAGE * N_HEADS), 1)
    spread = jnp.where((ci >> 3) == ri, 1.0, 0.0).astype(BF16)
    sx = _mm(_b16(sel_ref[0].reshape(PC * GROUP, PAGE)), spread)
    selx = jnp.concatenate([sx[p * GROUP:(p + 1) * GROUP] for p in range(PC)], axis=1)
    addm = jnp.where(jnp.concatenate([selx] * N_HEADS, axis=0) > 0.5, hp_ref[...], NEG_INF)
    s = _nt(q64, _b16(kbuf[slot].reshape(NX, HEAD_DIM))) + addm
    s = s + last * jnp.concatenate([jnp.zeros((N_HEADS * GROUP, NX - PAGE * N_HEADS), F32), blx_ref[...]], axis=1)
    online(s, _b16(vbuf[slot].reshape(NX, HEAD_DIM)))

    @pl.when(c == NCH - 1)
    def _():
        zpad = jnp.zeros((128 - N_HEADS * GROUP, HEAD_DIM), F32)
        kn = _b16(jnp.concatenate([knew_ref[0], zpad], axis=0))
        vn = _b16(jnp.concatenate([vnew_ref[0], zpad], axis=0))
        online(_nt(q64, kn) + tm_ref[0], vn)
        out = acc_scr[...] / l_scr[...]
        for h in range(N_HEADS):
            o_ref[:, h * 128:(h + 1) * 128] = out[h * GROUP:(h + 1) * GROUP, :]


def sample_attention(page_table, q64, selp, kn64, vn64, tmask, hp, blx, cache_k, cache_v, PC=8):
    nb, npg = page_table.shape
    PC = min(PC, npg)
    nch = npg // PC
    R = N_HEADS * GROUP
    grid_spec = pltpu.PrefetchScalarGridSpec(
        num_scalar_prefetch=1,
        grid=(nb, nch),
        in_specs=[pl.BlockSpec((1, R, HEAD_DIM), lambda b, c, pt: (b, 0, 0)),
                  pl.BlockSpec((1, PC, GROUP, PAGE), lambda b, c, pt: (b, c, 0, 0)),
                  pl.BlockSpec((1, R, HEAD_DIM), lambda b, c, pt: (b, 0, 0)),
                  pl.BlockSpec((1, R, HEAD_DIM), lambda b, c, pt: (b, 0, 0)),
                  pl.BlockSpec((1, R, 128), lambda b, c, pt: (b, 0, 0)),
                  pl.BlockSpec((R, PC * PAGE * N_HEADS), lambda b, c, pt: (0, 0), pipeline_mode=pl.Buffered(1)),
                  pl.BlockSpec((R, PAGE * N_HEADS), lambda b, c, pt: (0, 0), pipeline_mode=pl.Buffered(1)),
                  pl.BlockSpec(memory_space=pl.ANY),
                  pl.BlockSpec(memory_space=pl.ANY)],
        out_specs=pl.BlockSpec((GROUP, 1024), lambda b, c, pt: (b, 0)),
        scratch_shapes=[pltpu.VMEM((2, PC * PAGE, N_HEADS, HEAD_DIM), F32),
                        pltpu.VMEM((2, PC * PAGE, N_HEADS, HEAD_DIM), F32),
                        pltpu.SemaphoreType.DMA((2,)), pltpu.SemaphoreType.DMA((2,)),
                        pltpu.VMEM((R, 1), F32), pltpu.VMEM((R, 1), F32), pltpu.VMEM((R, HEAD_DIM), F32)])
    return pl.pallas_call(
        functools.partial(_satt_kernel, PC=PC, NCH=nch, NB=nb),
        grid_spec=grid_spec,
        out_shape=jax.ShapeDtypeStruct((nb * GROUP, 1024), F32),
        compiler_params=_cp(("arbitrary", "arbitrary")),
        name="sample_attention",
    )(page_table, q64, selp, kn64, vn64, tmask, hp, blx, cache_k, cache_v)


def _mix_kernel(oa_ref, ob_ref, ga_ref, gb_ref, x_ref, g1_ref, wa_ref, wb_ref, wo_ref, o_ref):
    ya = _mm(oa_ref[...].astype(BF16), wa_ref[...])
    yb = _mm(ob_ref[...].astype(BF16), wb_ref[...])
    mixed = ga_ref[...] * ya + gb_ref[...] * yb
    o_ref[...] = x_ref[...] + g1_ref[...] * _mm(mixed.astype(BF16), wo_ref[...])


def mix(oa, ob, ga, gb, x, g1, wa, wb, wo, tm=512):
    t = x.shape[0]
    tm = min(tm, t)
    per_row = g1.shape[0] != 1
    mod_spec = (pl.BlockSpec((tm, D_MODEL), lambda i: (i, 0)) if per_row
                else pl.BlockSpec((1, D_MODEL), lambda i: (0, 0)))
    row = pl.BlockSpec((tm, D_MODEL), lambda i: (i, 0))
    wspec = _const_spec((D_MODEL, D_MODEL))
    return pl.pallas_call(
        _mix_kernel,
        grid=(t // tm,),
        in_specs=[row, row, row, row, row, mod_spec, wspec, wspec, wspec],
        out_specs=row,
        out_shape=jax.ShapeDtypeStruct((t, D_MODEL), F32),
        compiler_params=_cp(("parallel",)),
        name="mix",
    )(oa, ob, ga, gb, x, g1, wa, wb, wo)


def _ffn_kernel(*refs, TM, has_hist):
    if has_hist:
        (x_ref, sh_ref, sc_ref, g2_ref, hist_ref, n2_ref, wup_ref, cw_ref, cb_ref, wdn_ref, fw_ref,
         y_ref, u_ref, ubuf) = refs
    else:
        (x_ref, sh_ref, sc_ref, g2_ref, n2_ref, wup_ref, cw_ref, cb_ref, wdn_ref, fw_ref,
         y_ref, u_ref, ubuf) = refs
    i = pl.program_id(0)

    @pl.when(i == 0)
    def _():
        ubuf[0:8, :] = jnp.zeros((8, 2 * D_FF), F32)

    x = x_ref[...]
    h = x * lax.rsqrt(jnp.mean(x * x, -1, keepdims=True) + EPS) * n2_ref[...]
    h = h * (1.0 + sc_ref[...]) + sh_ref[...]
    hb = h.astype(BF16)
    if has_hist:
        is_hist = (lax.broadcasted_iota(I32, (TM, 1), 0) % GROUP) < FIRST_TOK
    CB = 256
    acc = jnp.zeros((TM, D_MODEL), F32)
    def up(c):
        us = []
        for base in (c, D_FF + c):
            cs = slice(base, base + CB)
            u = _mm(hb, wup_ref[:, cs])
            if has_hist:
                u = jnp.where(is_hist, hist_ref[:, cs], u)
            ubuf[8:8 + TM, cs] = u
            us.append(u)
        return us

    starts = list(range(0, D_FF, CB))
    nxt = up(starts[0])
    for n, c in enumerate(starts):
        cur = nxt
        if n + 1 < len(starts):
            nxt = up(starts[n + 1])
        halves = []
        for u, base in zip(cur, (c, D_FF + c)):
            cs = slice(base, base + CB)
            halves.append(cw_ref[2:3, cs] * u + cw_ref[1:2, cs] * ubuf[7:7 + TM, cs]
                          + cw_ref[0:1, cs] * ubuf[6:6 + TM, cs] + cb_ref[:, cs])
        act = _silu(halves[0]) * halves[1]
        acc = acc + _mm(act.astype(BF16), wdn_ref[c:c + CB, :])
    if has_hist:
        u_ref[...] = ubuf[8:8 + TM, :]
    else:
        u_ref[...] = ubuf[TM:TM + 8, :]
    ubuf[0:8, :] = ubuf[TM:TM + 8, :]
    x2 = x + g2_ref[...] * acc
    y_ref[...] = x2 * lax.rsqrt(jnp.mean(x2 * x2, -1, keepdims=True) + EPS) * fw_ref[...]


def ffn(x1, sh, sc, g2, hist, norm2_w, w_up, conv_w, conv_b, w_down, final_w, tm=256):
    t = x1.shape[0]
    tm = min(tm, t)
    has_hist = hist is not None
    per_row = sh.shape[0] != 1
    mod_spec = (pl.BlockSpec((tm, D_MODEL), lambda i: (i, 0)) if per_row
                else pl.BlockSpec((1, D_MODEL), lambda i: (0, 0)))
    row = pl.BlockSpec((tm, D_MODEL), lambda i: (i, 0))
    in_specs = [row, mod_spec, mod_spec, mod_spec]
    args = [x1, sh, sc, g2]
    if has_hist:
        in_specs.append(pl.BlockSpec((tm, 2 * D_FF), lambda i: (i, 0)))
        args.append(hist)
    in_specs += [_const_spec((1, D_MODEL)), _const_spec((D_MODEL, 2 * D_FF)), _const_spec((FFN_CONV, 2 * D_FF)),
                 _const_spec((1, 2 * D_FF)), _const_spec((D_FF, D_MODEL)), _const_spec((1, D_MODEL))]
    args += [norm2_w, w_up, conv_w, conv_b, w_down, final_w]
    if has_hist:
        u_spec = pl.BlockSpec((tm, 2 * D_FF), lambda i: (i, 0))
        u_shape = jax.ShapeDtypeStruct((t, 2 * D_FF), F32)
    else:
        u_spec = pl.BlockSpec((8, 2 * D_FF), lambda i: (0, 0))
        u_shape = jax.ShapeDtypeStruct((8, 2 * D_FF), F32)
    return pl.pallas_call(
        functools.partial(_ffn_kernel, TM=tm, has_hist=has_hist),
        grid=(t // tm,),
        in_specs=in_specs,
        out_specs=[row, u_spec],
        out_shape=[jax.ShapeDtypeStruct((t, D_MODEL), F32), u_shape],
        scratch_shapes=[pltpu.VMEM((8 + tm, 2 * D_FF), F32)],
        compiler_params=_cp(("arbitrary",)),
        name="ffn",
    )(*args)


def _group_rows(a, first):
    b, n, c = a.shape
    return jnp.pad(a, ((0, 0), (first, GROUP - first - n), (0, 0))).reshape(b * GROUP, c)


def kernel(x_prompt, x_sample, cache_k, cache_v, cache_idx_k, state_gdn, state_gdn_conv, state_ffn_conv,
           page_table, c_prompt, c_sample, w_ada, b_ada, norm1_w, w_in, gdn_conv_w, gdn_A_log, gdn_dt_bias,
           gdn_norm_w, idx_knorm_w, idx_knorm_b, w_branch_a, w_branch_b, w_out, norm2_w, w_up, ffn_conv_w,
           ffn_conv_b, w_down, rel_bias, final_norm_w):
    bp, tp, _ = x_prompt.shape
    bs, ts, _ = x_sample.shape
    assert bp == 1 and ts == GROUP - FIRST_TOK and w_ada.shape[0] == 1
    npg = page_table.shape[1]
    past = npg * PAGE

    w = w_in[0]
    w_cat = jnp.concatenate(
        [w[:, 0:4096], w[:, 4112:7184], w[:, 7768:9816], w[:, 7184:7696],
         w[:, 7696:7760], w[:, 4096:4112], w[:, 7760:7768], jnp.zeros((D_MODEL, 40), F32),
         jnp.zeros((D_MODEL, 64), F32), w[:, 7696:7760]], axis=1).astype(BF16)
    w_smt = w_cat[:, C_SM:C_SM2].T
    z64 = jnp.zeros((IDX_DIM,), F32)
    lnw2 = jnp.stack([jnp.concatenate([idx_knorm_w[0], z64]), jnp.concatenate([z64, idx_knorm_w[0]])])
    lnb2 = jnp.stack([jnp.concatenate([idx_knorm_b[0], z64]), jnp.concatenate([z64, idx_knorm_b[0]])])
    misc = jnp.zeros((8, 128), F32)
    misc = misc.at[0, L_A:L_A + N_HEADS].set(gdn_A_log[0]).at[1, L_A:L_A + N_HEADS].set(gdn_dt_bias[0])
    wa16, wb16, wo16 = w_branch_a[0].astype(BF16), w_branch_b[0].astype(BF16), w_out[0].astype(BF16)
    wup16, wdn16 = w_up[0].astype(BF16), w_down[0].astype(BF16)
    n1 = norm1_w[0].reshape(1, D_MODEL)
    n2 = norm2_w[0].reshape(1, D_MODEL)
    fw = final_norm_w.reshape(1, D_MODEL)
    gnw = gdn_norm_w[0].reshape(1, 128)
    ffn_b = ffn_conv_b[0].reshape(1, 2 * D_FF)

    c_all = jnp.concatenate([c_prompt, c_sample], axis=0)
    pad_r = (-c_all.shape[0]) % 8
    mod = ada_mod(jnp.pad(c_all, ((0, pad_r), (0, 0))), w_ada[0], b_ada[0])
    mod_p = mod[0:1]
    mod_s = jnp.repeat(mod[1:1 + bs], GROUP, axis=0)

    def mods(m):
        return [m[:, i * D_MODEL:(i + 1) * D_MODEL] for i in range(6)]

    e_tab, b_last, b_tail = bias_tables(rel_bias)

    xp = x_prompt[0]
    sh1, sc1, g1, sh2, sc2, g2 = mods(mod_p)
    (conv_p, gz_p, q_p, kf_p, vf_p, k16_p, v16_p, ga_p, gb_p, qi_p, small_p, ki_p, kb0_p, kb1_p, smt_p) = in_proj(
        xp, sh1, sc1, n1, w_cat, w_smt, lnw2, lnb2)
    oa_p, s_p = gdn(conv_p, None, small_p, smt_p, gz_p, jnp.zeros((1, N_HEADS, 128, 128), F32), gdn_conv_w[0],
                    misc, gnw, nseq=1, rows=min(4 * GDN_CHUNK, tp), out_dtype=BF16)
    mask_p = prompt_topk_mask(qi_p, small_p, kb0_p, kb1_p, min(TOPK_MAX, tp // 4))
    ob_p = prompt_attention(q_p, k16_p, v16_p, mask_p, e_tab)
    x1_p = mix(oa_p, ob_p, ga_p, gb_p, xp, g1, wa16, wb16, wo16)
    y_p, utail_p = ffn(x1_p, sh2, sc2, g2, None, n2, wup16, ffn_conv_w[0], ffn_b, wdn16, fw)

    xs = _group_rows(x_sample, FIRST_TOK)
    sh1, sc1, g1, sh2, sc2, g2 = mods(mod_s)
    (conv_s, gz_s, q_s, kf_s, vf_s, _, _, ga_s, gb_s, qi_s, small_s, ki_s, _, _, _) = in_proj(
        xs, sh1, sc1, n1, w_cat, w_smt, lnw2, lnb2)
    hist_gdn = _group_rows(state_gdn_conv[0], FIRST_TOK - (GDN_CONV - 1))
    oa_s, s_s = gdn(conv_s, hist_gdn, small_s, None, gz_s, state_gdn[0], gdn_conv_w[0], misc, gnw,
                    nseq=bs, rows=GROUP, out_dtype=F32)
    qm = qi_s.reshape(bs, GROUP, IDX_HEADS, IDX_DIM).transpose(0, 2, 1, 3).reshape(bs, IDX_HEADS * GROUP, IDX_DIM)
    wcol = (small_s[:, L_WI:L_WI + IDX_HEADS] * IDX_W_SCALE).reshape(bs, GROUP, IDX_HEADS)
    wcol = wcol.transpose(0, 2, 1).reshape(bs, IDX_HEADS * GROUP, 1)
    scores = sample_scores(page_table, qm, wcol, ki_s, cache_idx_k)
    sc_tok = scores[:, FIRST_TOK:, :].reshape(bs * ts, past + 128)
    sel = sample_topk_mask(sc_tok, min(TOPK_MAX, (past + ts) // 4))
    sel = jnp.pad(sel.reshape(bs, ts, past + 128), ((0, 0), (FIRST_TOK, 0), (0, 0)), constant_values=1.0)
    selp = sel[:, :, :past].reshape(bs, GROUP, npg, PAGE).transpose(0, 2, 1, 3)
    heads = jnp.arange(N_HEADS)
    ok = (sel[:, None, :, past:past + GROUP, None] > 0.5) & (heads[:, None] == heads[None, :])[None, :, None, None, :]
    bt = b_tail.reshape(N_HEADS, GROUP, 128)[None, :, :, :GROUP, None]
    tmask = jnp.where(ok, bt, NEG_INF).reshape(bs, N_HEADS * GROUP, GROUP * N_HEADS)
    tmask = jnp.pad(tmask, ((0, 0), (0, 0), (0, 128 - GROUP * N_HEADS)), constant_values=NEG_INF)
    nx = min(8, npg) * PAGE * N_HEADS
    hp = jnp.asarray(np.where((np.arange(nx)[None, :] & 7) == (np.arange(N_HEADS * GROUP)[:, None] >> 3),
                              0.0, NEG_INF).astype(np.float32))
    blx = jnp.repeat(b_last, N_HEADS, axis=1)
    q64 = q_s.reshape(bs, GROUP, N_HEADS, HEAD_DIM).transpose(0, 2, 1, 3).reshape(bs, N_HEADS * GROUP, HEAD_DIM)
    ob_s = sample_attention(page_table, q64, selp, kf_s.reshape(bs, GROUP * N_HEADS, HEAD_DIM),
                            vf_s.reshape(bs, GROUP * N_HEADS, HEAD_DIM), tmask, hp, blx, cache_k, cache_v)
    x1_s = mix(oa_s, ob_s, ga_s, gb_s, xs, g1, wa16, wb16, wo16)
    hist_ffn = _group_rows(state_ffn_conv[0], FIRST_TOK - (FFN_CONV - 1))
    y_s, u_s = ffn(x1_s, sh2, sc2, g2, hist_ffn, n2, wup16, ffn_conv_w[0], ffn_b, wdn16, fw)

    def tok(a):
        return a.reshape(bs, GROUP, -1)[:, FIRST_TOK:, :]

    hd = (N_HEADS, HEAD_DIM)
    return (y_p[None], tok(y_s),
            kf_p.reshape((1, 1, tp) + hd), vf_p.reshape((1, 1, tp) + hd), ki_p.reshape(1, 1, tp, IDX_DIM),
            tok(kf_s).reshape((1, bs, ts) + hd), tok(vf_s).reshape((1, bs, ts) + hd), tok(ki_s)[None],
            s_p[None], s_s[None],
            conv_p[tp - (GDN_CONV - 1):][None, None],
            conv_s.reshape(bs, GROUP, -1)[:, GROUP - (GDN_CONV - 1):][None],
            utail_p[8 - (FFN_CONV - 1):][None, None],
            u_s.reshape(bs, GROUP, -1)[:, GROUP - (FFN_CONV - 1):][None])
```

```python
import functools
import math

import jax
import jax.numpy as jnp
import numpy as np
from jax import lax
from jax.experimental import pallas as pl
from jax.experimental.pallas import tpu as pltpu

F32 = jnp.float32
BF16 = jnp.bfloat16
I32 = jnp.int32
HI = lax.Precision.HIGHEST

D_MODEL = 1024
N_HEADS = 8
HEAD_DIM = 128
IDX_HEADS = 8
IDX_DIM = 64
TOPK_MAX = 256
D_FF = 2816
GDN_CONV = 4
FFN_CONV = 3
GDN_CHUNK = 64
PAGE = 128
REL_BUCKETS = 32
REL_MAX_DIST = 128
EPS = 1e-6
IDX_W_SCALE = (IDX_HEADS * IDX_DIM) ** -0.5
ATT_SCALE = HEAD_DIM ** -0.5

C_CONV, C_GZ, C_Q, C_K, C_V, C_GA, C_GB, C_QI, C_SM, C_SM2, C_END = (
    0, 3072, 4096, 5120, 6144, 7168, 8192, 9216, 9728, 9856, 9984)
L_KI, L_B, L_A, L_WI = 0, 64, 72, 80

GROUP = 8
FIRST_TOK = 4
NEG_INF = float("-inf")
KEY_NEG_INF = -2139095041
VMEM_LIMIT = 56 * 1024 * 1024


def _cp(sem, vmem=VMEM_LIMIT):
    return pltpu.CompilerParams(dimension_semantics=sem, vmem_limit_bytes=vmem)


def _const_spec(shape):
    nd = len(shape)
    return pl.BlockSpec(shape, lambda *a: (0,) * nd, pipeline_mode=pl.Buffered(1))


def _sigmoid(x):
    return 1.0 / (1.0 + jnp.exp(-x))


def _silu(x):
    return x * _sigmoid(x)


def _softplus(x):
    return jnp.maximum(x, 0.0) + jnp.log(1.0 + jnp.exp(-jnp.abs(x)))


def _nt(a, b, precision=None):
    return lax.dot_general(a, b, (((1,), (1,)), ((), ())), precision=precision, preferred_element_type=F32)


def _tn(a, b, precision=None):
    return lax.dot_general(a, b, (((0,), (0,)), ((), ())), precision=precision, preferred_element_type=F32)


def _mm(a, b, precision=None):
    return jnp.dot(a, b, precision=precision, preferred_element_type=F32)


def _ada_kernel(c_ref, w_ref, b_ref, o_ref):
    c = c_ref[...]
    o_ref[...] = _mm(_silu(c), w_ref[...], HI) + b_ref[...]


def ada_mod(c_all, w_ada, b_ada):
    r, d = c_all.shape
    n = w_ada.shape[1]
    tn = 1024
    return pl.pallas_call(
        _ada_kernel,
        grid=(n // tn,),
        in_specs=[pl.BlockSpec((r, d), lambda j: (0, 0)),
                  pl.BlockSpec((d, tn), lambda j: (0, j)),
                  pl.BlockSpec((1, tn), lambda j: (0, j))],
        out_specs=pl.BlockSpec((r, tn), lambda j: (0, j)),
        out_shape=jax.ShapeDtypeStruct((r, n), F32),
        compiler_params=_cp(("parallel",)),
        name="ada_mod",
    )(c_all, w_ada, b_ada.reshape(1, n))


def _rel_bucket(d):
    n = jnp.maximum(d, 0)
    max_exact = REL_BUCKETS // 2
    nf = jnp.maximum(n, max_exact).astype(F32)
    large = max_exact + (jnp.log(nf / max_exact) / math.log(REL_MAX_DIST / max_exact)
                         * (REL_BUCKETS - max_exact)).astype(I32)
    large = jnp.minimum(large, REL_BUCKETS - 1)
    return jnp.where(n < max_exact, n, large)


def _bias_kernel(rb_ref, e_ref, bl_ref, bt_ref):
    def table(d, h):
        bk = _rel_bucket(d)
        acc = jnp.zeros(d.shape, F32)
        for b in range(REL_BUCKETS):
            acc = acc + jnp.where(bk == b, rb_ref[b, h], 0.0)
        return acc - rb_ref[REL_BUCKETS - 1, h]

    a = lax.broadcasted_iota(I32, (128, 128), 0)
    b = lax.broadcasted_iota(I32, (128, 128), 1)
    r = lax.broadcasted_iota(I32, (GROUP, 128), 0)
    c = lax.broadcasted_iota(I32, (GROUP, 128), 1)
    for h in range(N_HEADS):
        e_ref[0, h] = table(a - b, h)
        e_ref[1, h] = table(a - b + 128, h)
        bl_ref[h * GROUP:(h + 1) * GROUP, :] = table(128 + r - FIRST_TOK - c, h)
        bt_ref[h * GROUP:(h + 1) * GROUP, :] = table(r - c, h)


def bias_tables(rel_bias):
    return pl.pallas_call(
        _bias_kernel,
        in_specs=[pl.BlockSpec(memory_space=pltpu.SMEM)],
        out_shape=(jax.ShapeDtypeStruct((2, N_HEADS, 128, 128), F32),
                   jax.ShapeDtypeStruct((N_HEADS * GROUP, 128), F32),
                   jax.ShapeDtypeStruct((N_HEADS * GROUP, 128), F32)),
        name="bias_tables",
    )(rel_bias)


def _in_kernel(x_ref, sh_ref, sc_ref, nw_ref, w_ref, wt_ref, lnw_ref, lnb_ref,
               conv_ref, gz_ref, q_ref, kf_ref, vf_ref, k16_ref, v16_ref, ga_ref, gb_ref, qi_ref,
               small_ref, ki_ref, kb0_ref, kb1_ref, smt_ref):
    x = x_ref[...]
    h = x * lax.rsqrt(jnp.mean(x * x, -1, keepdims=True) + EPS) * nw_ref[...]
    h = h * (1.0 + sc_ref[...]) + sh_ref[...]
    hb = h.astype(BF16)

    def proj(a, b):
        return _mm(hb, w_ref[:, a:b])

    for c in range(0, 3072, 1024):
        conv_ref[:, c:c + 1024] = proj(C_CONV + c, C_CONV + c + 1024)
    gz_ref[...] = proj(C_GZ, C_Q)
    q_ref[...] = (proj(C_Q, C_K) * ATT_SCALE).astype(BF16)
    k = proj(C_K, C_V)
    kf_ref[...] = k
    k16_ref[...] = k.astype(BF16)
    v = proj(C_V, C_GA)
    vf_ref[...] = v
    v16_ref[...] = v.astype(BF16)
    ga_ref[...] = _sigmoid(proj(C_GA, C_GB))
    gb_ref[...] = _sigmoid(proj(C_GB, C_QI))
    qi_ref[...] = proj(C_QI, C_SM).astype(BF16)
    z = proj(C_SM, C_SM2)
    small_ref[...] = z
    smt_ref[...] = _nt(wt_ref[...], hb)
    lane = lax.broadcasted_iota(I32, z.shape, 1)
    for zz, lo, out_ref in ((z, True, kb0_ref), (proj(C_SM2, C_END), False, kb1_ref)):
        m = (lane < IDX_DIM) if lo else (lane >= IDX_DIM)
        zk = jnp.where(m, zz, 0.0)
        mean = jnp.sum(zk, -1, keepdims=True) * (1.0 / IDX_DIM)
        xc = jnp.where(m, zz - mean, 0.0)
        var = jnp.sum(xc * xc, -1, keepdims=True) * (1.0 / IDX_DIM)
        row = 0 if lo else 1
        kn = xc * lax.rsqrt(var + EPS) * lnw_ref[row:row + 1, :] + lnb_ref[row:row + 1, :]
        out_ref[...] = kn.astype(BF16)
        if lo:
            ki_ref[...] = kn[:, :IDX_DIM]


def in_proj(x, sh, sc, norm_w, w_cat, w_smt, lnw2, lnb2, tm=256):
    t = x.shape[0]
    tm = min(tm, t)
    per_row = sh.shape[0] != 1
    mod_spec = (pl.BlockSpec((tm, D_MODEL), lambda i: (i, 0)) if per_row
                else pl.BlockSpec((1, D_MODEL), lambda i: (0, 0)))

    def rows(n, dt):
        return pl.BlockSpec((tm, n), lambda i: (i, 0)), jax.ShapeDtypeStruct((t, n), dt)

    outs = [rows(3072, F32), rows(1024, F32), rows(1024, BF16), rows(1024, F32), rows(1024, F32),
            rows(1024, BF16), rows(1024, BF16), rows(1024, F32), rows(1024, F32), rows(512, BF16),
            rows(128, F32), rows(IDX_DIM, F32), rows(128, BF16), rows(128, BF16),
            (pl.BlockSpec((128, tm), lambda i: (0, i)), jax.ShapeDtypeStruct((128, t), F32))]
    return pl.pallas_call(
        _in_kernel,
        grid=(t // tm,),
        in_specs=[pl.BlockSpec((tm, D_MODEL), lambda i: (i, 0)), mod_spec, mod_spec,
                  _const_spec((1, D_MODEL)), _const_spec((D_MODEL, C_END)), _const_spec((128, D_MODEL)),
                  _const_spec((2, 128)), _const_spec((2, 128))],
        out_specs=[o[0] for o in outs],
        out_shape=[o[1] for o in outs],
        compiler_params=_cp(("parallel",)),
        name="in_proj",
    )(x, sh, sc, norm_w, w_cat, w_smt, lnw2, lnb2)


def _b16(a):
    return a.astype(BF16)


def _mmb(a, b):
    return jnp.dot(_b16(a), _b16(b), preferred_element_type=F32)


def _split(a):
    hi = _b16(a)
    return hi, _b16(a - hi.astype(F32))


def _mm3(a, b):
    ah, al = a
    bh, bl = b
    return (jnp.dot(ah, bh, preferred_element_type=F32) + jnp.dot(ah, bl, preferred_element_type=F32)
            + jnp.dot(al, bh, preferred_element_type=F32))


def _gdn_kernel(*refs, R, C, N, has_hist):
    s_scr = refs[-N_HEADS:]
    refs = refs[:-N_HEADS]
    if has_hist:
        conv_ref, hist_ref, small_ref, gz_ref, s0_ref, cw_ref, misc_ref, nw_ref, o_ref, sout_ref, xbuf = refs
    else:
        (conv_ref, small_ref, smt_ref, at_ref, dtt_ref, gz_ref, s0_ref, cw_ref, misc_ref, nw_ref,
         o_ref, sout_ref, xbuf) = refs
    nch = max(R // C, 1)
    RC = nch * C
    shift = C.bit_length() - 1
    n = pl.program_id(1)

    @pl.when(n == 0)
    def _():
        xbuf[...] = jnp.zeros_like(xbuf)
        for h in range(N_HEADS):
            s_scr[h][...] = s0_ref[0, h]

    x = conv_ref[...]
    validf = None
    if has_hist:
        rows_r = lax.broadcasted_iota(I32, (R, 1), 0)
        x = jnp.where((rows_r % GROUP) < FIRST_TOK, hist_ref[...], x)
        rows_c = lax.broadcasted_iota(I32, (RC, 1), 0)
        validf = jnp.where((rows_c < R) & ((rows_c % GROUP) >= FIRST_TOK), 1.0, 0.0)
    xbuf[8:8 + R, :] = x

    small = small_ref[...]
    if R < RC:
        small = jnp.concatenate([small, jnp.zeros((RC - R, 128), F32)], axis=0)
    misc = misc_ref[...]
    gfull = -jnp.exp(misc[0:1]) * _softplus(small + misc[1:2])
    betaf = _sigmoid(small)
    if validf is not None:
        gfull = gfull * validf
        betaf = betaf * validf
    ri = lax.broadcasted_iota(I32, (RC, RC), 0)
    ci = lax.broadcasted_iota(I32, (RC, RC), 1)
    same = (ri >> shift) == (ci >> shift)
    gcol = _mm(jnp.where(same & (ri >= ci), 1.0, 0.0), gfull, HI)
    if not has_hist:
        gt = -jnp.exp(at_ref[...]) * _softplus(smt_ref[...] + dtt_ref[...])
        grow_all = _mm(gt[L_A:L_A + N_HEADS, :], jnp.where(same & (ri <= ci), 1.0, 0.0), HI)
    rc = lax.broadcasted_iota(I32, (C, C), 0)
    cc = lax.broadcasted_iota(I32, (C, C), 1)
    tril = rc >= cc
    strict = rc > cc
    eye = rc == cc
    ones_cc = jnp.ones((C, C), F32)
    w = cw_ref[...]

    pre = []
    zrows = jnp.zeros((128 - C, 128), F32)
    for c in range(nch):
        r0 = c * C
        y = (w[3:4] * xbuf[8 + r0:8 + r0 + C, :] + w[2:3] * xbuf[7 + r0:7 + r0 + C, :]
             + w[1:2] * xbuf[6 + r0:6 + r0 + C, :] + w[0:1] * xbuf[5 + r0:5 + r0 + C, :])
        y = _silu(y)
        vf = None if validf is None else validf[r0:r0 + C]
        for h in range(N_HEADS):
            q = y[:, h * 128:(h + 1) * 128]
            k = y[:, 1024 + h * 128:1024 + (h + 1) * 128]
            v = y[:, 2048 + h * 128:2048 + (h + 1) * 128]
            q = q * lax.rsqrt(jnp.sum(q * q, -1, keepdims=True) + EPS) * (HEAD_DIM ** -0.5)
            k = k * lax.rsqrt(jnp.sum(k * k, -1, keepdims=True) + EPS)
            if vf is not None:
                k = k * vf
                v = v * vf
            bc = jnp.broadcast_to(betaf[r0:r0 + C, L_B + h:L_B + h + 1], (C, 128))
            gc = jnp.broadcast_to(gcol[r0:r0 + C, L_A + h:L_A + h + 1], (C, 128))
            gb = gc[:, :C]
            if has_hist:
                grow = _mm(ones_cc, jnp.where(eye, gb, 0.0), HI)
            else:
                grow = jnp.broadcast_to(grow_all[h:h + 1, r0:r0 + C], (C, C))
            dec = jnp.where(tril, jnp.exp(jnp.minimum(gb - grow, 0.0)), 0.0)
            eg = jnp.exp(gc)
            glast = gc[C - 1:C, :]
            kd = k * jnp.exp(glast - gc)
            kdt = jnp.concatenate([kd, zrows], axis=0).T[:, :C]
            pre.append((q, k, v, bc, dec, eg, jnp.exp(glast), _b16(kdt)))

    nprob = nch * N_HEADS
    a_l = []
    for q, k, v, bc, dec, eg, ge, kdt in pre:
        kh, kl = _split(k)
        a_l.append(jnp.where(strict, bc[:, :C] * (_nt(kh, kh) + _nt(kh, kl) + _nt(kl, kh)) * dec, 0.0))
    xo_l = [-a for a in a_l]
    pw_l = []
    for a in a_l:
        a_s = _split(a)
        pw_l.append(_mm3(a_s, a_s))
    span = 2
    while span < C:
        span *= 2
        pws_l = [_split(pw) for pw in pw_l]
        xo_l = [xo + pw + _mm3(_split(xo), pws) for xo, pw, pws in zip(xo_l, pw_l, pws_l)]
        if span < C:
            pw_l = [_mm3(pws, pws) for pws in pws_l]
    um_l, wm_l, qk_l = [], [], []
    for (q, k, v, bc, dec, eg, ge, kdt), xo in zip(pre, xo_l):
        rv = v * bc
        rk = k * (bc * eg)
        xo_s = _split(xo)
        um_l.append(rv + _mm3(xo_s, _split(rv)))
        wm_l.append(rk + _mm3(xo_s, _split(rk)))
        qk_l.append(_nt(_b16(q), _b16(k)) * dec)

    outs = [None] * nprob
    for c in range(nch):
        probs = range(c * N_HEADS, (c + 1) * N_HEADS)
        s_l = [s_scr[h][...] for h in range(N_HEADS)]
        sb_l = [_b16(s) for s in s_l]
        u_l = [um_l[p] - _mm(_b16(wm_l[p]), sb) for p, sb in zip(probs, sb_l)]
        for h, p in enumerate(probs):
            q, k, v, bc, dec, eg, ge, kdt = pre[p]
            outs[p] = _mm(_b16(q * eg), sb_l[h]) + _mmb(qk_l[p], u_l[h])
            s_scr[h][...] = s_l[h] * ge + _mm(kdt, _b16(u_l[h]))

    ro = min(R, C)
    for c in range(nch):
        r0 = c * C
        for h in range(N_HEADS):
            o = outs[c * N_HEADS + h][:ro]
            on = o * lax.rsqrt(jnp.mean(o * o, -1, keepdims=True) + EPS) * nw_ref[...]
            gz = gz_ref[r0:r0 + ro, h * 128:(h + 1) * 128]
            o_ref[r0:r0 + ro, h * 128:(h + 1) * 128] = (on * _silu(gz)).astype(o_ref.dtype)

    if N > 1:
        xbuf[0:8, :] = xbuf[RC:RC + 8, :]

    @pl.when(n == N - 1)
    def _():
        for h in range(N_HEADS):
            sout_ref[0, h] = s_scr[h][...]


def gdn(conv_in, hist, small, smt, gz, s0, conv_w, misc, norm_w, *, nseq, rows, out_dtype):
    C = GDN_CHUNK
    t = conv_in.shape[0]
    nstep = t // (nseq * rows)
    has_hist = hist is not None
    rc = max(rows // C, 1) * C

    def rspec(n):
        return pl.BlockSpec((rows, n), lambda b, c: (b * nstep + c, 0))

    def cspec(shape):
        return pl.BlockSpec(shape, lambda b, c: (0,) * len(shape))

    if has_hist:
        in_specs = [rspec(3072), rspec(3072), rspec(128), rspec(1024)]
        args = [conv_in, hist, small, gz]
    else:
        lane_rep = jnp.ones((1, rows), F32)
        at = misc[0].reshape(128, 1) * lane_rep
        dtt = misc[1].reshape(128, 1) * lane_rep
        in_specs = [rspec(3072), rspec(128), pl.BlockSpec((128, rows), lambda b, c: (0, b * nstep + c)),
                    cspec((128, rows)), cspec((128, rows)), rspec(1024)]
        args = [conv_in, small, smt, at, dtt, gz]
    in_specs += [pl.BlockSpec((1, N_HEADS, 128, 128), lambda b, c: (b, 0, 0, 0)),
                 cspec((GDN_CONV, 3072)), cspec((8, 128)), cspec((1, 128))]
    args += [s0, conv_w, misc, norm_w]
    return pl.pallas_call(
        functools.partial(_gdn_kernel, R=rows, C=C, N=nstep, has_hist=has_hist),
        grid=(nseq, nstep),
        in_specs=in_specs,
        out_specs=[rspec(1024), pl.BlockSpec((1, N_HEADS, 128, 128), lambda b, c: (b, 0, 0, 0))],
        out_shape=[jax.ShapeDtypeStruct((t, 1024), out_dtype),
                   jax.ShapeDtypeStruct((nseq, N_HEADS, 128, 128), F32)],
        scratch_shapes=[pltpu.VMEM((8 + rc, 3072), F32)] + [pltpu.VMEM((128, 128), F32)] * N_HEADS,
        compiler_params=_cp(("arbitrary", "arbitrary")),
        name="gdn",
    )(*args)


def _sort_key(s, col=None):
    bits = pltpu.bitcast(s + 0.0, I32)
    key = bits ^ ((bits >> 31) & 0x7FFFFFFF)
    if col is None:
        return key
    return jnp.where(s == 0.0, -1 - col, key)


def _count_ge(keys_ref, nkt, cand, tq, tk):
    rb = min(128, tq)
    cand_b = jnp.broadcast_to(cand, (tq, 128))
    accs = []
    for r0 in range(0, tq, rb):
        def body(j, acc, r0=r0, cr=cand_b[r0:r0 + rb]):
            kt = keys_ref[r0:r0 + rb, pl.ds(pl.multiple_of(j * tk, tk), tk)]
            for g in range(tk // 128):
                acc = acc + jnp.where(kt[:, g * 128:(g + 1) * 128] >= cr, 1, 0)
            return acc

        accs.append(lax.fori_loop(0, nkt, body, jnp.zeros((rb, 128), I32)))
    acc = accs[0] if len(accs) == 1 else jnp.concatenate(accs, axis=0)
    return jnp.sum(acc, axis=-1, keepdims=True)


def _kth_largest(keys_ref, nkt, rmax, tq, tk, k):
    hi = _sort_key(rmax) + 1
    lo0 = _sort_key(jnp.where(rmax >= 0.0, rmax * 0.125, rmax * 8.0))
    c0 = _count_ge(keys_ref, nkt, lo0, tq, tk)
    ok0 = c0 >= k
    lo = jnp.where(ok0, lo0, KEY_NEG_INF - 1)
    cnt = jnp.where(ok0, c0, nkt * tk)
    hi = jnp.where(ok0, hi, lo0)
    c_pos = _count_ge(keys_ref, nkt, jnp.ones((tq, 1), I32), tq, tk)
    up = (c_pos >= k) & (lo < 1)
    lo = jnp.where(up, 1, lo)
    cnt = jnp.where(up, c_pos, cnt)
    hi = jnp.where((c_pos < k) & (hi > 1), 1, hi)

    def n_active(lo, hi, cnt):
        return jnp.max(jnp.where((cnt != k) & (hi - lo > 1), 1, 0))

    def cond(st):
        return st[3] > 0

    def body(st):
        lo, hi, cnt, _ = st
        mid = lo + ((hi - lo) >> 1)
        c = _count_ge(keys_ref, nkt, mid, tq, tk)
        up = c >= k
        lo = jnp.where(up, mid, lo)
        cnt = jnp.where(up, c, cnt)
        hi = jnp.where(up, hi, mid)
        return lo, hi, cnt, n_active(lo, hi, cnt)

    lo, _, cnt, _ = lax.while_loop(cond, body, (lo, hi, cnt, n_active(lo, hi, cnt)))
    return lo, cnt


def _emit_selection(keys_ref, nkt, thr, cnt, tq, tk, k, emit):
    tied = (cnt > k) & (thr > KEY_NEG_INF)
    any_tied = jnp.max(jnp.where(tied, 1, 0))

    @pl.when(any_tied == 0)
    def _():
        def body(j, carry):
            kt = keys_ref[:, pl.ds(pl.multiple_of(j * tk, tk), tk)]
            emit(j, kt >= thr)
            return carry
        lax.fori_loop(0, nkt, body, 0)

    @pl.when(any_tied != 0)
    def _():
        c_gt = _count_ge(keys_ref, nkt, thr + 1, tq, tk)
        budget = jnp.where(tied, k - c_gt, nkt * tk).astype(F32)
        ri = lax.broadcasted_iota(I32, (tk, tk), 0)
        ci = lax.broadcasted_iota(I32, (tk, tk), 1)
        upper = jnp.where(ri <= ci, 1.0, 0.0).astype(BF16)

        def body(j, seen):
            kt = keys_ref[:, pl.ds(pl.multiple_of(j * tk, tk), tk)]
            eq = jnp.where(kt == thr, 1.0, 0.0)
            rank = _mm(eq.astype(BF16), upper) + seen
            sel = jnp.where(kt > thr, 1.0, jnp.where(rank <= budget, eq, 0.0))
            emit(j, sel > 0.5)
            return seen + jnp.sum(eq, axis=-1, keepdims=True)
        lax.fori_loop(0, nkt, body, jnp.zeros((tq, 1), F32))


def _idx_kernel(qi_ref, small_ref, kb0_ref, kb1_ref, mask_ref, keys_ref, *, TQ, TK, T, k):
    i = pl.program_id(0)
    nkt = (i * TQ + TQ + TK - 1) // TK
    w = small_ref[:, L_WI:L_WI + IDX_HEADS] * IDX_W_SCALE
    pos_q = i * TQ + lax.broadcasted_iota(I32, (TQ, 1), 0)

    def score_tile(j, rmax):
        off = pl.multiple_of(j * TK, TK)
        ka = kb0_ref[pl.ds(off, TK), :]
        kb = kb1_ref[pl.ds(off, TK), :]
        acc = jnp.zeros((TQ, TK), F32)
        for p in range(IDX_HEADS // 2):
            q2 = qi_ref[:, p * 128:(p + 1) * 128]
            acc = acc + w[:, 2 * p:2 * p + 1] * jnp.maximum(_nt(q2, ka), 0.0)
            acc = acc + w[:, 2 * p + 1:2 * p + 2] * jnp.maximum(_nt(q2, kb), 0.0)
        pos_k = off + lax.broadcasted_iota(I32, (1, TK), 1)
        acc = jnp.where(pos_k <= pos_q, acc, NEG_INF)
        keys_ref[:, pl.ds(off, TK)] = _sort_key(acc, pos_k)
        for g in range(TK // 128):
            rmax = jnp.maximum(rmax, acc[:, g * 128:(g + 1) * 128])
        return rmax

    rmax = lax.fori_loop(0, nkt, score_tile, jnp.full((TQ, 128), NEG_INF, F32))
    thr, cnt = _kth_largest(keys_ref, nkt, jnp.max(rmax, axis=-1, keepdims=True), TQ, TK, k)

    def emit(j, sel):
        off = pl.multiple_of(j * TK, TK)
        pos_k = off + lax.broadcasted_iota(I32, (1, TK), 1)
        m = jnp.where(pos_k <= pos_q, jnp.where(sel, 1, 0), 0)
        mask_ref[:, pl.ds(off, TK)] = m.astype(jnp.int8)

    _emit_selection(keys_ref, nkt, thr, cnt, TQ, TK, k, emit)

    def zero_tile(j, carry):
        mask_ref[:, pl.ds(pl.multiple_of(j * TK, TK), TK)] = jnp.zeros((TQ, TK), jnp.int8)
        return carry

    lax.fori_loop(nkt, T // TK, zero_tile, 0)


def prompt_topk_mask(qi16, small, kb0, kb1, k, TQ=256, TK=512):
    t = qi16.shape[0]
    TK = min(TK, t)
    TQ = min(TQ, t)
    return pl.pallas_call(
        functools.partial(_idx_kernel, TQ=TQ, TK=TK, T=t, k=k),
        grid=(t // TQ,),
        in_specs=[pl.BlockSpec((TQ, 512), lambda i: (i, 0)),
                  pl.BlockSpec((TQ, 128), lambda i: (i, 0)),
                  _const_spec((t, 128)), _const_spec((t, 128))],
        out_specs=pl.BlockSpec((TQ, t), lambda i: (i, 0)),
        out_shape=jax.ShapeDtypeStruct((t, t), jnp.int8),
        scratch_shapes=[pltpu.VMEM((TQ, t), I32)],
        compiler_params=_cp(("parallel",)),
        name="prompt_topk_mask",
    )(qi16, small, kb0, kb1)


def _att_kernel(qa_ref, ka_ref, q_ref, k_ref, v_ref, mask_ref, e_ref, o_ref, *scr, TA, RS):
    m_scr, l_scr, acc_scr = scr[0:8], scr[8:16], scr[16:24]
    s_scr, p_scr, b_scr = scr[24:26], scr[26:28], scr[28]
    p = pl.program_id(0)
    i = qa_ref[p]
    j = ka_ref[p]

    @pl.when(j == 0)
    def _():
        for h in range(N_HEADS):
            m_scr[h][...] = jnp.full((TA, 1), -1e30, F32)
            l_scr[h][...] = jnp.zeros((TA, HEAD_DIM), F32)
            acc_scr[h][...] = jnp.zeros((TA, HEAD_DIM), F32)

    for r0 in range(0, TA, 32):
        b_scr[r0:r0 + 32, :] = jnp.where(mask_ref[r0:r0 + 32, :].astype(I32) != 0, 0.0, NEG_INF)

    def update(near):
        nb = TA // 128
        if near:
            fd = jnp.where(j == i, 1.0, 0.0)
        def logits(h):
            hs = slice(h * 128, (h + 1) * 128)
            sb = s_scr[h % 2]
            sb[...] = _nt(q_ref[:, hs], k_ref[:, hs]) + b_scr[...]
            if near:
                for b in range(nb):
                    bs = slice(b * 128, (b + 1) * 128)
                    sb[bs, bs] += fd * e_ref[0, h]
                    if b + 1 < nb:
                        sb[(b + 1) * 128:(b + 2) * 128, bs] += fd * e_ref[1, h]
                sb[0:128, TA - 128:TA] += (1.0 - fd) * e_ref[1, h]

        def row_max(h):
            m_all = m_scr[h][...]
            return m_all, [jnp.maximum(m_all[r0:r0 + RS], jnp.max(s_scr[h % 2][r0:r0 + RS, :], axis=-1, keepdims=True))
                           for r0 in range(0, TA, RS)]

        def probs(h, m_parts):
            for n, r0 in enumerate(range(0, TA, RS)):
                rs = slice(r0, r0 + RS)
                p_scr[h % 2][rs, :] = jnp.exp(s_scr[h % 2][rs, :] - m_parts[n]).astype(BF16)

        def accumulate(h, m_all, m_parts):
            hs = slice(h * 128, (h + 1) * 128)
            m_new = jnp.concatenate(m_parts, axis=0)
            alpha = jnp.exp(m_all - m_new)
            m_scr[h][...] = m_new
            pv = _mm(p_scr[h % 2][...], jnp.concatenate([v_ref[:, hs], jnp.ones((TA, HEAD_DIM), BF16)], axis=1))
            acc_scr[h][...] = alpha * acc_scr[h][...] + pv[:, :HEAD_DIM]
            l_scr[h][...] = alpha * l_scr[h][...] + pv[:, HEAD_DIM:]

        for h in range(N_HEADS):
            logits(h)
            m_all, m_parts = row_max(h)
            probs(h, m_parts)
            accumulate(h, m_all, m_parts)

    @pl.when(j >= i - 1)
    def _():
        update(True)

    @pl.when(j < i - 1)
    def _():
        update(False)

    @pl.when(j == i)
    def _():
        for h in range(N_HEADS):
            o_ref[:, h * 128:(h + 1) * 128] = (acc_scr[h][...] / l_scr[h][...]).astype(o_ref.dtype)


def prompt_attention(q16, k16, v16, mask, e_tab, TA=512, RS=16):
    t = q16.shape[0]
    TA = min(TA, t)
    nb = t // TA
    qa = np.concatenate([np.full(i + 1, i, np.int32) for i in range(nb)])
    ka = np.concatenate([np.arange(i + 1, dtype=np.int32) for i in range(nb)])
    grid_spec = pltpu.PrefetchScalarGridSpec(
        num_scalar_prefetch=2,
        grid=(len(qa),),
        in_specs=[pl.BlockSpec((TA, 1024), lambda p, qa, ka: (qa[p], 0)),
                  pl.BlockSpec((TA, 1024), lambda p, qa, ka: (ka[p], 0)),
                  pl.BlockSpec((TA, 1024), lambda p, qa, ka: (ka[p], 0)),
                  pl.BlockSpec((TA, TA), lambda p, qa, ka: (qa[p], ka[p])),
                  pl.BlockSpec((2, N_HEADS, 128, 128), lambda p, qa, ka: (0, 0, 0, 0))],
        out_specs=pl.BlockSpec((TA, 1024), lambda p, qa, ka: (qa[p], 0)),
        scratch_shapes=([pltpu.VMEM((TA, 1), F32)] * N_HEADS + [pltpu.VMEM((TA, HEAD_DIM), F32)] * (2 * N_HEADS)
                        + [pltpu.VMEM((TA, TA), F32)] * 2 + [pltpu.VMEM((TA, TA), BF16)] * 2
                        + [pltpu.VMEM((TA, TA), F32)]))
    return pl.pallas_call(
        functools.partial(_att_kernel, TA=TA, RS=RS),
        grid_spec=grid_spec,
        out_shape=jax.ShapeDtypeStruct((t, 1024), BF16),
        compiler_params=_cp(("arbitrary",)),
        name="prompt_attention",
    )(jnp.asarray(qa), jnp.asarray(ka), q16, k16, v16, mask, e_tab)


def _page_copies(pt_ref, cache_ref, buf_ref, sem_ref, b, first_page, npages, slot):
    out = []
    for g in range(npages):
        pid = pt_ref[b, first_page + g]
        out.append(pltpu.make_async_copy(cache_ref.at[0, pid], buf_ref.at[slot, pl.ds(g * PAGE, PAGE)],
                                         sem_ref.at[slot]))
    return out


def _sidx_kernel(pt_ref, qm_ref, w_ref, knew_ref, cache_ref, sc_ref, kbuf, sem, *, NP, NB):
    b = pl.program_id(0)
    slot = b % 2

    def copies(bq, sl):
        return [pltpu.make_async_copy(cache_ref.at[0, pt_ref[bq, g]], kbuf.at[sl, :, pl.ds(g * PAGE, PAGE)],
                                      sem.at[sl]) for g in range(NP)]

    @pl.when(b == 0)
    def _():
        for c in copies(0, 0):
            c.start()

    @pl.when(b + 1 < NB)
    def _():
        for c in copies(b + 1, 1 - slot):
            c.start()

    for c in copies(b, slot):
        c.wait()

    qm = qm_ref[0]
    w = w_ref[0]

    def fold(s):
        s = jnp.maximum(s, 0.0) * w
        out = s[0:GROUP]
        for h in range(1, IDX_HEADS):
            out = out + s[h * GROUP:(h + 1) * GROUP]
        return out

    past = NP * PAGE
    CH = 2048
    for c0 in range(0, past, CH):
        kc = kbuf[slot, :, c0:c0 + CH].astype(BF16)
        sc_ref[0, :, c0:c0 + CH] = fold(_mm(qm, kc))
    knew = jnp.concatenate([knew_ref[...], jnp.zeros((128 - GROUP, IDX_DIM), F32)], axis=0).astype(BF16)
    tail = fold(_nt(qm, knew))
    r = lax.broadcasted_iota(I32, (GROUP, 128), 0)
    c = lax.broadcasted_iota(I32, (GROUP, 128), 1)
    sc_ref[0, :, past:past + 128] = jnp.where((c >= FIRST_TOK) & (c <= r), tail, NEG_INF)


def sample_scores(page_table, qm, wcol, ki_s, cache_idx):
    nb, npg = page_table.shape
    past = npg * PAGE
    grid_spec = pltpu.PrefetchScalarGridSpec(
        num_scalar_prefetch=1,
        grid=(nb,),
        in_specs=[pl.BlockSpec((1, IDX_HEADS * GROUP, IDX_DIM), lambda b, pt: (b, 0, 0)),
                  pl.BlockSpec((1, IDX_HEADS * GROUP, 1), lambda b, pt: (b, 0, 0)),
                  pl.BlockSpec((GROUP, IDX_DIM), lambda b, pt: (b, 0)),
                  pl.BlockSpec(memory_space=pl.ANY)],
        out_specs=pl.BlockSpec((1, GROUP, past + 128), lambda b, pt: (b, 0, 0)),
        scratch_shapes=[pltpu.VMEM((2, IDX_DIM, past), F32), pltpu.SemaphoreType.DMA((2,))])
    return pl.pallas_call(
        functools.partial(_sidx_kernel, NP=npg, NB=nb),
        grid_spec=grid_spec,
        out_shape=jax.ShapeDtypeStruct((nb, GROUP, past + 128), F32),
        compiler_params=_cp(("arbitrary",)),
        name="sample_scores",
    )(page_table, qm, wcol, ki_s, jnp.swapaxes(cache_idx, 2, 3))


def _ssel_kernel(sc_ref, am_ref, keys_ref, *, TQ, TK, W, k):
    nkt = W // TK

    def to_keys(j, rmax):
        off = pl.multiple_of(j * TK, TK)
        s = sc_ref[:, pl.ds(off, TK)]
        keys_ref[:, pl.ds(off, TK)] = _sort_key(s, off + lax.broadcasted_iota(I32, (1, TK), 1))
        for g in range(TK // 128):
            rmax = jnp.maximum(rmax, s[:, g * 128:(g + 1) * 128])
        return rmax

    rmax = lax.fori_loop(0, nkt, to_keys, jnp.full((TQ, 128), NEG_INF, F32))
    thr, cnt = _kth_largest(keys_ref, nkt, jnp.max(rmax, axis=-1, keepdims=True), TQ, TK, k)

    def emit(j, sel):
        off = pl.multiple_of(j * TK, TK)
        valid = sc_ref[:, pl.ds(off, TK)] > NEG_INF
        am_ref[:, pl.ds(off, TK)] = jnp.where(valid, jnp.where(sel, 1.0, 0.0), 0.0)

    _emit_selection(keys_ref, nkt, thr, cnt, TQ, TK, k, emit)


def sample_topk_mask(scores, k, TQ=32, TK=128):
    r, w = scores.shape
    TQ = min(TQ, r)
    return pl.pallas_call(
        functools.partial(_ssel_kernel, TQ=TQ, TK=TK, W=w, k=k),
        grid=(r // TQ,),
        in_specs=[pl.BlockSpec((TQ, w), lambda i: (i, 0))],
        out_specs=pl.BlockSpec((TQ, w), lambda i: (i, 0)),
        out_shape=jax.ShapeDtypeStruct((r, w), F32),
        scratch_shapes=[pltpu.VMEM((TQ, w), I32)],
        compiler_params=_cp(("parallel",)),
        name="sample_topk_mask",
    )(scores)


def _satt_kernel(pt_ref, q_ref, sel_ref, knew_ref, vnew_ref, tm_ref, hp_ref, blx_ref, ck_ref, cv_ref, o_ref,
                 kbuf, vbuf, ksem, vsem, m_scr, l_scr, acc_scr, *, PC, NCH, NB):
    b = pl.program_id(0)
    c = pl.program_id(1)
    g = b * NCH + c
    slot = g % 2
    CW = PC * PAGE

    def copies(bq, cq, sl):
        return (_page_copies(pt_ref, ck_ref, kbuf, ksem, bq, cq * PC, PC, sl)
                + _page_copies(pt_ref, cv_ref, vbuf, vsem, bq, cq * PC, PC, sl))

    @pl.when(g == 0)
    def _():
        for cp in copies(0, 0, 0):
            cp.start()

    @pl.when(g + 1 < NB * NCH)
    def _():
        nxt = g + 1
        for cp in copies(nxt // NCH, nxt % NCH, 1 - slot):
            cp.start()

    for cp in copies(b, c, slot):
        cp.wait()

    @pl.when(c == 0)
    def _():
        m_scr[...] = jnp.full(m_scr.shape, -1e30, F32)
        l_scr[...] = jnp.zeros(l_scr.shape, F32)
        acc_scr[...] = jnp.zeros(acc_scr.shape, F32)

    def online(s, v16):
        m_old = m_scr[...]
        m_new = jnp.maximum(m_old, jnp.max(s, axis=-1, keepdims=True))
        alpha = jnp.exp(m_old - m_new)
        pr = jnp.exp(s - m_new)
        l_scr[...] = alpha * l_scr[...] + jnp.sum(pr, axis=-1, keepdims=True)
        acc_scr[...] = alpha * acc_scr[...] + _mm(_b16(pr), v16)
        m_scr[...] = m_new

    NX = CW * N_HEADS
    q64 = q_ref[0]
    last = jnp.where(c == NCH - 1, 1.0, 0.0)
    ri = lax.broadcasted_iota(I32, (PAGE, PAGE * N_HEADS), 0)
    ci = lax.broadcasted_iota(I32, (PAGE, PAGE * N_HEADS), 1)
    spread = jnp.where((ci >> 3) == ri, 1.0, 0.0).astype(BF16)
    sx = _mm(_b16(sel_ref[0].reshape(PC * GROUP, PAGE)), spread)
    selx = jnp.concatenate([sx[p * GROUP:(p + 1) * GROUP] for p in range(PC)], axis=1)
    addm = jnp.where(jnp.concatenate([selx] * N_HEADS, axis=0) > 0.5, hp_ref[...], NEG_INF)
    s = _nt(q64, _b16(kbuf[slot].reshape(NX, HEAD_DIM))) + addm
    s = s + last * jnp.concatenate([jnp.zeros((N_HEADS * GROUP, NX - PAGE * N_HEADS), F32), blx_ref[...]], axis=1)
    online(s, _b16(vbuf[slot].reshape(NX, HEAD_DIM)))

    @pl.when(c == NCH - 1)
    def _():
        zpad = jnp.zeros((128 - N_HEADS * GROUP, HEAD_DIM), F32)
        kn = _b16(jnp.concatenate([knew_ref[0], zpad], axis=0))
        vn = _b16(jnp.concatenate([vnew_ref[0], zpad], axis=0))
        online(_nt(q64, kn) + tm_ref[0], vn)
        out = acc_scr[...] / l_scr[...]
        for h in range(N_HEADS):
            o_ref[:, h * 128:(h + 1) * 128] = out[h * GROUP:(h + 1) * GROUP, :]


def sample_attention(page_table, q64, selp, kn64, vn64, tmask, hp, blx, cache_k, cache_v, PC=8):
    nb, npg = page_table.shape
    PC = min(PC, npg)
    nch = npg // PC
    R = N_HEADS * GROUP
    grid_spec = pltpu.PrefetchScalarGridSpec(
        num_scalar_prefetch=1,
        grid=(nb, nch),
        in_specs=[pl.BlockSpec((1, R, HEAD_DIM), lambda b, c, pt: (b, 0, 0)),
                  pl.BlockSpec((1, PC, GROUP, PAGE), lambda b, c, pt: (b, c, 0, 0)),
                  pl.BlockSpec((1, R, HEAD_DIM), lambda b, c, pt: (b, 0, 0)),
                  pl.BlockSpec((1, R, HEAD_DIM), lambda b, c, pt: (b, 0, 0)),
                  pl.BlockSpec((1, R, 128), lambda b, c, pt: (b, 0, 0)),
                  pl.BlockSpec((R, PC * PAGE * N_HEADS), lambda b, c, pt: (0, 0), pipeline_mode=pl.Buffered(1)),
                  pl.BlockSpec((R, PAGE * N_HEADS), lambda b, c, pt: (0, 0), pipeline_mode=pl.Buffered(1)),
                  pl.BlockSpec(memory_space=pl.ANY),
                  pl.BlockSpec(memory_space=pl.ANY)],
        out_specs=pl.BlockSpec((GROUP, 1024), lambda b, c, pt: (b, 0)),
        scratch_shapes=[pltpu.VMEM((2, PC * PAGE, N_HEADS, HEAD_DIM), F32),
                        pltpu.VMEM((2, PC * PAGE, N_HEADS, HEAD_DIM), F32),
                        pltpu.SemaphoreType.DMA((2,)), pltpu.SemaphoreType.DMA((2,)),
                        pltpu.VMEM((R, 1), F32), pltpu.VMEM((R, 1), F32), pltpu.VMEM((R, HEAD_DIM), F32)])
    return pl.pallas_call(
        functools.partial(_satt_kernel, PC=PC, NCH=nch, NB=nb),
        grid_spec=grid_spec,
        out_shape=jax.ShapeDtypeStruct((nb * GROUP, 1024), F32),
        compiler_params=_cp(("arbitrary", "arbitrary")),
        name="sample_attention",
    )(page_table, q64, selp, kn64, vn64, tmask, hp, blx, cache_k, cache_v)


def _mix_kernel(oa_ref, ob_ref, ga_ref, gb_ref, x_ref, g1_ref, wa_ref, wb_ref, wo_ref, o_ref):
    ya = _mm(oa_ref[...].astype(BF16), wa_ref[...])
    yb = _mm(ob_ref[...].astype(BF16), wb_ref[...])
    mixed = ga_ref[...] * ya + gb_ref[...] * yb
    o_ref[...] = x_ref[...] + g1_ref[...] * _mm(mixed.astype(BF16), wo_ref[...])


def mix(oa, ob, ga, gb, x, g1, wa, wb, wo, tm=512):
    t = x.shape[0]
    tm = min(tm, t)
    per_row = g1.shape[0] != 1
    mod_spec = (pl.BlockSpec((tm, D_MODEL), lambda i: (i, 0)) if per_row
                else pl.BlockSpec((1, D_MODEL), lambda i: (0, 0)))
    row = pl.BlockSpec((tm, D_MODEL), lambda i: (i, 0))
    wspec = _const_spec((D_MODEL, D_MODEL))
    return pl.pallas_call(
        _mix_kernel,
        grid=(t // tm,),
        in_specs=[row, row, row, row, row, mod_spec, wspec, wspec, wspec],
        out_specs=row,
        out_shape=jax.ShapeDtypeStruct((t, D_MODEL), F32),
        compiler_params=_cp(("parallel",)),
        name="mix",
    )(oa, ob, ga, gb, x, g1, wa, wb, wo)


def _ffn_kernel(*refs, TM, has_hist):
    if has_hist:
        (x_ref, sh_ref, sc_ref, g2_ref, hist_ref, n2_ref, wup_ref, cw_ref, cb_ref, wdn_ref, fw_ref,
         y_ref, u_ref, ubuf) = refs
    else:
        (x_ref, sh_ref, sc_ref, g2_ref, n2_ref, wup_ref, cw_ref, cb_ref, wdn_ref, fw_ref,
         y_ref, u_ref, ubuf) = refs
    i = pl.program_id(0)

    @pl.when(i == 0)
    def _():
        ubuf[0:8, :] = jnp.zeros((8, 2 * D_FF), F32)

    x = x_ref[...]
    h = x * lax.rsqrt(jnp.mean(x * x, -1, keepdims=True) + EPS) * n2_ref[...]
    h = h * (1.0 + sc_ref[...]) + sh_ref[...]
    hb = h.astype(BF16)
    if has_hist:
        is_hist = (lax.broadcasted_iota(I32, (TM, 1), 0) % GROUP) < FIRST_TOK
    CB = 256
    acc = jnp.zeros((TM, D_MODEL), F32)
    def up(c):
        us = []
        for base in (c, D_FF + c):
            cs = slice(base, base + CB)
            u = _mm(hb, wup_ref[:, cs])
            if has_hist:
                u = jnp.where(is_hist, hist_ref[:, cs], u)
            ubuf[8:8 + TM, cs] = u
            us.append(u)
        return us

    starts = list(range(0, D_FF, CB))
    nxt = up(starts[0])
    for n, c in enumerate(starts):
        cur = nxt
        if n + 1 < len(starts):
            nxt = up(starts[n + 1])
        halves = []
        for u, base in zip(cur, (c, D_FF + c)):
            cs = slice(base, base + CB)
            halves.append(cw_ref[2:3, cs] * u + cw_ref[1:2, cs] * ubuf[7:7 + TM, cs]
                          + cw_ref[0:1, cs] * ubuf[6:6 + TM, cs] + cb_ref[:, cs])
        act = _silu(halves[0]) * halves[1]
        acc = acc + _mm(act.astype(BF16), wdn_ref[c:c + CB, :])
    if has_hist:
        u_ref[...] = ubuf[8:8 + TM, :]
    else:
        u_ref[...] = ubuf[TM:TM + 8, :]
    ubuf[0:8, :] = ubuf[TM:TM + 8, :]
    x2 = x + g2_ref[...] * acc
    y_ref[...] = x2 * lax.rsqrt(jnp.mean(x2 * x2, -1, keepdims=True) + EPS) * fw_ref[...]


def ffn(x1, sh, sc, g2, hist, norm2_w, w_up, conv_w, conv_b, w_down, final_w, tm=256):
    t = x1.shape[0]
    tm = min(tm, t)
    has_hist = hist is not None
    per_row = sh.shape[0] != 1
    mod_spec = (pl.BlockSpec((tm, D_MODEL), lambda i: (i, 0)) if per_row
                else pl.BlockSpec((1, D_MODEL), lambda i: (0, 0)))
    row = pl.BlockSpec((tm, D_MODEL), lambda i: (i, 0))
    in_specs = [row, mod_spec, mod_spec, mod_spec]
    args = [x1, sh, sc, g2]
    if has_hist:
        in_specs.append(pl.BlockSpec((tm, 2 * D_FF), lambda i: (i, 0)))
        args.append(hist)
    in_specs += [_const_spec((1, D_MODEL)), _const_spec((D_MODEL, 2 * D_FF)), _const_spec((FFN_CONV, 2 * D_FF)),
                 _const_spec((1, 2 * D_FF)), _const_spec((D_FF, D_MODEL)), _const_spec((1, D_MODEL))]
    args += [norm2_w, w_up, conv_w, conv_b, w_down, final_w]
    if has_hist:
        u_spec = pl.BlockSpec((tm, 2 * D_FF), lambda i: (i, 0))
        u_shape = jax.ShapeDtypeStruct((t, 2 * D_FF), F32)
    else:
        u_spec = pl.BlockSpec((8, 2 * D_FF), lambda i: (0, 0))
        u_shape = jax.ShapeDtypeStruct((8, 2 * D_FF), F32)
    return pl.pallas_call(
        functools.partial(_ffn_kernel, TM=tm, has_hist=has_hist),
        grid=(t // tm,),
        in_specs=in_specs,
        out_specs=[row, u_spec],
        out_shape=[jax.ShapeDtypeStruct((t, D_MODEL), F32), u_shape],
        scratch_shapes=[pltpu.VMEM((8 + tm, 2 * D_FF), F32)],
        compiler_params=_cp(("arbitrary",)),
        name="ffn",
    )(*args)


def _group_rows(a, first):
    b, n, c = a.shape
    return jnp.pad(a, ((0, 0), (first, GROUP - first - n), (0, 0))).reshape(b * GROUP, c)


def kernel(x_prompt, x_sample, cache_k, cache_v, cache_idx_k, state_gdn, state_gdn_conv, state_ffn_conv,
           page_table, c_prompt, c_sample, w_ada, b_ada, norm1_w, w_in, gdn_conv_w, gdn_A_log, gdn_dt_bias,
           gdn_norm_w, idx_knorm_w, idx_knorm_b, w_branch_a, w_branch_b, w_out, norm2_w, w_up, ffn_conv_w,
           ffn_conv_b, w_down, rel_bias, final_norm_w):
    bp, tp, _ = x_prompt.shape
    bs, ts, _ = x_sample.shape
    assert bp == 1 and ts == GROUP - FIRST_TOK and w_ada.shape[0] == 1
    npg = page_table.shape[1]
    past = npg * PAGE

    w = w_in[0]
    w_cat = jnp.concatenate(
        [w[:, 0:4096], w[:, 4112:7184], w[:, 7768:9816], w[:, 7184:7696],
         w[:, 7696:7760], w[:, 4096:4112], w[:, 7760:7768], jnp.zeros((D_MODEL, 40), F32),
         jnp.zeros((D_MODEL, 64), F32), w[:, 7696:7760]], axis=1).astype(BF16)
    w_smt = w_cat[:, C_SM:C_SM2].T
    z64 = jnp.zeros((IDX_DIM,), F32)
    lnw2 = jnp.stack([jnp.concatenate([idx_knorm_w[0], z64]), jnp.concatenate([z64, idx_knorm_w[0]])])
    lnb2 = jnp.stack([jnp.concatenate([idx_knorm_b[0], z64]), jnp.concatenate([z64, idx_knorm_b[0]])])
    misc = jnp.zeros((8, 128), F32)
    misc = misc.at[0, L_A:L_A + N_HEADS].set(gdn_A_log[0]).at[1, L_A:L_A + N_HEADS].set(gdn_dt_bias[0])
    wa16, wb16, wo16 = w_branch_a[0].astype(BF16), w_branch_b[0].astype(BF16), w_out[0].astype(BF16)
    wup16, wdn16 = w_up[0].astype(BF16), w_down[0].astype(BF16)
    n1 = norm1_w[0].reshape(1, D_MODEL)
    n2 = norm2_w[0].reshape(1, D_MODEL)
    fw = final_norm_w.reshape(1, D_MODEL)
    gnw = gdn_norm_w[0].reshape(1, 128)
    ffn_b = ffn_conv_b[0].reshape(1, 2 * D_FF)

    c_all = jnp.concatenate([c_prompt, c_sample], axis=0)
    pad_r = (-c_all.shape[0]) % 8
    mod = ada_mod(jnp.pad(c_all, ((0, pad_r), (0, 0))), w_ada[0], b_ada[0])
    mod_p = mod[0:1]
    mod_s = jnp.repeat(mod[1:1 + bs], GROUP, axis=0)

    def mods(m):
        return [m[:, i * D_MODEL:(i + 1) * D_MODEL] for i in range(6)]

    e_tab, b_last, b_tail = bias_tables(rel_bias)

    xp = x_prompt[0]
    sh1, sc1, g1, sh2, sc2, g2 = mods(mod_p)
    (conv_p, gz_p, q_p, kf_p, vf_p, k16_p, v16_p, ga_p, gb_p, qi_p, small_p, ki_p, kb0_p, kb1_p, smt_p) = in_proj(
        xp, sh1, sc1, n1, w_cat, w_smt, lnw2, lnb2)
    oa_p, s_p = gdn(conv_p, None, small_p, smt_p, gz_p, jnp.zeros((1, N_HEADS, 128, 128), F32), gdn_conv_w[0],
                    misc, gnw, nseq=1, rows=min(4 * GDN_CHUNK, tp), out_dtype=BF16)
    mask_p = prompt_topk_mask(qi_p, small_p, kb0_p, kb1_p, min(TOPK_MAX, tp // 4))
    ob_p = prompt_attention(q_p, k16_p, v16_p, mask_p, e_tab)
    x1_p = mix(oa_p, ob_p, ga_p, gb_p, xp, g1, wa16, wb16, wo16)
    y_p, utail_p = ffn(x1_p, sh2, sc2, g2, None, n2, wup16, ffn_conv_w[0], ffn_b, wdn16, fw)

    xs = _group_rows(x_sample, FIRST_TOK)
    sh1, sc1, g1, sh2, sc2, g2 = mods(mod_s)
    (conv_s, gz_s, q_s, kf_s, vf_s, _, _, ga_s, gb_s, qi_s, small_s, ki_s, _, _, _) = in_proj(
        xs, sh1, sc1, n1, w_cat, w_smt, lnw2, lnb2)
    hist_gdn = _group_rows(state_gdn_conv[0], FIRST_TOK - (GDN_CONV - 1))
    oa_s, s_s = gdn(conv_s, hist_gdn, small_s, None, gz_s, state_gdn[0], gdn_conv_w[0], misc, gnw,
                    nseq=bs, rows=GROUP, out_dtype=F32)
    qm = qi_s.reshape(bs, GROUP, IDX_HEADS, IDX_DIM).transpose(0, 2, 1, 3).reshape(bs, IDX_HEADS * GROUP, IDX_DIM)
    wcol = (small_s[:, L_WI:L_WI + IDX_HEADS] * IDX_W_SCALE).reshape(bs, GROUP, IDX_HEADS)
    wcol = wcol.transpose(0, 2, 1).reshape(bs, IDX_HEADS * GROUP, 1)
    scores = sample_scores(page_table, qm, wcol, ki_s, cache_idx_k)
    sc_tok = scores[:, FIRST_TOK:, :].reshape(bs * ts, past + 128)
    sel = sample_topk_mask(sc_tok, min(TOPK_MAX, (past + ts) // 4))
    sel = jnp.pad(sel.reshape(bs, ts, past + 128), ((0, 0), (FIRST_TOK, 0), (0, 0)), constant_values=1.0)
    selp = sel[:, :, :past].reshape(bs, GROUP, npg, PAGE).transpose(0, 2, 1, 3)
    heads = jnp.arange(N_HEADS)
    ok = (sel[:, None, :, past:past + GROUP, None] > 0.5) & (heads[:, None] == heads[None, :])[None, :, None, None, :]
    bt = b_tail.reshape(N_HEADS, GROUP, 128)[None, :, :, :GROUP, None]
    tmask = jnp.where(ok, bt, NEG_INF).reshape(bs, N_HEADS * GROUP, GROUP * N_HEADS)
    tmask = jnp.pad(tmask, ((0, 0), (0, 0), (0, 128 - GROUP * N_HEADS)), constant_values=NEG_INF)
    nx = min(8, npg) * PAGE * N_HEADS
    hp = jnp.asarray(np.where((np.arange(nx)[None, :] & 7) == (np.arange(N_HEADS * GROUP)[:, None] >> 3),
                              0.0, NEG_INF).astype(np.float32))
    blx = jnp.repeat(b_last, N_HEADS, axis=1)
    q64 = q_s.reshape(bs, GROUP, N_HEADS, HEAD_DIM).transpose(0, 2, 1, 3).reshape(bs, N_HEADS * GROUP, HEAD_DIM)
    ob_s = sample_attention(page_table, q64, selp, kf_s.reshape(bs, GROUP * N_HEADS, HEAD_DIM),
                            vf_s.reshape(bs, GROUP * N_HEADS, HEAD_DIM), tmask, hp, blx, cache_k, cache_v)
    x1_s = mix(oa_s, ob_s, ga_s, gb_s, xs, g1, wa16, wb16, wo16)
    hist_ffn = _group_rows(state_ffn_conv[0], FIRST_TOK - (FFN_CONV - 1))
    y_s, u_s = ffn(x1_s, sh2, sc2, g2, hist_ffn, n2, wup16, ffn_conv_w[0], ffn_b, wdn16, fw)

    def tok(a):
        return a.reshape(bs, GROUP, -1)[:, FIRST_TOK:, :]

    hd = (N_HEADS, HEAD_DIM)
    return (y_p[None], tok(y_s),
            kf_p.reshape((1, 1, tp) + hd), vf_p.reshape((1, 1, tp) + hd), ki_p.reshape(1, 1, tp, IDX_DIM),
            tok(kf_s).reshape((1, bs, ts) + hd), tok(vf_s).reshape((1, bs, ts) + hd), tok(ki_s)[None],
            s_p[None], s_s[None],
            conv_p[tp - (GDN_CONV - 1):][None, None],
            conv_s.reshape(bs, GROUP, -1)[:, GROUP - (GDN_CONV - 1):][None],
            utail_p[8 - (FFN_CONV - 1):][None, None],
            u_s.reshape(bs, GROUP, -1)[:, GROUP - (FFN_CONV - 1):][None])
```

```python
import functools
import math

import jax
import jax.numpy as jnp
import numpy as np
from jax import lax
from jax.experimental import pallas as pl
from jax.experimental.pallas import tpu as pltpu

F32 = jnp.float32
BF16 = jnp.bfloat16
I32 = jnp.int32
HI = lax.Precision.HIGHEST

D_MODEL = 1024
N_HEADS = 8
HEAD_DIM = 128
IDX_HEADS = 8
IDX_DIM = 64
TOPK_MAX = 256
D_FF = 2816
GDN_CONV = 4
FFN_CONV = 3
GDN_CHUNK = 64
PAGE = 128
REL_BUCKETS = 32
REL_MAX_DIST = 128
EPS = 1e-6
IDX_W_SCALE = (IDX_HEADS * IDX_DIM) ** -0.5
ATT_SCALE = HEAD_DIM ** -0.5

C_CONV, C_GZ, C_Q, C_K, C_V, C_GA, C_GB, C_QI, C_SM, C_SM2, C_END = (
    0, 3072, 4096, 5120, 6144, 7168, 8192, 9216, 9728, 9856, 9984)
L_KI, L_B, L_A, L_WI = 0, 64, 72, 80

GROUP = 8
FIRST_TOK = 4
NEG_INF = float("-inf")
KEY_NEG_INF = -2139095041
VMEM_LIMIT = 56 * 1024 * 1024


def _cp(sem, vmem=VMEM_LIMIT):
    return pltpu.CompilerParams(dimension_semantics=sem, vmem_limit_bytes=vmem)


def _const_spec(shape):
    nd = len(shape)
    return pl.BlockSpec(shape, lambda *a: (0,) * nd, pipeline_mode=pl.Buffered(1))


def _sigmoid(x):
    return 1.0 / (1.0 + jnp.exp(-x))


def _silu(x):
    return x * _sigmoid(x)


def _softplus(x):
    return jnp.maximum(x, 0.0) + jnp.log(1.0 + jnp.exp(-jnp.abs(x)))


def _nt(a, b, precision=None):
    return lax.dot_general(a, b, (((1,), (1,)), ((), ())), precision=precision, preferred_element_type=F32)


def _tn(a, b, precision=None):
    return lax.dot_general(a, b, (((0,), (0,)), ((), ())), precision=precision, preferred_element_type=F32)


def _mm(a, b, precision=None):
    return jnp.dot(a, b, precision=precision, preferred_element_type=F32)


def _ada_kernel(c_ref, w_ref, b_ref, o_ref):
    c = c_ref[...]
    o_ref[...] = _mm(_silu(c), w_ref[...], HI) + b_ref[...]


def ada_mod(c_all, w_ada, b_ada):
    r, d = c_all.shape
    n = w_ada.shape[1]
    tn = 1024
    return pl.pallas_call(
        _ada_kernel,
        grid=(n // tn,),
        in_specs=[pl.BlockSpec((r, d), lambda j: (0, 0)),
                  pl.BlockSpec((d, tn), lambda j: (0, j)),
                  pl.BlockSpec((1, tn), lambda j: (0, j))],
        out_specs=pl.BlockSpec((r, tn), lambda j: (0, j)),
        out_shape=jax.ShapeDtypeStruct((r, n), F32),
        compiler_params=_cp(("parallel",)),
        name="ada_mod",
    )(c_all, w_ada, b_ada.reshape(1, n))


def _rel_bucket(d):
    n = jnp.maximum(d, 0)
    max_exact = REL_BUCKETS // 2
    nf = jnp.maximum(n, max_exact).astype(F32)
    large = max_exact + (jnp.log(nf / max_exact) / math.log(REL_MAX_DIST / max_exact)
                         * (REL_BUCKETS - max_exact)).astype(I32)
    large = jnp.minimum(large, REL_BUCKETS - 1)
    return jnp.where(n < max_exact, n, large)


def _bias_kernel(rb_ref, e_ref, bl_ref, bt_ref):
    def table(d, h):
        bk = _rel_bucket(d)
        acc = jnp.zeros(d.shape, F32)
        for b in range(REL_BUCKETS):
            acc = acc + jnp.where(bk == b, rb_ref[b, h], 0.0)
        return acc - rb_ref[REL_BUCKETS - 1, h]

    a = lax.broadcasted_iota(I32, (128, 128), 0)
    b = lax.broadcasted_iota(I32, (128, 128), 1)
    r = lax.broadcasted_iota(I32, (GROUP, 128), 0)
    c = lax.broadcasted_iota(I32, (GROUP, 128), 1)
    for h in range(N_HEADS):
        e_ref[0, h] = table(a - b, h)
        e_ref[1, h] = table(a - b + 128, h)
        bl_ref[h * GROUP:(h + 1) * GROUP, :] = table(128 + r - FIRST_TOK - c, h)
        bt_ref[h * GROUP:(h + 1) * GROUP, :] = table(r - c, h)


def bias_tables(rel_bias):
    return pl.pallas_call(
        _bias_kernel,
        in_specs=[pl.BlockSpec(memory_space=pltpu.SMEM)],
        out_shape=(jax.ShapeDtypeStruct((2, N_HEADS, 128, 128), F32),
                   jax.ShapeDtypeStruct((N_HEADS * GROUP, 128), F32),
                   jax.ShapeDtypeStruct((N_HEADS * GROUP, 128), F32)),
        name="bias_tables",
    )(rel_bias)


def _in_kernel(x_ref, sh_ref, sc_ref, nw_ref, w_ref, wt_ref, lnw_ref, lnb_ref,
               conv_ref, gz_ref, q_ref, kf_ref, vf_ref, k16_ref, v16_ref, ga_ref, gb_ref, qi_ref,
               small_ref, ki_ref, kb0_ref, kb1_ref, smt_ref):
    x = x_ref[...]
    h = x * lax.rsqrt(jnp.mean(x * x, -1, keepdims=True) + EPS) * nw_ref[...]
    h = h * (1.0 + sc_ref[...]) + sh_ref[...]
    hb = h.astype(BF16)

    def proj(a, b):
        return _mm(hb, w_ref[:, a:b])

    for c in range(0, 3072, 1024):
        conv_ref[:, c:c + 1024] = proj(C_CONV + c, C_CONV + c + 1024)
    gz_ref[...] = proj(C_GZ, C_Q)
    q_ref[...] = (proj(C_Q, C_K) * ATT_SCALE).astype(BF16)
    k = proj(C_K, C_V)
    kf_ref[...] = k
    k16_ref[...] = k.astype(BF16)
    v = proj(C_V, C_GA)
    vf_ref[...] = v
    v16_ref[...] = v.astype(BF16)
    ga_ref[...] = _sigmoid(proj(C_GA, C_GB))
    gb_ref[...] = _sigmoid(proj(C_GB, C_QI))
    qi_ref[...] = proj(C_QI, C_SM).astype(BF16)
    z = proj(C_SM, C_SM2)
    small_ref[...] = z
    smt_ref[...] = _nt(wt_ref[...], hb)
    lane = lax.broadcasted_iota(I32, z.shape, 1)
    for zz, lo, out_ref in ((z, True, kb0_ref), (proj(C_SM2, C_END), False, kb1_ref)):
        m = (lane < IDX_DIM) if lo else (lane >= IDX_DIM)
        zk = jnp.where(m, zz, 0.0)
        mean = jnp.sum(zk, -1, keepdims=True) * (1.0 / IDX_DIM)
        xc = jnp.where(m, zz - mean, 0.0)
        var = jnp.sum(xc * xc, -1, keepdims=True) * (1.0 / IDX_DIM)
        row = 0 if lo else 1
        kn = xc * lax.rsqrt(var + EPS) * lnw_ref[row:row + 1, :] + lnb_ref[row:row + 1, :]
        out_ref[...] = kn.astype(BF16)
        if lo:
            ki_ref[...] = kn[:, :IDX_DIM]


def in_proj(x, sh, sc, norm_w, w_cat, w_smt, lnw2, lnb2, tm=256):
    t = x.shape[0]
    tm = min(tm, t)
    per_row = sh.shape[0] != 1
    mod_spec = (pl.BlockSpec((tm, D_MODEL), lambda i: (i, 0)) if per_row
                else pl.BlockSpec((1, D_MODEL), lambda i: (0, 0)))

    def rows(n, dt):
        return pl.BlockSpec((tm, n), lambda i: (i, 0)), jax.ShapeDtypeStruct((t, n), dt)

    outs = [rows(3072, F32), rows(1024, F32), rows(1024, BF16), rows(1024, F32), rows(1024, F32),
            rows(1024, BF16), rows(1024, BF16), rows(1024, F32), rows(1024, F32), rows(512, BF16),
            rows(128, F32), rows(IDX_DIM, F32), rows(128, BF16), rows(128, BF16),
            (pl.BlockSpec((128, tm), lambda i: (0, i)), jax.ShapeDtypeStruct((128, t), F32))]
    return pl.pallas_call(
        _in_kernel,
        grid=(t // tm,),
        in_specs=[pl.BlockSpec((tm, D_MODEL), lambda i: (i, 0)), mod_spec, mod_spec,
                  _const_spec((1, D_MODEL)), _const_spec((D_MODEL, C_END)), _const_spec((128, D_MODEL)),
                  _const_spec((2, 128)), _const_spec((2, 128))],
        out_specs=[o[0] for o in outs],
        out_shape=[o[1] for o in outs],
        compiler_params=_cp(("parallel",)),
        name="in_proj",
    )(x, sh, sc, norm_w, w_cat, w_smt, lnw2, lnb2)


def _b16(a):
    return a.astype(BF16)


def _mmb(a, b):
    return jnp.dot(_b16(a), _b16(b), preferred_element_type=F32)


def _split(a):
    hi = _b16(a)
    return hi, _b16(a - hi.astype(F32))


def _mm3(a, b):
    ah, al = a
    bh, bl = b
    return (jnp.dot(ah, bh, preferred_element_type=F32) + jnp.dot(ah, bl, preferred_element_type=F32)
            + jnp.dot(al, bh, preferred_element_type=F32))


def _gdn_kernel(*refs, R, C, N, has_hist):
    s_scr = refs[-N_HEADS:]
    refs = refs[:-N_HEADS]
    if has_hist:
        conv_ref, hist_ref, small_ref, gz_ref, s0_ref, cw_ref, misc_ref, nw_ref, o_ref, sout_ref, xbuf = refs
    else:
        (conv_ref, small_ref, smt_ref, at_ref, dtt_ref, gz_ref, s0_ref, cw_ref, misc_ref, nw_ref,
         o_ref, sout_ref, xbuf) = refs
    nch = max(R // C, 1)
    RC = nch * C
    shift = C.bit_length() - 1
    n = pl.program_id(1)

    @pl.when(n == 0)
    def _():
        xbuf[...] = jnp.zeros_like(xbuf)
        for h in range(N_HEADS):
            s_scr[h][...] = s0_ref[0, h]

    x = conv_ref[...]
    validf = None
    if has_hist:
        rows_r = lax.broadcasted_iota(I32, (R, 1), 0)
        x = jnp.where((rows_r % GROUP) < FIRST_TOK, hist_ref[...], x)
        rows_c = lax.broadcasted_iota(I32, (RC, 1), 0)
        validf = jnp.where((rows_c < R) & ((rows_c % GROUP) >= FIRST_TOK), 1.0, 0.0)
    xbuf[8:8 + R, :] = x

    small = small_ref[...]
    if R < RC:
        small = jnp.concatenate([small, jnp.zeros((RC - R, 128), F32)], axis=0)
    misc = misc_ref[...]
    gfull = -jnp.exp(misc[0:1]) * _softplus(small + misc[1:2])
    betaf = _sigmoid(small)
    if validf is not None:
        gfull = gfull * validf
        betaf = betaf * validf
    ri = lax.broadcasted_iota(I32, (RC, RC), 0)
    ci = lax.broadcasted_iota(I32, (RC, RC), 1)
    same = (ri >> shift) == (ci >> shift)
    gcol = _mm(jnp.where(same & (ri >= ci), 1.0, 0.0), gfull, HI)
    if not has_hist:
        gt = -jnp.exp(at_ref[...]) * _softplus(smt_ref[...] + dtt_ref[...])
        grow_all = _mm(gt[L_A:L_A + N_HEADS, :], jnp.where(same & (ri <= ci), 1.0, 0.0), HI)
    rc = lax.broadcasted_iota(I32, (C, C), 0)
    cc = lax.broadcasted_iota(I32, (C, C), 1)
    tril = rc >= cc
    strict = rc > cc
    eye = rc == cc
    ones_cc = jnp.ones((C, C), F32)
    w = cw_ref[...]

    pre = []
    zrows = jnp.zeros((128 - C, 128), F32)
    for c in range(nch):
        r0 = c * C
        y = (w[3:4] * xbuf[8 + r0:8 + r0 + C, :] + w[2:3] * xbuf[7 + r0:7 + r0 + C, :]
             + w[1:2] * xbuf[6 + r0:6 + r0 + C, :] + w[0:1] * xbuf[5 + r0:5 + r0 + C, :])
        y = _silu(y)
        vf = None if validf is None else validf[r0:r0 + C]
        for h in range(N_HEADS):
            q = y[:, h * 128:(h + 1) * 128]
            k = y[:, 1024 + h * 128:1024 + (h + 1) * 128]
            v = y[:, 2048 + h * 128:2048 + (h + 1) * 128]
            q = q * lax.rsqrt(jnp.sum(q * q, -1, keepdims=True) + EPS) * (HEAD_DIM ** -0.5)
            k = k * lax.rsqrt(jnp.sum(k * k, -1, keepdims=True) + EPS)
            if vf is not None:
                k = k * vf
                v = v * vf
            bc = jnp.broadcast_to(betaf[r0:r0 + C, L_B + h:L_B + h + 1], (C, 128))
            gc = jnp.broadcast_to(gcol[r0:r0 + C, L_A + h:L_A + h + 1], (C, 128))
            gb = gc[:, :C]
            if has_hist:
                grow = _mm(ones_cc, jnp.where(eye, gb, 0.0), HI)
            else:
                grow = jnp.broadcast_to(grow_all[h:h + 1, r0:r0 + C], (C, C))
            dec = jnp.where(tril, jnp.exp(jnp.minimum(gb - grow, 0.0)), 0.0)
            eg = jnp.exp(gc)
            glast = gc[C - 1:C, :]
            kd = k * jnp.exp(glast - gc)
            kdt = jnp.concatenate([kd, zrows], axis=0).T[:, :C]
            pre.append((q, k, v, bc, dec, eg, jnp.exp(glast), _b16(kdt)))

    nprob = nch * N_HEADS
    a_l = []
    for q, k, v, bc, dec, eg, ge, kdt in pre:
        kh, kl = _split(k)
        a_l.append(jnp.where(strict, bc[:, :C] * (_nt(kh, kh) + _nt(kh, kl) + _nt(kl, kh)) * dec, 0.0))
    xo_l = [-a for a in a_l]
    pw_l = []
    for a in a_l:
        a_s = _split(a)
        pw_l.append(_mm3(a_s, a_s))
    span = 2
    while span < C:
        span *= 2
        if span <= 8:
            pws_l = [_split(pw) for pw in pw_l]
            xo_l = [xo + pw + _mm3(_split(xo), pws) for xo, pw, pws in zip(xo_l, pw_l, pws_l)]
            if span < C:
                pw_l = [_mm3(pws, pws) for pws in pws_l]
        else:
            xo_l = [xo + pw + _mmb(xo, pw) for xo, pw in zip(xo_l, pw_l)]
            if span < C:
                pw_l = [_mmb(pw, pw) for pw in pw_l]
    um_l, wm_l, qk_l = [], [], []
    for (q, k, v, bc, dec, eg, ge, kdt), xo in zip(pre, xo_l):
        rv = v * bc
        rk = k * (bc * eg)
        xo_s = _split(xo)
        um_l.append(rv + _mm3(xo_s, _split(rv)))
        wm_l.append(rk + _mm3(xo_s, _split(rk)))
        qk_l.append(_nt(_b16(q), _b16(k)) * dec)

    outs = [None] * nprob
    for c in range(nch):
        probs = range(c * N_HEADS, (c + 1) * N_HEADS)
        s_l = [s_scr[h][...] for h in range(N_HEADS)]
        sb_l = [_b16(s) for s in s_l]
        u_l = [um_l[p] - _mm(_b16(wm_l[p]), sb) for p, sb in zip(probs, sb_l)]
        for h, p in enumerate(probs):
            q, k, v, bc, dec, eg, ge, kdt = pre[p]
            outs[p] = _mm(_b16(q * eg), sb_l[h]) + _mmb(qk_l[p], u_l[h])
            s_scr[h][...] = s_l[h] * ge + _mm(kdt, _b16(u_l[h]))

    ro = min(R, C)
    for c in range(nch):
        r0 = c * C
        for h in range(N_HEADS):
            o = outs[c * N_HEADS + h][:ro]
            on = o * lax.rsqrt(jnp.mean(o * o, -1, keepdims=True) + EPS) * nw_ref[...]
            gz = gz_ref[r0:r0 + ro, h * 128:(h + 1) * 128]
            o_ref[r0:r0 + ro, h * 128:(h + 1) * 128] = (on * _silu(gz)).astype(o_ref.dtype)

    if N > 1:
        xbuf[0:8, :] = xbuf[RC:RC + 8, :]

    @pl.when(n == N - 1)
    def _():
        for h in range(N_HEADS):
            sout_ref[0, h] = s_scr[h][...]


def gdn(conv_in, hist, small, smt, gz, s0, conv_w, misc, norm_w, *, nseq, rows, out_dtype):
    C = GDN_CHUNK
    t = conv_in.shape[0]
    nstep = t // (nseq * rows)
    has_hist = hist is not None
    rc = max(rows // C, 1) * C

    def rspec(n):
        return pl.BlockSpec((rows, n), lambda b, c: (b * nstep + c, 0))

    def cspec(shape):
        return pl.BlockSpec(shape, lambda b, c: (0,) * len(shape))

    if has_hist:
        in_specs = [rspec(3072), rspec(3072), rspec(128), rspec(1024)]
        args = [conv_in, hist, small, gz]
    else:
        lane_rep = jnp.ones((1, rows), F32)
        at = misc[0].reshape(128, 1) * lane_rep
        dtt = misc[1].reshape(128, 1) * lane_rep
        in_specs = [rspec(3072), rspec(128), pl.BlockSpec((128, rows), lambda b, c: (0, b * nstep + c)),
                    cspec((128, rows)), cspec((128, rows)), rspec(1024)]
        args = [conv_in, small, smt, at, dtt, gz]
    in_specs += [pl.BlockSpec((1, N_HEADS, 128, 128), lambda b, c: (b, 0, 0, 0)),
                 cspec((GDN_CONV, 3072)), cspec((8, 128)), cspec((1, 128))]
    args += [s0, conv_w, misc, norm_w]
    return pl.pallas_call(
        functools.partial(_gdn_kernel, R=rows, C=C, N=nstep, has_hist=has_hist),
        grid=(nseq, nstep),
        in_specs=in_specs,
        out_specs=[rspec(1024), pl.BlockSpec((1, N_HEADS, 128, 128), lambda b, c: (b, 0, 0, 0))],
        out_shape=[jax.ShapeDtypeStruct((t, 1024), out_dtype),
                   jax.ShapeDtypeStruct((nseq, N_HEADS, 128, 128), F32)],
        scratch_shapes=[pltpu.VMEM((8 + rc, 3072), F32)] + [pltpu.VMEM((128, 128), F32)] * N_HEADS,
        compiler_params=_cp(("arbitrary", "arbitrary")),
        name="gdn",
    )(*args)


def _sort_key(s, col=None):
    bits = pltpu.bitcast(s + 0.0, I32)
    key = bits ^ ((bits >> 31) & 0x7FFFFFFF)
    if col is None:
        return key
    return jnp.where(s == 0.0, -1 - col, key)


def _count_ge(keys_ref, nkt, cand, tq, tk):
    rb = min(128, tq)
    cand_b = jnp.broadcast_to(cand, (tq, 128))
    accs = []
    for r0 in range(0, tq, rb):
        def body(j, acc, r0=r0, cr=cand_b[r0:r0 + rb]):
            kt = keys_ref[r0:r0 + rb, pl.ds(pl.multiple_of(j * tk, tk), tk)]
            for g in range(tk // 128):
                acc = acc + jnp.where(kt[:, g * 128:(g + 1) * 128] >= cr, 1, 0)
            return acc

        accs.append(lax.fori_loop(0, nkt, body, jnp.zeros((rb, 128), I32)))
    acc = accs[0] if len(accs) == 1 else jnp.concatenate(accs, axis=0)
    return jnp.sum(acc, axis=-1, keepdims=True)


def _kth_largest(keys_ref, nkt, rmax, tq, tk, k):
    hi = _sort_key(rmax) + 1
    lo0 = _sort_key(jnp.where(rmax >= 0.0, rmax * 0.125, rmax * 8.0))
    c0 = _count_ge(keys_ref, nkt, lo0, tq, tk)
    ok0 = c0 >= k
    lo = jnp.where(ok0, lo0, KEY_NEG_INF - 1)
    cnt = jnp.where(ok0, c0, nkt * tk)
    hi = jnp.where(ok0, hi, lo0)
    c_pos = _count_ge(keys_ref, nkt, jnp.ones((tq, 1), I32), tq, tk)
    up = (c_pos >= k) & (lo < 1)
    lo = jnp.where(up, 1, lo)
    cnt = jnp.where(up, c_pos, cnt)
    hi = jnp.where((c_pos < k) & (hi > 1), 1, hi)

    def n_active(lo, hi, cnt):
        return jnp.max(jnp.where((cnt != k) & (hi - lo > 1), 1, 0))

    def cond(st):
        return st[3] > 0

    def body(st):
        lo, hi, cnt, _ = st
        mid = lo + ((hi - lo) >> 1)
        c = _count_ge(keys_ref, nkt, mid, tq, tk)
        up = c >= k
        lo = jnp.where(up, mid, lo)
        cnt = jnp.where(up, c, cnt)
        hi = jnp.where(up, hi, mid)
        return lo, hi, cnt, n_active(lo, hi, cnt)

    lo, _, cnt, _ = lax.while_loop(cond, body, (lo, hi, cnt, n_active(lo, hi, cnt)))
    return lo, cnt


def _emit_selection(keys_ref, nkt, thr, cnt, tq, tk, k, emit):
    tied = (cnt > k) & (thr > KEY_NEG_INF)
    any_tied = jnp.max(jnp.where(tied, 1, 0))

    @pl.when(any_tied == 0)
    def _():
        def body(j, carry):
            kt = keys_ref[:, pl.ds(pl.multiple_of(j * tk, tk), tk)]
            emit(j, kt >= thr)
            return carry
        lax.fori_loop(0, nkt, body, 0)

    @pl.when(any_tied != 0)
    def _():
        c_gt = _count_ge(keys_ref, nkt, thr + 1, tq, tk)
        budget = jnp.where(tied, k - c_gt, nkt * tk).astype(F32)
        ri = lax.broadcasted_iota(I32, (tk, tk), 0)
        ci = lax.broadcasted_iota(I32, (tk, tk), 1)
        upper = jnp.where(ri <= ci, 1.0, 0.0).astype(BF16)

        def body(j, seen):
            kt = keys_ref[:, pl.ds(pl.multiple_of(j * tk, tk), tk)]
            eq = jnp.where(kt == thr, 1.0, 0.0)
            rank = _mm(eq.astype(BF16), upper) + seen
            sel = jnp.where(kt > thr, 1.0, jnp.where(rank <= budget, eq, 0.0))
            emit(j, sel > 0.5)
            return seen + jnp.sum(eq, axis=-1, keepdims=True)
        lax.fori_loop(0, nkt, body, jnp.zeros((tq, 1), F32))


def _idx_kernel(qi_ref, small_ref, kb0_ref, kb1_ref, mask_ref, keys_ref, *, TQ, TK, T, k):
    i = pl.program_id(0)
    nkt = (i * TQ + TQ + TK - 1) // TK
    w = small_ref[:, L_WI:L_WI + IDX_HEADS] * IDX_W_SCALE
    pos_q = i * TQ + lax.broadcasted_iota(I32, (TQ, 1), 0)

    def score_tile(j, rmax):
        off = pl.multiple_of(j * TK, TK)
        ka = kb0_ref[pl.ds(off, TK), :]
        kb = kb1_ref[pl.ds(off, TK), :]
        acc = jnp.zeros((TQ, TK), F32)
        for p in range(IDX_HEADS // 2):
            q2 = qi_ref[:, p * 128:(p + 1) * 128]
            acc = acc + w[:, 2 * p:2 * p + 1] * jnp.maximum(_nt(q2, ka), 0.0)
            acc = acc + w[:, 2 * p + 1:2 * p + 2] * jnp.maximum(_nt(q2, kb), 0.0)
        pos_k = off + lax.broadcasted_iota(I32, (1, TK), 1)
        acc = jnp.where(pos_k <= pos_q, acc, NEG_INF)
        keys_ref[:, pl.ds(off, TK)] = _sort_key(acc, pos_k)
        for g in range(TK // 128):
            rmax = jnp.maximum(rmax, acc[:, g * 128:(g + 1) * 128])
        return rmax

    rmax = lax.fori_loop(0, nkt, score_tile, jnp.full((TQ, 128), NEG_INF, F32))
    thr, cnt = _kth_largest(keys_ref, nkt, jnp.max(rmax, axis=-1, keepdims=True), TQ, TK, k)

    def emit(j, sel):
        off = pl.multiple_of(j * TK, TK)
        pos_k = off + lax.broadcasted_iota(I32, (1, TK), 1)
        m = jnp.where(pos_k <= pos_q, jnp.where(sel, 1, 0), 0)
        mask_ref[:, pl.ds(off, TK)] = m.astype(jnp.int8)

    _emit_selection(keys_ref, nkt, thr, cnt, TQ, TK, k, emit)

    def zero_tile(j, carry):
        mask_ref[:, pl.ds(pl.multiple_of(j * TK, TK), TK)] = jnp.zeros((TQ, TK), jnp.int8)
        return carry

    lax.fori_loop(nkt, T // TK, zero_tile, 0)


def prompt_topk_mask(qi16, small, kb0, kb1, k, TQ=256, TK=512):
    t = qi16.shape[0]
    TK = min(TK, t)
    TQ = min(TQ, t)
    return pl.pallas_call(
        functools.partial(_idx_kernel, TQ=TQ, TK=TK, T=t, k=k),
        grid=(t // TQ,),
        in_specs=[pl.BlockSpec((TQ, 512), lambda i: (i, 0)),
                  pl.BlockSpec((TQ, 128), lambda i: (i, 0)),
                  _const_spec((t, 128)), _const_spec((t, 128))],
        out_specs=pl.BlockSpec((TQ, t), lambda i: (i, 0)),
        out_shape=jax.ShapeDtypeStruct((t, t), jnp.int8),
        scratch_shapes=[pltpu.VMEM((TQ, t), I32)],
        compiler_params=_cp(("parallel",)),
        name="prompt_topk_mask",
    )(qi16, small, kb0, kb1)


def _att_kernel(qa_ref, ka_ref, q_ref, k_ref, v_ref, mask_ref, e_ref, o_ref, *scr, TA, RS):
    m_scr, l_scr, acc_scr = scr[0:8], scr[8:16], scr[16:24]
    s_scr, p_scr, b_scr = scr[24:26], scr[26:28], scr[28]
    p = pl.program_id(0)
    i = qa_ref[p]
    j = ka_ref[p]

    @pl.when(j == 0)
    def _():
        for h in range(N_HEADS):
            m_scr[h][...] = jnp.full((TA, 1), -1e30, F32)
            l_scr[h][...] = jnp.zeros((TA, HEAD_DIM), F32)
            acc_scr[h][...] = jnp.zeros((TA, HEAD_DIM), F32)

    for r0 in range(0, TA, 32):
        b_scr[r0:r0 + 32, :] = jnp.where(mask_ref[r0:r0 + 32, :].astype(I32) != 0, 0.0, NEG_INF)

    def update(near):
        nb = TA // 128
        if near:
            fd = jnp.where(j == i, 1.0, 0.0)
        def logits(h):
            hs = slice(h * 128, (h + 1) * 128)
            sb = s_scr[h % 2]
            sb[...] = _nt(q_ref[:, hs], k_ref[:, hs]) + b_scr[...]
            if near:
                for b in range(nb):
                    bs = slice(b * 128, (b + 1) * 128)
                    sb[bs, bs] += fd * e_ref[0, h]
                    if b + 1 < nb:
                        sb[(b + 1) * 128:(b + 2) * 128, bs] += fd * e_ref[1, h]
                sb[0:128, TA - 128:TA] += (1.0 - fd) * e_ref[1, h]

        def row_max(h):
            m_all = m_scr[h][...]
            return m_all, [jnp.maximum(m_all[r0:r0 + RS], jnp.max(s_scr[h % 2][r0:r0 + RS, :], axis=-1, keepdims=True))
                           for r0 in range(0, TA, RS)]

        def probs(h, m_parts):
            for n, r0 in enumerate(range(0, TA, RS)):
                rs = slice(r0, r0 + RS)
                p_scr[h % 2][rs, :] = jnp.exp(s_scr[h % 2][rs, :] - m_parts[n]).astype(BF16)

        def accumulate(h, m_all, m_parts):
            hs = slice(h * 128, (h + 1) * 128)
            m_new = jnp.concatenate(m_parts, axis=0)
            alpha = jnp.exp(m_all - m_new)
            m_scr[h][...] = m_new
            pv = _mm(p_scr[h % 2][...], jnp.concatenate([v_ref[:, hs], jnp.ones((TA, HEAD_DIM), BF16)], axis=1))
            acc_scr[h][...] = alpha * acc_scr[h][...] + pv[:, :HEAD_DIM]
            l_scr[h][...] = alpha * l_scr[h][...] + pv[:, HEAD_DIM:]

        for h in range(N_HEADS):
            logits(h)
            m_all, m_parts = row_max(h)
            probs(h, m_parts)
            accumulate(h, m_all, m_parts)

    @pl.when(j >= i - 1)
    def _():
        update(True)

    @pl.when(j < i - 1)
    def _():
        update(False)

    @pl.when(j == i)
    def _():
        for h in range(N_HEADS):
            o_ref[:, h * 128:(h + 1) * 128] = (acc_scr[h][...] / l_scr[h][...]).astype(o_ref.dtype)


def prompt_attention(q16, k16, v16, mask, e_tab, TA=512, RS=16):
    t = q16.shape[0]
    TA = min(TA, t)
    nb = t // TA
    qa = np.concatenate([np.full(i + 1, i, np.int32) for i in range(nb)])
    ka = np.concatenate([np.arange(i + 1, dtype=np.int32) for i in range(nb)])
    grid_spec = pltpu.PrefetchScalarGridSpec(
        num_scalar_prefetch=2,
        grid=(len(qa),),
        in_specs=[pl.BlockSpec((TA, 1024), lambda p, qa, ka: (qa[p], 0)),
                  pl.BlockSpec((TA, 1024), lambda p, qa, ka: (ka[p], 0)),
                  pl.BlockSpec((TA, 1024), lambda p, qa, ka: (ka[p], 0)),
                  pl.BlockSpec((TA, TA), lambda p, qa, ka: (qa[p], ka[p])),
                  pl.BlockSpec((2, N_HEADS, 128, 128), lambda p, qa, ka: (0, 0, 0, 0))],
        out_specs=pl.BlockSpec((TA, 1024), lambda p, qa, ka: (qa[p], 0)),
        scratch_shapes=([pltpu.VMEM((TA, 1), F32)] * N_HEADS + [pltpu.VMEM((TA, HEAD_DIM), F32)] * (2 * N_HEADS)
                        + [pltpu.VMEM((TA, TA), F32)] * 2 + [pltpu.VMEM((TA, TA), BF16)] * 2
                        + [pltpu.VMEM((TA, TA), F32)]))
    return pl.pallas_call(
        functools.partial(_att_kernel, TA=TA, RS=RS),
        grid_spec=grid_spec,
        out_shape=jax.ShapeDtypeStruct((t, 1024), BF16),
        compiler_params=_cp(("arbitrary",)),
        name="prompt_attention",
    )(jnp.asarray(qa), jnp.asarray(ka), q16, k16, v16, mask, e_tab)


def _page_copies(pt_ref, cache_ref, buf_ref, sem_ref, b, first_page, npages, slot):
    out = []
    for g in range(npages):
        pid = pt_ref[b, first_page + g]
        out.append(pltpu.make_async_copy(cache_ref.at[0, pid], buf_ref.at[slot, pl.ds(g * PAGE, PAGE)],
                                         sem_ref.at[slot]))
    return out


def _sidx_kernel(pt_ref, qm_ref, w_ref, knew_ref, cache_ref, sc_ref, kbuf, sem, *, NP, NB):
    b = pl.program_id(0)
    slot = b % 2

    def copies(bq, sl):
        return [pltpu.make_async_copy(cache_ref.at[0, pt_ref[bq, g]], kbuf.at[sl, :, pl.ds(g * PAGE, PAGE)],
                                      sem.at[sl]) for g in range(NP)]

    @pl.when(b == 0)
    def _():
        for c in copies(0, 0):
            c.start()

    @pl.when(b + 1 < NB)
    def _():
        for c in copies(b + 1, 1 - slot):
            c.start()

    for c in copies(b, slot):
        c.wait()

    qm = qm_ref[0]
    w = w_ref[0]

    def fold(s):
        s = jnp.maximum(s, 0.0) * w
        out = s[0:GROUP]
        for h in range(1, IDX_HEADS):
            out = out + s[h * GROUP:(h + 1) * GROUP]
        return out

    past = NP * PAGE
    CH = 2048
    for c0 in range(0, past, CH):
        kc = kbuf[slot, :, c0:c0 + CH].astype(BF16)
        sc_ref[0, :, c0:c0 + CH] = fold(_mm(qm, kc))
    knew = jnp.concatenate([knew_ref[...], jnp.zeros((128 - GROUP, IDX_DIM), F32)], axis=0).astype(BF16)
    tail = fold(_nt(qm, knew))
    r = lax.broadcasted_iota(I32, (GROUP, 128), 0)
    c = lax.broadcasted_iota(I32, (GROUP, 128), 1)
    sc_ref[0, :, past:past + 128] = jnp.where((c >= FIRST_TOK) & (c <= r), tail, NEG_INF)


def sample_scores(page_table, qm, wcol, ki_s, cache_idx):
    nb, npg = page_table.shape
    past = npg * PAGE
    grid_spec = pltpu.PrefetchScalarGridSpec(
        num_scalar_prefetch=1,
        grid=(nb,),
        in_specs=[pl.BlockSpec((1, IDX_HEADS * GROUP, IDX_DIM), lambda b, pt: (b, 0, 0)),
                  pl.BlockSpec((1, IDX_HEADS * GROUP, 1), lambda b, pt: (b, 0, 0)),
                  pl.BlockSpec((GROUP, IDX_DIM), lambda b, pt: (b, 0)),
                  pl.BlockSpec(memory_space=pl.ANY)],
        out_specs=pl.BlockSpec((1, GROUP, past + 128), lambda b, pt: (b, 0, 0)),
        scratch_shapes=[pltpu.VMEM((2, IDX_DIM, past), F32), pltpu.SemaphoreType.DMA((2,))])
    return pl.pallas_call(
        functools.partial(_sidx_kernel, NP=npg, NB=nb),
        grid_spec=grid_spec,
        out_shape=jax.ShapeDtypeStruct((nb, GROUP, past + 128), F32),
        compiler_params=_cp(("arbitrary",)),
        name="sample_scores",
    )(page_table, qm, wcol, ki_s, jnp.swapaxes(cache_idx, 2, 3))


def _ssel_kernel(sc_ref, am_ref, keys_ref, *, TQ, TK, W, k):
    nkt = W // TK

    def to_keys(j, rmax):
        off = pl.multiple_of(j * TK, TK)
        s = sc_ref[:, pl.ds(off, TK)]
        keys_ref[:, pl.ds(off, TK)] = _sort_key(s, off + lax.broadcasted_iota(I32, (1, TK), 1))
        for g in range(TK // 128):
            rmax = jnp.maximum(rmax, s[:, g * 128:(g + 1) * 128])
        return rmax

    rmax = lax.fori_loop(0, nkt, to_keys, jnp.full((TQ, 128), NEG_INF, F32))
    thr, cnt = _kth_largest(keys_ref, nkt, jnp.max(rmax, axis=-1, keepdims=True), TQ, TK, k)

    def emit(j, sel):
        off = pl.multiple_of(j * TK, TK)
        valid = sc_ref[:, pl.ds(off, TK)] > NEG_INF
        am_ref[:, pl.ds(off, TK)] = jnp.where(valid, jnp.where(sel, 1.0, 0.0), 0.0)

    _emit_selection(keys_ref, nkt, thr, cnt, TQ, TK, k, emit)


def sample_topk_mask(scores, k, TQ=32, TK=128):
    r, w = scores.shape
    TQ = min(TQ, r)
    return pl.pallas_call(
        functools.partial(_ssel_kernel, TQ=TQ, TK=TK, W=w, k=k),
        grid=(r // TQ,),
        in_specs=[pl.BlockSpec((TQ, w), lambda i: (i, 0))],
        out_specs=pl.BlockSpec((TQ, w), lambda i: (i, 0)),
        out_shape=jax.ShapeDtypeStruct((r, w), F32),
        scratch_shapes=[pltpu.VMEM((TQ, w), I32)],
        compiler_params=_cp(("parallel",)),
        name="sample_topk_mask",
    )(scores)


def _satt_kernel(pt_ref, q_ref, sel_ref, knew_ref, vnew_ref, tm_ref, hp_ref, blx_ref, ck_ref, cv_ref, o_ref,
                 kbuf, vbuf, ksem, vsem, m_scr, l_scr, acc_scr, *, PC, NCH, NB):
    b = pl.program_id(0)
    c = pl.program_id(1)
    g = b * NCH + c
    slot = g % 2
    CW = PC * PAGE

    def copies(bq, cq, sl):
        return (_page_copies(pt_ref, ck_ref, kbuf, ksem, bq, cq * PC, PC, sl)
                + _page_copies(pt_ref, cv_ref, vbuf, vsem, bq, cq * PC, PC, sl))

    @pl.when(g == 0)
    def _():
        for cp in copies(0, 0, 0):
            cp.start()

    @pl.when(g + 1 < NB * NCH)
    def _():
        nxt = g + 1
        for cp in copies(nxt // NCH, nxt % NCH, 1 - slot):
            cp.start()

    for cp in copies(b, c, slot):
        cp.wait()

    @pl.when(c == 0)
    def _():
        m_scr[...] = jnp.full(m_scr.shape, -1e30, F32)
        l_scr[...] = jnp.zeros(l_scr.shape, F32)
        acc_scr[...] = jnp.zeros(acc_scr.shape, F32)

    def online(s, v16):
        m_old = m_scr[...]
        m_new = jnp.maximum(m_old, jnp.max(s, axis=-1, keepdims=True))
        alpha = jnp.exp(m_old - m_new)
        pr = jnp.exp(s - m_new)
        l_scr[...] = alpha * l_scr[...] + jnp.sum(pr, axis=-1, keepdims=True)
        acc_scr[...] = alpha * acc_scr[...] + _mm(_b16(pr), v16)
        m_scr[...] = m_new

    NX = CW * N_HEADS
    q64 = q_ref[0]
    last = jnp.where(c == NCH - 1, 1.0, 0.0)
    ri = lax.broadcasted_iota(I32, (PAGE, PAGE * N_HEADS), 0)
    ci = lax.broadcasted_iota(I32, (PAGE, PAGE * N_HEADS), 1)
    spread = jnp.where((ci >> 3) == ri, 1.0, 0.0).astype(BF16)
    sx = _mm(_b16(sel_ref[0].reshape(PC * GROUP, PAGE)), spread)
    selx = jnp.concatenate([sx[p * GROUP:(p + 1) * GROUP] for p in range(PC)], axis=1)
    addm = jnp.where(jnp.concatenate([selx] * N_HEADS, axis=0) > 0.5, hp_ref[...], NEG_INF)
    s = _nt(q64, _b16(kbuf[slot].reshape(NX, HEAD_DIM))) + addm
    s = s + last * jnp.concatenate([jnp.zeros((N_HEADS * GROUP, NX - PAGE * N_HEADS), F32), blx_ref[...]], axis=1)
    online(s, _b16(vbuf[slot].reshape(NX, HEAD_DIM)))

    @pl.when(c == NCH - 1)
    def _():
        zpad = jnp.zeros((128 - N_HEADS * GROUP, HEAD_DIM), F32)
        kn = _b16(jnp.concatenate([knew_ref[0], zpad], axis=0))
        vn = _b16(jnp.concatenate([vnew_ref[0], zpad], axis=0))
        online(_nt(q64, kn) + tm_ref[0], vn)
        out = acc_scr[...] / l_scr[...]
        for h in range(N_HEADS):
            o_ref[:, h * 128:(h + 1) * 128] = out[h * GROUP:(h + 1) * GROUP, :]


def sample_attention(page_table, q64, selp, kn64, vn64, tmask, hp, blx, cache_k, cache_v, PC=8):
    nb, npg = page_table.shape
    PC = min(PC, npg)
    nch = npg // PC
    R = N_HEADS * GROUP
    grid_spec = pltpu.PrefetchScalarGridSpec(
        num_scalar_prefetch=1,
        grid=(nb, nch),
        in_specs=[pl.BlockSpec((1, R, HEAD_DIM), lambda b, c, pt: (b, 0, 0)),
                  pl.BlockSpec((1, PC, GROUP, PAGE), lambda b, c, pt: (b, c, 0, 0)),
                  pl.BlockSpec((1, R, HEAD_DIM), lambda b, c, pt: (b, 0, 0)),
                  pl.BlockSpec((1, R, HEAD_DIM), lambda b, c, pt: (b, 0, 0)),
                  pl.BlockSpec((1, R, 128), lambda b, c, pt: (b, 0, 0)),
                  pl.BlockSpec((R, PC * PAGE * N_HEADS), lambda b, c, pt: (0, 0), pipeline_mode=pl.Buffered(1)),
                  pl.BlockSpec((R, PAGE * N_HEADS), lambda b, c, pt: (0, 0), pipeline_mode=pl.Buffered(1)),
                  pl.BlockSpec(memory_space=pl.ANY),
                  pl.BlockSpec(memory_space=pl.ANY)],
        out_specs=pl.BlockSpec((GROUP, 1024), lambda b, c, pt: (b, 0)),
        scratch_shapes=[pltpu.VMEM((2, PC * PAGE, N_HEADS, HEAD_DIM), F32),
                        pltpu.VMEM((2, PC * PAGE, N_HEADS, HEAD_DIM), F32),
                        pltpu.SemaphoreType.DMA((2,)), pltpu.SemaphoreType.DMA((2,)),
                        pltpu.VMEM((R, 1), F32), pltpu.VMEM((R, 1), F32), pltpu.VMEM((R, HEAD_DIM), F32)])
    return pl.pallas_call(
        functools.partial(_satt_kernel, PC=PC, NCH=nch, NB=nb),
        grid_spec=grid_spec,
        out_shape=jax.ShapeDtypeStruct((nb * GROUP, 1024), F32),
        compiler_params=_cp(("arbitrary", "arbitrary")),
        name="sample_attention",
    )(page_table, q64, selp, kn64, vn64, tmask, hp, blx, cache_k, cache_v)


def _mix_kernel(oa_ref, ob_ref, ga_ref, gb_ref, x_ref, g1_ref, wa_ref, wb_ref, wo_ref, o_ref):
    ya = _mm(oa_ref[...].astype(BF16), wa_ref[...])
    yb = _mm(ob_ref[...].astype(BF16), wb_ref[...])
    mixed = ga_ref[...] * ya + gb_ref[...] * yb
    o_ref[...] = x_ref[...] + g1_ref[...] * _mm(mixed.astype(BF16), wo_ref[...])


def mix(oa, ob, ga, gb, x, g1, wa, wb, wo, tm=512):
    t = x.shape[0]
    tm = min(tm, t)
    per_row = g1.shape[0] != 1
    mod_spec = (pl.BlockSpec((tm, D_MODEL), lambda i: (i, 0)) if per_row
                else pl.BlockSpec((1, D_MODEL), lambda i: (0, 0)))
    row = pl.BlockSpec((tm, D_MODEL), lambda i: (i, 0))
    wspec = _const_spec((D_MODEL, D_MODEL))
    return pl.pallas_call(
        _mix_kernel,
        grid=(t // tm,),
        in_specs=[row, row, row, row, row, mod_spec, wspec, wspec, wspec],
        out_specs=row,
        out_shape=jax.ShapeDtypeStruct((t, D_MODEL), F32),
        compiler_params=_cp(("parallel",)),
        name="mix",
    )(oa, ob, ga, gb, x, g1, wa, wb, wo)


def _ffn_kernel(*refs, TM, has_hist):
    if has_hist:
        (x_ref, sh_ref, sc_ref, g2_ref, hist_ref, n2_ref, wup_ref, cw_ref, cb_ref, wdn_ref, fw_ref,
         y_ref, u_ref, ubuf) = refs
    else:
        (x_ref, sh_ref, sc_ref, g2_ref, n2_ref, wup_ref, cw_ref, cb_ref, wdn_ref, fw_ref,
         y_ref, u_ref, ubuf) = refs
    i = pl.program_id(0)

    @pl.when(i == 0)
    def _():
        ubuf[0:8, :] = jnp.zeros((8, 2 * D_FF), F32)

    x = x_ref[...]
    h = x * lax.rsqrt(jnp.mean(x * x, -1, keepdims=True) + EPS) * n2_ref[...]
    h = h * (1.0 + sc_ref[...]) + sh_ref[...]
    hb = h.astype(BF16)
    if has_hist:
        is_hist = (lax.broadcasted_iota(I32, (TM, 1), 0) % GROUP) < FIRST_TOK
    CB = 256
    acc = jnp.zeros((TM, D_MODEL), F32)
    def up(c):
        us = []
        for base in (c, D_FF + c):
            cs = slice(base, base + CB)
            u = _mm(hb, wup_ref[:, cs])
            if has_hist:
                u = jnp.where(is_hist, hist_ref[:, cs], u)
            ubuf[8:8 + TM, cs] = u
            us.append(u)
        return us

    starts = list(range(0, D_FF, CB))
    nxt = up(starts[0])
    for n, c in enumerate(starts):
        cur = nxt
        if n + 1 < len(starts):
            nxt = up(starts[n + 1])
        halves = []
        for u, base in zip(cur, (c, D_FF + c)):
            cs = slice(base, base + CB)
            halves.append(cw_ref[2:3, cs] * u + cw_ref[1:2, cs] * ubuf[7:7 + TM, cs]
                          + cw_ref[0:1, cs] * ubuf[6:6 + TM, cs] + cb_ref[:, cs])
        act = _silu(halves[0]) * halves[1]
        acc = acc + _mm(act.astype(BF16), wdn_ref[c:c + CB, :])
    if has_hist:
        u_ref[...] = ubuf[8:8 + TM, :]
    else:
        u_ref[...] = ubuf[TM:TM + 8, :]
    ubuf[0:8, :] = ubuf[TM:TM + 8, :]
    x2 = x + g2_ref[...] * acc
    y_ref[...] = x2 * lax.rsqrt(jnp.mean(x2 * x2, -1, keepdims=True) + EPS) * fw_ref[...]


def ffn(x1, sh, sc, g2, hist, norm2_w, w_up, conv_w, conv_b, w_down, final_w, tm=256):
    t = x1.shape[0]
    tm = min(tm, t)
    has_hist = hist is not None
    per_row = sh.shape[0] != 1
    mod_spec = (pl.BlockSpec((tm, D_MODEL), lambda i: (i, 0)) if per_row
                else pl.BlockSpec((1, D_MODEL), lambda i: (0, 0)))
    row = pl.BlockSpec((tm, D_MODEL), lambda i: (i, 0))
    in_specs = [row, mod_spec, mod_spec, mod_spec]
    args = [x1, sh, sc, g2]
    if has_hist:
        in_specs.append(pl.BlockSpec((tm, 2 * D_FF), lambda i: (i, 0)))
        args.append(hist)
    in_specs += [_const_spec((1, D_MODEL)), _const_spec((D_MODEL, 2 * D_FF)), _const_spec((FFN_CONV, 2 * D_FF)),
                 _const_spec((1, 2 * D_FF)), _const_spec((D_FF, D_MODEL)), _const_spec((1, D_MODEL))]
    args += [norm2_w, w_up, conv_w, conv_b, w_down, final_w]
    if has_hist:
        u_spec = pl.BlockSpec((tm, 2 * D_FF), lambda i: (i, 0))
        u_shape = jax.ShapeDtypeStruct((t, 2 * D_FF), F32)
    else:
        u_spec = pl.BlockSpec((8, 2 * D_FF), lambda i: (0, 0))
        u_shape = jax.ShapeDtypeStruct((8, 2 * D_FF), F32)
    return pl.pallas_call(
        functools.partial(_ffn_kernel, TM=tm, has_hist=has_hist),
        grid=(t // tm,),
        in_specs=in_specs,
        out_specs=[row, u_spec],
        out_shape=[jax.ShapeDtypeStruct((t, D_MODEL), F32), u_shape],
        scratch_shapes=[pltpu.VMEM((8 + tm, 2 * D_FF), F32)],
        compiler_params=_cp(("arbitrary",)),
        name="ffn",
    )(*args)


def _group_rows(a, first):
    b, n, c = a.shape
    return jnp.pad(a, ((0, 0), (first, GROUP - first - n), (0, 0))).reshape(b * GROUP, c)


def kernel(x_prompt, x_sample, cache_k, cache_v, cache_idx_k, state_gdn, state_gdn_conv, state_ffn_conv,
           page_table, c_prompt, c_sample, w_ada, b_ada, norm1_w, w_in, gdn_conv_w, gdn_A_log, gdn_dt_bias,
           gdn_norm_w, idx_knorm_w, idx_knorm_b, w_branch_a, w_branch_b, w_out, norm2_w, w_up, ffn_conv_w,
           ffn_conv_b, w_down, rel_bias, final_norm_w):
    bp, tp, _ = x_prompt.shape
    bs, ts, _ = x_sample.shape
    assert bp == 1 and ts == GROUP - FIRST_TOK and w_ada.shape[0] == 1
    npg = page_table.shape[1]
    past = npg * PAGE

    w = w_in[0]
    w_cat = jnp.concatenate(
        [w[:, 0:4096], w[:, 4112:7184], w[:, 7768:9816], w[:, 7184:7696],
         w[:, 7696:7760], w[:, 4096:4112], w[:, 7760:7768], jnp.zeros((D_MODEL, 40), F32),
         jnp.zeros((D_MODEL, 64), F32), w[:, 7696:7760]], axis=1).astype(BF16)
    w_smt = w_cat[:, C_SM:C_SM2].T
    z64 = jnp.zeros((IDX_DIM,), F32)
    lnw2 = jnp.stack([jnp.concatenate([idx_knorm_w[0], z64]), jnp.concatenate([z64, idx_knorm_w[0]])])
    lnb2 = jnp.stack([jnp.concatenate([idx_knorm_b[0], z64]), jnp.concatenate([z64, idx_knorm_b[0]])])
    misc = jnp.zeros((8, 128), F32)
    misc = misc.at[0, L_A:L_A + N_HEADS].set(gdn_A_log[0]).at[1, L_A:L_A + N_HEADS].set(gdn_dt_bias[0])
    wa16, wb16, wo16 = w_branch_a[0].astype(BF16), w_branch_b[0].astype(BF16), w_out[0].astype(BF16)
    wup16, wdn16 = w_up[0].astype(BF16), w_down[0].astype(BF16)
    n1 = norm1_w[0].reshape(1, D_MODEL)
    n2 = norm2_w[0].reshape(1, D_MODEL)
    fw = final_norm_w.reshape(1, D_MODEL)
    gnw = gdn_norm_w[0].reshape(1, 128)
    ffn_b = ffn_conv_b[0].reshape(1, 2 * D_FF)

    c_all = jnp.concatenate([c_prompt, c_sample], axis=0)
    pad_r = (-c_all.shape[0]) % 8
    mod = ada_mod(jnp.pad(c_all, ((0, pad_r), (0, 0))), w_ada[0], b_ada[0])
    mod_p = mod[0:1]
    mod_s = jnp.repeat(mod[1:1 + bs], GROUP, axis=0)

    def mods(m):
        return [m[:, i * D_MODEL:(i + 1) * D_MODEL] for i in range(6)]

    e_tab, b_last, b_tail = bias_tables(rel_bias)

    xp = x_prompt[0]
    sh1, sc1, g1, sh2, sc2, g2 = mods(mod_p)
    (conv_p, gz_p, q_p, kf_p, vf_p, k16_p, v16_p, ga_p, gb_p, qi_p, small_p, ki_p, kb0_p, kb1_p, smt_p) = in_proj(
        xp, sh1, sc1, n1, w_cat, w_smt, lnw2, lnb2)
    oa_p, s_p = gdn(conv_p, None, small_p, smt_p, gz_p, jnp.zeros((1, N_HEADS, 128, 128), F32), gdn_conv_w[0],
                    misc, gnw, nseq=1, rows=min(4 * GDN_CHUNK, tp), out_dtype=BF16)
    mask_p = prompt_topk_mask(qi_p, small_p, kb0_p, kb1_p, min(TOPK_MAX, tp // 4))
    ob_p = prompt_attention(q_p, k16_p, v16_p, mask_p, e_tab)
    x1_p = mix(oa_p, ob_p, ga_p, gb_p, xp, g1, wa16, wb16, wo16)
    y_p, utail_p = ffn(x1_p, sh2, sc2, g2, None, n2, wup16, ffn_conv_w[0], ffn_b, wdn16, fw)

    xs = _group_rows(x_sample, FIRST_TOK)
    sh1, sc1, g1, sh2, sc2, g2 = mods(mod_s)
    (conv_s, gz_s, q_s, kf_s, vf_s, _, _, ga_s, gb_s, qi_s, small_s, ki_s, _, _, _) = in_proj(
        xs, sh1, sc1, n1, w_cat, w_smt, lnw2, lnb2)
    hist_gdn = _group_rows(state_gdn_conv[0], FIRST_TOK - (GDN_CONV - 1))
    oa_s, s_s = gdn(conv_s, hist_gdn, small_s, None, gz_s, state_gdn[0], gdn_conv_w[0], misc, gnw,
                    nseq=bs, rows=GROUP, out_dtype=F32)
    qm = qi_s.reshape(bs, GROUP, IDX_HEADS, IDX_DIM).transpose(0, 2, 1, 3).reshape(bs, IDX_HEADS * GROUP, IDX_DIM)
    wcol = (small_s[:, L_WI:L_WI + IDX_HEADS] * IDX_W_SCALE).reshape(bs, GROUP, IDX_HEADS)
    wcol = wcol.transpose(0, 2, 1).reshape(bs, IDX_HEADS * GROUP, 1)
    scores = sample_scores(page_table, qm, wcol, ki_s, cache_idx_k)
    sc_tok = scores[:, FIRST_TOK:, :].reshape(bs * ts, past + 128)
    sel = sample_topk_mask(sc_tok, min(TOPK_MAX, (past + ts) // 4))
    sel = jnp.pad(sel.reshape(bs, ts, past + 128), ((0, 0), (FIRST_TOK, 0), (0, 0)), constant_values=1.0)
    selp = sel[:, :, :past].reshape(bs, GROUP, npg, PAGE).transpose(0, 2, 1, 3)
    heads = jnp.arange(N_HEADS)
    ok = (sel[:, None, :, past:past + GROUP, None] > 0.5) & (heads[:, None] == heads[None, :])[None, :, None, None, :]
    bt = b_tail.reshape(N_HEADS, GROUP, 128)[None, :, :, :GROUP, None]
    tmask = jnp.where(ok, bt, NEG_INF).reshape(bs, N_HEADS * GROUP, GROUP * N_HEADS)
    tmask = jnp.pad(tmask, ((0, 0), (0, 0), (0, 128 - GROUP * N_HEADS)), constant_values=NEG_INF)
    nx = min(8, npg) * PAGE * N_HEADS
    hp = jnp.asarray(np.where((np.arange(nx)[None, :] & 7) == (np.arange(N_HEADS * GROUP)[:, None] >> 3),
                              0.0, NEG_INF).astype(np.float32))
    blx = jnp.repeat(b_last, N_HEADS, axis=1)
    q64 = q_s.reshape(bs, GROUP, N_HEADS, HEAD_DIM).transpose(0, 2, 1, 3).reshape(bs, N_HEADS * GROUP, HEAD_DIM)
    ob_s = sample_attention(page_table, q64, selp, kf_s.reshape(bs, GROUP * N_HEADS, HEAD_DIM),
                            vf_s.reshape(bs, GROUP * N_HEADS, HEAD_DIM), tmask, hp, blx, cache_k, cache_v)
    x1_s = mix(oa_s, ob_s, ga_s, gb_s, xs, g1, wa16, wb16, wo16)
    hist_ffn = _group_rows(state_ffn_conv[0], FIRST_TOK - (FFN_CONV - 1))
    y_s, u_s = ffn(x1_s, sh2, sc2, g2, hist_ffn, n2, wup16, ffn_conv_w[0], ffn_b, wdn16, fw)

    def tok(a):
        return a.reshape(bs, GROUP, -1)[:, FIRST_TOK:, :]

    hd = (N_HEADS, HEAD_DIM)
    return (y_p[None], tok(y_s),
            kf_p.reshape((1, 1, tp) + hd), vf_p.reshape((1, 1, tp) + hd), ki_p.reshape(1, 1, tp, IDX_DIM),
            tok(kf_s).reshape((1, bs, ts) + hd), tok(vf_s).reshape((1, bs, ts) + hd), tok(ki_s)[None],
            s_p[None], s_s[None],
            conv_p[tp - (GDN_CONV - 1):][None, None],
            conv_s.reshape(bs, GROUP, -1)[:, GROUP - (GDN_CONV - 1):][None],
            utail_p[8 - (FFN_CONV - 1):][None, None],
            u_s.reshape(bs, GROUP, -1)[:, GROUP - (FFN_CONV - 1):][None])
```
